```python
import math
import jax
import jax.numpy as jnp
from jax import lax
import numpy as np

D_MODEL = 4096
BATCH = 2
SEQ = 8192
DEPTH = 4

GRID_W = 64
CTX_LEN = 256
Q_BLOCK = 128
ROPE_THETA = 10000.0
EPS = 1e-6

HEAD_DIM = 128
GQA_HEADS = 8
GQA_KV_HEADS = 2
GQA_GROUP = GQA_HEADS // GQA_KV_HEADS

HYENA_WIDTH = 1024
HYENA_ORDER = 2
HYENA_EMB = 33
HYENA_BANDS = (HYENA_EMB - 1) // 2
HYENA_FILTER_WIDTH = 64
HYENA_DECAY_TARGET = 1e-2
HYENA_DECAY_PCT_MIN = 0.3
HYENA_DECAY_PCT_MAX = 1.5

MLA_HEADS = 8
MLA_NOPE = 128
MLA_ROPE = 64
MLA_QK = MLA_NOPE + MLA_ROPE
MLA_V = 128
MLA_KV_RANK = 512

N_BRANCH = 3
BRANCH_WIDTH = 1024
GATE_RANK = 256
ADA_RANK = 256
N_MOD = 6
FFN_HIDDEN = ((8 * D_MODEL + 3 * 256 - 1) // (3 * 256)) * 256

IN_WIDTHS = (
    GQA_HEADS * HEAD_DIM,
    GQA_KV_HEADS * HEAD_DIM,
    GQA_KV_HEADS * HEAD_DIM,
    3 * HYENA_WIDTH,
    MLA_HEADS * MLA_QK,
    MLA_KV_RANK + MLA_ROPE,
    GATE_RANK,
)
IN_COLS = sum(IN_WIDTHS)
IN_SPLITS = tuple(int(s) for s in np.cumsum(IN_WIDTHS)[:-1])

kernel_name = 'hybrid_gqa_hyena_mla_dit_trunk'


def rms_norm(x, g):
    x32 = x.astype(jnp.float32)
    y = x32 * lax.rsqrt(jnp.mean(x32 * x32, axis=-1, keepdims=True) + EPS)
    return (y * g.astype(jnp.float32)).astype(x.dtype)


def modulate(h, shift, scale):
    return h * (1.0 + scale) + shift


def rope_1d(x, pos):
    half = x.shape[-1] // 2
    freqs = ROPE_THETA ** (-jnp.arange(half, dtype=jnp.float32) / half)
    ang = pos[:, None] * freqs[None, :]
    cos = jnp.cos(ang)[None, :, None, :]
    sin = jnp.sin(ang)[None, :, None, :]
    x32 = x.astype(jnp.float32)
    x1, x2 = x32[..., :half], x32[..., half:]
    return jnp.concatenate([x1 * cos - x2 * sin, x1 * sin + x2 * cos], axis=-1).astype(x.dtype)


def axial_rope(x, row, col):
    half = x.shape[-1] // 2
    return jnp.concatenate([rope_1d(x[..., :half], row), rope_1d(x[..., half:], col)], axis=-1)


def block_attention(q, k, v):
    b, lq, kvh, g, dq = q.shape
    nb = lq // Q_BLOCK
    scale = dq ** -0.5
    qb = jnp.moveaxis(q.reshape(b, nb, Q_BLOCK, kvh, g, dq), 1, 0)

    def one_block(qi):
        s = jnp.einsum('bqkgd,bskd->bkgqs', qi, k, preferred_element_type=jnp.float32) * scale
        p = jax.nn.softmax(s, axis=-1).astype(v.dtype)
        return jnp.einsum('bkgqs,bskd->bqkgd', p, v)

    o = lax.map(one_block, qb)
    return jnp.moveaxis(o, 0, 1).reshape(b, lq, kvh * g * v.shape[-1])


def gqa_heads(pq, pk, pv, q_g, k_g, pos):
    b, l = pq.shape[:2]
    q = rms_norm(pq.reshape(b, l, GQA_HEADS, HEAD_DIM), q_g)
    k = rms_norm(pk.reshape(b, l, GQA_KV_HEADS, HEAD_DIM), k_g)
    v = pv.reshape(b, l, GQA_KV_HEADS, HEAD_DIM)
    if pos is not None:
        q = axial_rope(q, *pos)
        k = axial_rope(k, *pos)
    return q.reshape(b, l, GQA_KV_HEADS, GQA_GROUP, HEAD_DIM), k, v


def mla_heads(pq, pkv, kv_g, w_kvb, q_g, k_g, pos):
    b, l = pq.shape[:2]
    q = pq.reshape(b, l, MLA_HEADS, MLA_QK)
    c_kv, k_rope = jnp.split(pkv, [MLA_KV_RANK], axis=-1)
    kv = (rms_norm(c_kv, kv_g) @ w_kvb).reshape(b, l, MLA_HEADS, MLA_NOPE + MLA_V)
    k_nope, v = jnp.split(kv, [MLA_NOPE], axis=-1)
    k = jnp.concatenate([k_nope, jnp.broadcast_to(k_rope[:, :, None, :], (b, l, MLA_HEADS, MLA_ROPE))], axis=-1)
    q = rms_norm(q, q_g)
    k = rms_norm(k, k_g)
    if pos is not None:
        q = jnp.concatenate([q[..., :MLA_NOPE], axial_rope(q[..., MLA_NOPE:], *pos)], axis=-1)
        k = jnp.concatenate([k[..., :MLA_NOPE], axial_rope(k[..., MLA_NOPE:], *pos)], axis=-1)
    return q[:, :, :, None, :], k, v


def short_conv3(u, w, b):
    up = jnp.pad(u, ((0, 0), (1, 1), (0, 0)))
    return up[:, :-2] * w[0] + up[:, 1:-1] * w[1] + up[:, 2:] * w[2] + b


def hyena_filters(length, w1, b1, w2, b2, w3, freq):
    f32 = jnp.float32
    t = jnp.arange(length, dtype=f32)
    t_unit = t / max(length - 1, 1)
    bands = jnp.linspace(1e-4, HYENA_BANDS - 1, HYENA_BANDS, dtype=f32)
    ang = (2.0 * math.pi / length) * t[:, None] * bands[None, :]
    feats = jnp.concatenate([t_unit[:, None], jnp.cos(ang), -jnp.sin(ang)], axis=-1)
    h = jnp.sin(freq[0] * (feats @ w1 + b1))
    h = jnp.sin(freq[1] * (h @ w2 + b2))
    h = (h @ w3).reshape(length, HYENA_ORDER, 2, HYENA_WIDTH)
    deltas = jnp.abs(jnp.linspace(math.log(HYENA_DECAY_TARGET) / HYENA_DECAY_PCT_MIN,
                                  math.log(HYENA_DECAY_TARGET) / HYENA_DECAY_PCT_MAX,
                                  HYENA_WIDTH, dtype=f32))
    h = h * jnp.exp(-t_unit[:, None] * deltas[None, :])[:, None, None, :]
    fwd, bwd = h[:, :, 0], h[:, :, 1]
    taps = jnp.concatenate([fwd, jnp.zeros_like(fwd[:1]), jnp.flip(bwd[1:], axis=0)], axis=0)
    return taps * lax.rsqrt(jnp.sum(taps * taps, axis=0, keepdims=True) + EPS)


def fft_long_conv(z, taps):
    length = z.shape[1]
    n_fft = 2 * length
    zf = jnp.fft.rfft(z, n=n_fft, axis=1)
    tf = jnp.fft.rfft(taps, n=n_fft, axis=0)
    return jnp.fft.irfft(zf * tf[None], n=n_fft, axis=1)[:, :length]


def hyena_mixer(u, conv_w, conv_b, fw1, fb1, fw2, fb2, fw3, ffreq, skip):
    f32 = jnp.float32
    length = u.shape[1]
    u = short_conv3(u, conv_w, conv_b)
    v, x1, x2 = jnp.split(u, 3, axis=-1)
    taps = hyena_filters(length, fw1.astype(f32), fb1.astype(f32), fw2.astype(f32),
                         fb2.astype(f32), fw3.astype(f32), ffreq.astype(f32))
    skip = skip.astype(f32)
    z = v.astype(f32)
    for n, gate in enumerate((x1, x2)):
        z = gate.astype(f32) * (fft_long_conv(z, taps[:, n]) + skip[n] * z)
    return z.astype(u.dtype)


def merge_branches(branches, gate_lr, w_gate_up, b_gate, w_branch, w_out):
    y = None
    for n, br in enumerate(branches):
        g = jax.nn.sigmoid(gate_lr @ w_gate_up[n] + b_gate[n])
        term = g * (br @ w_branch[n])
        y = term if y is None else y + term
    return y @ w_out


def swiglu(h, w_gu, w_down):
    g, u = jnp.split(h @ w_gu, 2, axis=-1)
    return (jax.nn.silu(g) * u) @ w_down


def setup_inputs(seed: int = 0) -> dict:
    key = jax.random.key(seed)
    ks = iter(jax.random.split(key, 40))

    def nrm(shape, scale):
        return jax.random.normal(next(ks), shape, jnp.float32) * scale

    def gain(shape):
        return 1.0 + nrm(shape, 0.02)

    L, D = DEPTH, D_MODEL
    return {
        'x': nrm((BATCH, SEQ, D), 1.0),
        'c': nrm((BATCH, D), 1.0),
        'ctx': nrm((BATCH, CTX_LEN, D), 1.0),
        'c_ctx': nrm((D,), 1.0),
        'norm1_g': gain((L, D)),
        'norm2_g': gain((L, D)),
        'ada_down': nrm((L, D, ADA_RANK), D ** -0.5),
        'ada_up': nrm((L, ADA_RANK, N_MOD * D), 0.5 * ADA_RANK ** -0.5),
        'ada_b': nrm((L, N_MOD * D), 0.02),
        'w_in': nrm((L, D, IN_COLS), D ** -0.5),
        'gqa_q_norm': gain((L, HEAD_DIM)),
        'gqa_k_norm': gain((L, HEAD_DIM)),
        'hy_conv_w': nrm((L, 3, 3 * HYENA_WIDTH), 3 ** -0.5),
        'hy_conv_b': nrm((L, 3 * HYENA_WIDTH), 0.02),
        'hf_w1': nrm((L, HYENA_EMB, HYENA_FILTER_WIDTH), HYENA_EMB ** -0.5),
        'hf_b1': nrm((L, HYENA_FILTER_WIDTH), 0.02),
        'hf_w2': nrm((L, HYENA_FILTER_WIDTH, HYENA_FILTER_WIDTH), HYENA_FILTER_WIDTH ** -0.5),
        'hf_b2': nrm((L, HYENA_FILTER_WIDTH), 0.02),
        'hf_w3': nrm((L, HYENA_FILTER_WIDTH, HYENA_ORDER * 2 * HYENA_WIDTH), HYENA_FILTER_WIDTH ** -0.5),
        'hf_freq': gain((L, 2, HYENA_FILTER_WIDTH)),
        'hy_skip': nrm((L, HYENA_ORDER, HYENA_WIDTH), 0.1),
        'mla_kv_norm': gain((L, MLA_KV_RANK)),
        'mla_w_kvb': nrm((L, MLA_KV_RANK, MLA_HEADS * (MLA_NOPE + MLA_V)), MLA_KV_RANK ** -0.5),
        'mla_q_norm': gain((L, MLA_QK)),
        'mla_k_norm': gain((L, MLA_QK)),
        'w_gate_up': nrm((L, N_BRANCH, GATE_RANK, D), GATE_RANK ** -0.5),
        'b_gate': nrm((L, N_BRANCH, D), 0.02),
        'w_branch': nrm((L, N_BRANCH, BRANCH_WIDTH, D), BRANCH_WIDTH ** -0.5),
        'w_out': nrm((L, D, D), D ** -0.5),
        'ffn_w_gu': nrm((L, D, 2 * FFN_HIDDEN), D ** -0.5),
        'ffn_w_down': nrm((L, FFN_HIDDEN, D), FFN_HIDDEN ** -0.5),
    }


def reference(x, c, ctx, c_ctx, norm1_g, norm2_g, ada_down, ada_up, ada_b, w_in,
              gqa_q_norm, gqa_k_norm, hy_conv_w, hy_conv_b, hf_w1, hf_b1, hf_w2, hf_b2, hf_w3,
              hf_freq, hy_skip, mla_kv_norm, mla_w_kvb, mla_q_norm, mla_k_norm,
              w_gate_up, b_gate, w_branch, w_out, ffn_w_gu, ffn_w_down):
    b, s, _ = x.shape
    rows = s // GRID_W
    row_pos = jnp.repeat(jnp.arange(rows, dtype=jnp.float32), GRID_W)
    col_pos = jnp.tile(jnp.arange(GRID_W, dtype=jnp.float32), rows)
    pos = (row_pos, col_pos)
    cond = jax.nn.silu(jnp.concatenate([c, c_ctx[None, :]], axis=0))

    for l in range(DEPTH):
        last = l == DEPTH - 1
        mod = (cond @ ada_down[l]) @ ada_up[l] + ada_b[l]
        m_lat = jnp.split(mod[:b, None, :], N_MOD, axis=-1)
        m_ctx = jnp.split(mod[b], N_MOD, axis=-1)

        h_lat = modulate(rms_norm(x, norm1_g[l]), m_lat[0], m_lat[1])
        h_ctx = modulate(rms_norm(ctx, norm1_g[l]), m_ctx[0], m_ctx[1])
        p_lat = jnp.split(h_lat @ w_in[l], IN_SPLITS, axis=-1)
        p_ctx = jnp.split(h_ctx @ w_in[l], IN_SPLITS, axis=-1)

        qa_l, ka_l, va_l = gqa_heads(p_lat[0], p_lat[1], p_lat[2], gqa_q_norm[l], gqa_k_norm[l], pos)
        qa_c, ka_c, va_c = gqa_heads(p_ctx[0], p_ctx[1], p_ctx[2], gqa_q_norm[l], gqa_k_norm[l], None)
        a_lat = block_attention(qa_l, jnp.concatenate([ka_l, ka_c], axis=1), jnp.concatenate([va_l, va_c], axis=1))

        hy_lat = hyena_mixer(p_lat[3], hy_conv_w[l], hy_conv_b[l], hf_w1[l], hf_b1[l], hf_w2[l], hf_b2[l],
                             hf_w3[l], hf_freq[l], hy_skip[l])

        qc_l, kc_l, vc_l = mla_heads(p_lat[4], p_lat[5], mla_kv_norm[l], mla_w_kvb[l], mla_q_norm[l], mla_k_norm[l], pos)
        qc_c, kc_c, vc_c = mla_heads(p_ctx[4], p_ctx[5], mla_kv_norm[l], mla_w_kvb[l], mla_q_norm[l], mla_k_norm[l], None)
        c_lat = block_attention(qc_l, jnp.concatenate([kc_l, kc_c], axis=1), jnp.concatenate([vc_l, vc_c], axis=1))

        mix_lat = merge_branches((a_lat, hy_lat, c_lat), p_lat[6], w_gate_up[l], b_gate[l], w_branch[l], w_out[l])
        x = x + m_lat[2] * mix_lat

        if not last:
            a_ctx = block_attention(qa_c, ka_c, va_c)
            hy_ctx = hyena_mixer(p_ctx[3], hy_conv_w[l], hy_conv_b[l], hf_w1[l], hf_b1[l], hf_w2[l], hf_b2[l],
                                 hf_w3[l], hf_freq[l], hy_skip[l])
            mla_ctx = block_attention(qc_c, kc_c, vc_c)
            mix_ctx = merge_branches((a_ctx, hy_ctx, mla_ctx), p_ctx[6], w_gate_up[l], b_gate[l], w_branch[l], w_out[l])
            ctx = ctx + m_ctx[2] * mix_ctx
            h2_ctx = modulate(rms_norm(ctx, norm2_g[l]), m_ctx[3], m_ctx[4])
            ctx = ctx + m_ctx[5] * swiglu(h2_ctx, ffn_w_gu[l], ffn_w_down[l])

        h2_lat = modulate(rms_norm(x, norm2_g[l]), m_lat[3], m_lat[4])
        x = x + m_lat[5] * swiglu(h2_lat, ffn_w_gu[l], ffn_w_down[l])

    return x
```

```python
import functools
import math

import jax
import jax.numpy as jnp
import numpy as np
from jax import lax
from jax.experimental import pallas as pl
from jax.experimental.pallas import tpu as pltpu

f32 = jnp.float32
bf16 = jnp.bfloat16

GRID_W = 64
ROPE_THETA = 10000.0
EPS = 1e-6
HEAD_DIM = 128
GQA_HEADS = 8
GQA_KV_HEADS = 2
GQA_GROUP = GQA_HEADS // GQA_KV_HEADS
HY_C = 1024
HY_ORDER = 2
HY_EMB = 33
HY_BANDS = (HY_EMB - 1) // 2
HY_FW = 64
HY_DECAY_TARGET = 1e-2
HY_DECAY_PCT_MIN = 0.3
HY_DECAY_PCT_MAX = 1.5
MLA_HEADS = 8
MLA_NOPE = 128
MLA_ROPE = 64
MLA_QK = MLA_NOPE + MLA_ROPE
MLA_V = 128
MLA_RANK = 512
N_BRANCH = 3
BRANCH_W = 1024
GATE_RANK = 256
ADA_RANK = 256
N_MOD = 6
IN_COLS = 6976

V7X_LANES = 128
V7X_SUBLANES = 8
V7X_VMEM_BYTES = 64 * 1024 * 1024
MIB = 1024 * 1024

MLA_QPAD = 2 * V7X_LANES
QC_OFF = 0
QA_OFF = QC_OFF + MLA_HEADS * MLA_QPAD
KA_OFF = QA_OFF + GQA_HEADS * HEAD_DIM
VA_OFF = KA_OFF + GQA_KV_HEADS * HEAD_DIM
HY_OFF = VA_OFF + GQA_KV_HEADS * HEAD_DIM
CKV_OFF = HY_OFF + 3 * HY_C
GATE_OFF = CKV_OFF + MLA_RANK
KR_OFF = GATE_OFF + GATE_RANK
NP_COLS = 7680

LOG2E = math.log2(math.e)


def _cparams(sem, vmem_mib):
    return pltpu.CompilerParams(dimension_semantics=sem, vmem_limit_bytes=int(vmem_mib * MIB))


def _row_tile(s, nctx, cands):
    for t in cands:
        if s % t == 0 and nctx % t == 0:
            return t
    raise ValueError("no row tile fits")


def _ada_kernel(cond_ref, down_ref, up_ref, b_ref, o_ref, t_ref):
    @pl.when(pl.program_id(1) == 0)
    def _():
        c = cond_ref[...]
        c = c * (1.0 / (1.0 + jnp.exp(-c)))
        t_ref[...] = jnp.dot(c, down_ref[...], precision=lax.Precision.HIGHEST, preferred_element_type=f32)

    o_ref[...] = jnp.dot(t_ref[...], up_ref[...], precision=lax.Precision.HIGHEST,
                         preferred_element_type=f32) + b_ref[...]


def ada_modulation(cond8, ada_down, ada_up, ada_b):
    depth, d, _ = ada_down.shape
    n = ada_up.shape[2]
    tn = 2048 if n % 2048 == 0 else 512
    return pl.pallas_call(
        _ada_kernel,
        grid=(depth, n // tn),
        in_specs=[
            pl.BlockSpec((8, d), lambda l, j: (0, 0)),
            pl.BlockSpec((None, d, ADA_RANK), lambda l, j: (l, 0, 0)),
            pl.BlockSpec((None, ADA_RANK, tn), lambda l, j: (l, 0, j)),
            pl.BlockSpec((None, 1, tn), lambda l, j: (l, 0, j)),
        ],
        out_specs=pl.BlockSpec((None, 8, tn), lambda l, j: (l, 0, j)),
        out_shape=jax.ShapeDtypeStruct((depth, 8, n), f32),
        scratch_shapes=[pltpu.VMEM((8, ADA_RANK), f32)],
        compiler_params=_cparams(("arbitrary", "arbitrary"), 40),
        name="ada_modulation",
    )(cond8, ada_down, ada_up, ada_b.reshape(depth, 1, n))


def _rope_table_kernel(pr_ref, pc_ref, ca_ref, sa_ref, cc_ref, sc_ref):
    shape = ca_ref.shape
    lane = lax.broadcasted_iota(jnp.int32, shape, 1)
    pr = jnp.broadcast_to(pr_ref[...], shape)
    pc = jnp.broadcast_to(pc_ref[...], shape)
    log_theta = math.log(ROPE_THETA)
    fa = jnp.exp((lane & 31).astype(f32) * (-log_theta / 32.0))
    ang = jnp.where(lane < 64, pr, pc) * fa
    ca_ref[...] = jnp.cos(ang)
    sa_ref[...] = jnp.where((lane & 63) < 32, -1.0, 1.0) * jnp.sin(ang)
    fc = jnp.exp((lane & 15).astype(f32) * (-log_theta / 16.0))
    angc = jnp.where(lane < 64, jnp.where(lane < 32, pr, pc) * fc, 0.0)
    cc_ref[...] = jnp.cos(angc)
    sc_ref[...] = jnp.where((lane & 31) < 16, -1.0, 1.0) * jnp.sin(angc)


def rope_tables(pos_row, pos_col, tm):
    t = pos_row.shape[0]
    spec1 = pl.BlockSpec((tm, 1), lambda i: (i, 0))
    spec = pl.BlockSpec((tm, V7X_LANES), lambda i: (i, 0))
    sh = jax.ShapeDtypeStruct((t, V7X_LANES), f32)
    return pl.pallas_call(
        _rope_table_kernel, grid=(t // tm,), in_specs=[spec1, spec1], out_specs=[spec] * 4,
        out_shape=[sh] * 4, compiler_params=_cparams(("arbitrary",), 32), name="rope_tables",
    )(pos_row, pos_col)


def _prenorm_kernel(*refs, has_delta, want_h):
    if has_delta:
        x_ref, d_ref, gate_ref = refs[:3]
        rest = refs[3:]
    else:
        x_ref = refs[0]
        rest = refs[1:]
    x = x_ref[...]
    if has_delta:
        x = x + gate_ref[...] * d_ref[...].astype(f32)
    if want_h:
        g_ref, shift_ref, scale_ref = rest[:3]
        outs = rest[3:]
    else:
        outs = rest
    k = 0
    if has_delta:
        outs[k][...] = x
        k += 1
    if want_h:
        y = x * lax.rsqrt(jnp.mean(x * x, axis=-1, keepdims=True) + EPS)
        y = y * g_ref[...]
        outs[k][...] = (y * (1.0 + scale_ref[...]) + shift_ref[...]).astype(bf16)


def prenorm(x, mod5, layer, grp_of_block, tm, *, delta=None, gate_idx=None, gate_layer=None,
            norm_g=None, shift_idx=None, scale_idx=None, rows=None):
    t, d = x.shape
    rows = t if rows is None else rows
    has_delta = delta is not None
    want_h = norm_g is not None
    row_spec = pl.BlockSpec((tm, d), lambda i: (i, 0))

    def mod_spec(lyr, which):
        return pl.BlockSpec((None, None, None, 1, d), lambda i: (lyr, grp_of_block(i), which, 0, 0))

    in_specs, args = [row_spec], [x]
    if has_delta:
        in_specs += [row_spec, mod_spec(gate_layer, gate_idx)]
        args += [delta, mod5]
    if want_h:
        in_specs += [pl.BlockSpec((1, d), lambda i: (0, 0)), mod_spec(layer, shift_idx), mod_spec(layer, scale_idx)]
        args += [norm_g.reshape(1, d), mod5, mod5]
    out_specs, out_shape = [], []
    if has_delta:
        out_specs.append(row_spec)
        out_shape.append(jax.ShapeDtypeStruct((rows, d), f32))
    if want_h:
        out_specs.append(row_spec)
        out_shape.append(jax.ShapeDtypeStruct((rows, d), bf16))
    return pl.pallas_call(
        functools.partial(_prenorm_kernel, has_delta=has_delta, want_h=want_h),
        grid=(rows // tm,), in_specs=in_specs, out_specs=out_specs, out_shape=out_shape,
        compiler_params=_cparams(("arbitrary",), 48), name="prenorm",
    )(*args)


def _matmul_kernel(a_ref, w_ref, o_ref):
    o_ref[...] = jnp.dot(a_ref[...], w_ref[...], preferred_element_type=f32).astype(o_ref.dtype)


def matmul(a, w, tm, tn):
    t, k = a.shape
    n = w.shape[1]
    return pl.pallas_call(
        _matmul_kernel, grid=(t // tm, n // tn),
        in_specs=[pl.BlockSpec((tm, k), lambda i, j: (i, 0)), pl.BlockSpec((k, tn), lambda i, j: (0, j))],
        out_specs=pl.BlockSpec((tm, tn), lambda i, j: (i, j)),
        out_shape=jax.ShapeDtypeStruct((t, n), bf16),
        compiler_params=_cparams(("arbitrary", "arbitrary"), 48), name="in_proj",
    )(a, w)


def _rope128(x, cos, sin_signed, half):
    lane = lax.broadcasted_iota(jnp.int32, x.shape, 1)
    first = (lane & (2 * half - 1)) < half
    swapped = jnp.where(first, pltpu.roll(x, V7X_LANES - half, 1), pltpu.roll(x, half, 1))
    return x * cos + swapped * sin_signed


def _gqa_prep_kernel(q_ref, kv_ref, gq_ref, gk_ref, cos_ref, sin_ref, qo_ref, ko_ref):
    cos, sin = cos_ref[...], sin_ref[...]
    qscale = HEAD_DIM ** -0.5 * LOG2E
    for h in range(GQA_HEADS):
        sl = slice(h * HEAD_DIM, (h + 1) * HEAD_DIM)
        q = q_ref[:, sl].astype(f32)
        q = q * lax.rsqrt(jnp.mean(q * q, axis=-1, keepdims=True) + EPS) * gq_ref[...]
        qo_ref[:, sl] = (_rope128(q, cos, sin, 32) * qscale).astype(bf16)
    for h in range(GQA_KV_HEADS):
        sl = slice(h * HEAD_DIM, (h + 1) * HEAD_DIM)
        k = kv_ref[:, sl].astype(f32)
        k = k * lax.rsqrt(jnp.mean(k * k, axis=-1, keepdims=True) + EPS) * gk_ref[...]
        ko_ref[:, sl] = _rope128(k, cos, sin, 32).astype(bf16)


def gqa_prep(p, gq, gk, cos_a, sin_a, tm):
    t = p.shape[0]
    qw, kw = GQA_HEADS * HEAD_DIM, GQA_KV_HEADS * HEAD_DIM
    tab = pl.BlockSpec((tm, V7X_LANES), lambda i: (i, 0))
    vec = pl.BlockSpec((1, HEAD_DIM), lambda i: (0, 0))
    return pl.pallas_call(
        _gqa_prep_kernel, grid=(t // tm,),
        in_specs=[pl.BlockSpec((tm, qw), lambda i: (i, QA_OFF // qw)),
                  pl.BlockSpec((tm, 2 * kw), lambda i: (i, KA_OFF // (2 * kw))), vec, vec, tab, tab],
        out_specs=[pl.BlockSpec((tm, qw), lambda i: (i, 0)), pl.BlockSpec((tm, kw), lambda i: (i, 0))],
        out_shape=[jax.ShapeDtypeStruct((t, qw), bf16), jax.ShapeDtypeStruct((t, kw), bf16)],
        compiler_params=_cparams(("arbitrary",), 32), name="gqa_prep",
    )(p, p, gq.reshape(1, -1), gk.reshape(1, -1), cos_a, sin_a)


def _mla_prep_kernel(q_ref, ckv_ref, kr_ref, kvg_ref, wkvb_ref, gq_ref, gkn_ref, gkr_ref, cos_ref, sin_ref,
                     qo_ref, ko_ref, vo_ref):
    cos, sin = cos_ref[...], sin_ref[...]
    qscale = MLA_QK ** -0.5 * LOG2E
    inv_qk = 1.0 / MLA_QK
    for h in range(MLA_HEADS):
        lo = h * MLA_QPAD
        qn = q_ref[:, lo:lo + V7X_LANES].astype(f32)
        qr = q_ref[:, lo + V7X_LANES:lo + MLA_QPAD].astype(f32)
        ss = jnp.sum(qn * qn, axis=-1, keepdims=True) + jnp.sum(qr * qr, axis=-1, keepdims=True)
        r = lax.rsqrt(ss * inv_qk + EPS) * qscale
        qo_ref[:, lo:lo + V7X_LANES] = (qn * r * gq_ref[:, :V7X_LANES]).astype(bf16)
        qo_ref[:, lo + V7X_LANES:lo + MLA_QPAD] = (_rope128(qr * gq_ref[:, V7X_LANES:], cos, sin, 16) * r).astype(bf16)
    c = ckv_ref[...].astype(f32)
    cn = c * lax.rsqrt(jnp.mean(c * c, axis=-1, keepdims=True) + EPS) * kvg_ref[...]
    kv = jnp.dot(cn.astype(bf16), wkvb_ref[...], preferred_element_type=f32)
    kr = kr_ref[...].astype(f32)
    ss_r = jnp.sum(kr * kr, axis=-1, keepdims=True)
    kr_rot = _rope128(kr * gkr_ref[...], cos, sin, 16)
    nv = MLA_HEADS * MLA_NOPE
    for h in range(MLA_HEADS):
        kn = kv[:, h * MLA_NOPE:(h + 1) * MLA_NOPE]
        r = lax.rsqrt((jnp.sum(kn * kn, axis=-1, keepdims=True) + ss_r) * inv_qk + EPS)
        lo = h * MLA_QPAD
        ko_ref[:, lo:lo + V7X_LANES] = (kn * r * gkn_ref[...]).astype(bf16)
        ko_ref[:, lo + V7X_LANES:lo + MLA_QPAD] = (kr_rot * r).astype(bf16)
    vo_ref[...] = kv[:, nv:].astype(bf16)


def mla_prep(p, kv_g, wkvb_p, gq_pad, gk_nope, gk_rope, cos_c, sin_c, tm):
    t = p.shape[0]
    qw = MLA_HEADS * MLA_QPAD
    vw = MLA_HEADS * MLA_V
    tab = pl.BlockSpec((tm, V7X_LANES), lambda i: (i, 0))

    def vec(n):
        return pl.BlockSpec((1, n), lambda i: (0, 0))

    return pl.pallas_call(
        _mla_prep_kernel, grid=(t // tm,),
        in_specs=[pl.BlockSpec((tm, qw), lambda i: (i, QC_OFF // qw)),
                  pl.BlockSpec((tm, MLA_RANK), lambda i: (i, CKV_OFF // MLA_RANK)),
                  pl.BlockSpec((tm, V7X_LANES), lambda i: (i, KR_OFF // V7X_LANES)),
                  vec(MLA_RANK), pl.BlockSpec((MLA_RANK, 2 * vw), lambda i: (0, 0)),
                  vec(MLA_QPAD), vec(V7X_LANES), vec(V7X_LANES), tab, tab],
        out_specs=[pl.BlockSpec((tm, qw), lambda i: (i, 0)), pl.BlockSpec((tm, qw), lambda i: (i, 0)),
                   pl.BlockSpec((tm, vw), lambda i: (i, 0))],
        out_shape=[jax.ShapeDtypeStruct((t, qw), bf16), jax.ShapeDtypeStruct((t, qw), bf16),
                   jax.ShapeDtypeStruct((t, vw), bf16)],
        compiler_params=_cparams(("arbitrary",), 48), name="mla_prep",
    )(p, p, p, kv_g.reshape(1, -1), wkvb_p, gq_pad.reshape(1, -1), gk_nope.reshape(1, -1),
      gk_rope.reshape(1, -1), cos_c, sin_c)


def _attn_block(qs, k, v, m_ref, l_ref, acc_ref):
    s = lax.dot_general(qs, k, (((1,), (1,)), ((), ())), preferred_element_type=f32)
    m_prev = m_ref[...]
    m_new = jnp.maximum(m_prev, jnp.max(s, axis=-1, keepdims=True))
    alpha = jnp.exp2(m_prev - m_new)
    pr = jnp.exp2(s - m_new)
    l_ref[...] = alpha * l_ref[...] + jnp.sum(pr, axis=-1, keepdims=True)
    acc_ref[...] = alpha * acc_ref[...] + jnp.dot(pr.astype(bf16), v, preferred_element_type=f32)
    m_ref[...] = m_new


def _flash_kernel(*refs, group, dq, dv, tq, tk, n_lat):
    if n_lat:
        q_ref, kl_ref, vl_ref, kc_ref, vc_ref, o_ref, qs_ref, m_ref, l_ref, acc_ref = refs
    else:
        q_ref, kc_ref, vc_ref, _, o_ref, qs_ref, m_ref, l_ref, acc_ref = refs
    for g in range(group):
        qs_ref[g * tq:(g + 1) * tq, :] = q_ref[:, g * dq:(g + 1) * dq]
    m_ref[...] = jnp.full(m_ref.shape, -1e30, f32)
    l_ref[...] = jnp.zeros(l_ref.shape, f32)
    acc_ref[...] = jnp.zeros(acc_ref.shape, f32)
    qs = qs_ref[...]
    _attn_block(qs, kc_ref[...], vc_ref[...], m_ref, l_ref, acc_ref)
    if n_lat:
        def body(j, carry):
            off = pl.multiple_of(j * tk, tk)
            _attn_block(qs, kl_ref[pl.ds(off, tk), :], vl_ref[pl.ds(off, tk), :], m_ref, l_ref, acc_ref)
            return carry
        lax.fori_loop(0, n_lat, body, 0)
    for g in range(group):
        rows = slice(g * tq, (g + 1) * tq)
        o_ref[:, g * dv:(g + 1) * dv] = (acc_ref[rows, :] / l_ref[rows, :]).astype(o_ref.dtype)


def flash_attention(q, k, v, v_col_off, *, batch, seq, ctx_len, kv_heads, group, dq, dv, tq, tk, prev=None):
    t = q.shape[0]
    lat = prev is None
    ctx_blk0 = batch * seq // ctx_len
    voff = v_col_off // dv
    rows = group * tq
    kc_spec = pl.BlockSpec((ctx_len, dq), lambda b, h, i: (ctx_blk0 + b, h))
    vc_spec = pl.BlockSpec((ctx_len, dv), lambda b, h, i: (ctx_blk0 + b, voff + h))
    if lat:
        nq = seq // tq
        in_specs = [pl.BlockSpec((tq, group * dq), lambda b, h, i: (b * nq + i, h)),
                    pl.BlockSpec((seq, dq), lambda b, h, i: (b, h)),
                    pl.BlockSpec((seq, dv), lambda b, h, i: (b, voff + h)), kc_spec, vc_spec]
        args = [q, k, v, k, v]
        out_spec = pl.BlockSpec((tq, group * dv), lambda b, h, i: (b * nq + i, h))
        aliases = {}
        n_lat = seq // tk
    else:
        nq = 1
        assert tq == ctx_len
        in_specs = [pl.BlockSpec((tq, group * dq), lambda b, h, i: (ctx_blk0 + b, h)), kc_spec, vc_spec,
                    pl.BlockSpec(memory_space=pl.ANY)]
        args = [q, k, v, prev]
        out_spec = pl.BlockSpec((tq, group * dv), lambda b, h, i: (ctx_blk0 + b, h))
        aliases = {3: 0}
        n_lat = 0
    return pl.pallas_call(
        functools.partial(_flash_kernel, group=group, dq=dq, dv=dv, tq=tq, tk=tk, n_lat=n_lat),
        grid=(batch, kv_heads, nq), in_specs=in_specs, out_specs=out_spec,
        out_shape=jax.ShapeDtypeStruct((t, kv_heads * group * dv), bf16),
        scratch_shapes=[pltpu.VMEM((rows, dq), bf16), pltpu.VMEM((rows, 1), f32), pltpu.VMEM((rows, 1), f32),
                        pltpu.VMEM((rows, dv), f32)],
        input_output_aliases=aliases,
        compiler_params=_cparams(("arbitrary", "arbitrary", "arbitrary"), 48),
        name="flash_lat" if lat else "flash_ctx",
    )(*args)


CONV_ROWS = 128
CONV_HALO = 16


def _conv3_kernel(prev_ref, cur_ref, next_ref, w_ref, b_ref, o_ref, *, blocks_per_seq):
    r = pl.program_id(0)
    pos = r % blocks_per_seq
    x = cur_ref[...].astype(f32)
    row = lax.broadcasted_iota(jnp.int32, x.shape, 0)
    prev_row = jnp.where(pos == 0, 0.0, prev_ref[...].astype(f32)[CONV_HALO - 1:CONV_HALO, :])
    next_row = jnp.where(pos == blocks_per_seq - 1, 0.0, next_ref[...].astype(f32)[0:1, :])
    xm = jnp.where(row == 0, prev_row, pltpu.roll(x, 1, 0))
    xp = jnp.where(row == CONV_ROWS - 1, next_row, pltpu.roll(x, CONV_ROWS - 1, 0))
    o_ref[...] = xm * w_ref[0:1, :] + x * w_ref[1:2, :] + xp * w_ref[2:3, :] + b_ref[...]


def short_conv(p, conv_w, conv_b, row0, nrows, seq_len, cw):
    width = 3 * HY_C
    rb0 = row0 // CONV_ROWS
    nblk = nrows // CONV_ROWS
    sub = CONV_ROWS // CONV_HALO
    last_halo = p.shape[0] // CONV_HALO - 1
    c0 = HY_OFF // cw
    return pl.pallas_call(
        functools.partial(_conv3_kernel, blocks_per_seq=seq_len // CONV_ROWS),
        grid=(nblk, width // cw),
        in_specs=[
            pl.BlockSpec((CONV_HALO, cw), lambda r, c: (jnp.maximum((rb0 + r) * sub - 1, 0), c0 + c)),
            pl.BlockSpec((CONV_ROWS, cw), lambda r, c: (rb0 + r, c0 + c)),
            pl.BlockSpec((CONV_HALO, cw), lambda r, c: (jnp.minimum((rb0 + r + 1) * sub, last_halo), c0 + c)),
            pl.BlockSpec((3, cw), lambda r, c: (0, c)),
            pl.BlockSpec((1, cw), lambda r, c: (0, c)),
        ],
        out_specs=pl.BlockSpec((CONV_ROWS, cw), lambda r, c: (r, c)),
        out_shape=jax.ShapeDtypeStruct((nrows, width), f32),
        compiler_params=_cparams(("arbitrary", "arbitrary"), 32), name="short_conv",
    )(p, p, p, conv_w, conv_b.reshape(1, width))


@functools.lru_cache(maxsize=None)
def _dft_tables(length):
    nb = 128 if length >= 1024 else 16
    n_fft = 2 * length
    na = n_fft // nb
    nah = na // 2
    lo = np.arange(nb, dtype=np.int64)[:, None, None]
    k1 = np.arange(na, dtype=np.int64)[None, :, None]
    hi = np.arange(nah, dtype=np.int64)[None, None, :]
    ang = 2.0 * np.pi * (((nb * hi + lo) * k1) % n_fft) / n_fft
    c, s = np.cos(ang), np.sin(ang)
    a1 = np.concatenate([np.concatenate([c, s], axis=2), np.concatenate([-s, c], axis=2)], axis=1)
    ct, st = np.swapaxes(c, 1, 2) / n_fft, np.swapaxes(s, 1, 2) / n_fft
    a3 = np.concatenate([np.concatenate([ct, -st], axis=2), np.concatenate([st, ct], axis=2)], axis=1)
    kk = np.arange(nb, dtype=np.int64)
    angb = 2.0 * np.pi * ((kk[:, None] * kk[None, :]) % nb) / nb
    cb, sb = np.cos(angb), np.sin(angb)
    mf = np.block([[cb, sb], [-sb, cb]])
    mfc = np.block([[cb, -sb], [sb, cb]])
    return dict(nb=nb, na=na, nah=nah,
                a1=jnp.asarray(a1, dtype=bf16), a1r=jnp.asarray(a1[:, :, :nah], dtype=bf16),
                a3=jnp.asarray(a3, dtype=bf16), mf=jnp.asarray(mf, dtype=bf16), mfc=jnp.asarray(mfc, dtype=bf16),
                m2f=jnp.asarray(np.concatenate([mf, mfc], axis=1), dtype=bf16))


def _filter_stage1_kernel(a1r_ref, band_ref, w1_ref, b1_ref, w2_ref, b2_ref, w3_ref, freq_ref, delta_ref,
                          o_ref, ss_ref, *, length, nb, nah):
    j = pl.program_id(1)
    hp = lax.Precision.HIGHEST
    rows = V7X_SUBLANES * nah
    ridx = lax.broadcasted_iota(jnp.int32, (rows, 1), 0)
    sh = nah.bit_length() - 1
    t_int = nb * (ridx & (nah - 1)) + (j * V7X_SUBLANES + (ridx >> sh))
    t = t_int.astype(f32)
    t_unit = t / float(max(length - 1, 1))
    lane = lax.broadcasted_iota(jnp.int32, (rows, V7X_LANES), 1)
    ang = ((2.0 * math.pi / length) * t) * band_ref[...]
    feats = jnp.where(lane == 0, t_unit,
                      jnp.where(lane <= HY_BANDS, jnp.cos(ang), jnp.where(lane <= 2 * HY_BANDS, -jnp.sin(ang), 0.0)))
    h = jnp.sin(freq_ref[0:1, :] * (jnp.dot(feats, w1_ref[...], precision=hp, preferred_element_type=f32) + b1_ref[...]))
    h = jnp.sin(freq_ref[1:2, :] * (jnp.dot(h, w2_ref[...], precision=hp, preferred_element_type=f32) + b2_ref[...]))
    decay = jnp.exp(-t_unit * delta_ref[...])

    @pl.when(j == 0)
    def _():
        ss_ref[...] = jnp.zeros(ss_ref.shape, f32)

    for o in range(HY_ORDER):
        fwd = jnp.dot(h, w3_ref[2 * o], precision=hp, preferred_element_type=f32) * decay
        bwd = jnp.dot(h, w3_ref[2 * o + 1], precision=hp, preferred_element_type=f32) * decay
        bwd = jnp.where(t_int == 0, 0.0, bwd)
        ss_ref[o:o + 1, :] += jnp.sum(fwd * fwd + bwd * bwd, axis=0, keepdims=True)
        fb, bb = fwd.astype(bf16), bwd.astype(bf16)
        na = 2 * nah
        for l in range(V7X_SUBLANES):
            a = a1r_ref[l]
            ff = jnp.dot(a, fb[l * nah:(l + 1) * nah], preferred_element_type=f32)
            gg = jnp.dot(a, bb[l * nah:(l + 1) * nah], preferred_element_type=f32)
            o_ref[o, l, 0] = ff[:na]
            o_ref[o, l, 1] = ff[na:]
            o_ref[o, l, 2] = gg[:na]
            o_ref[o, l, 3] = -gg[na:]


def filter_stage1(tabs, length, band_row, w1p, b1, w2, b2, w3r, freq, delta_row):
    nb, na, nah = tabs["nb"], tabs["na"], tabs["nah"]
    ncb = HY_C // V7X_LANES
    full = lambda *shape: pl.BlockSpec(shape, lambda c, j: (0,) * len(shape))
    return pl.pallas_call(
        functools.partial(_filter_stage1_kernel, length=length, nb=nb, nah=nah),
        grid=(ncb, nb // V7X_SUBLANES),
        in_specs=[pl.BlockSpec((V7X_SUBLANES, 2 * na, nah), lambda c, j: (j, 0, 0)),
                  full(1, V7X_LANES), full(V7X_LANES, HY_FW), full(1, HY_FW), full(HY_FW, HY_FW), full(1, HY_FW),
                  pl.BlockSpec((2 * HY_ORDER, HY_FW, V7X_LANES), lambda c, j: (0, 0, c)),
                  full(2, HY_FW), pl.BlockSpec((1, V7X_LANES), lambda c, j: (0, c))],
        out_specs=[pl.BlockSpec((HY_ORDER, V7X_SUBLANES, 4, na, V7X_LANES), lambda c, j: (0, j, 0, 0, c)),
                   pl.BlockSpec((V7X_SUBLANES, V7X_LANES), lambda c, j: (0, c))],
        out_shape=[jax.ShapeDtypeStruct((HY_ORDER, nb, 4, na, HY_C), f32),
                   jax.ShapeDtypeStruct((V7X_SUBLANES, HY_C), f32)],
        compiler_params=_cparams(("arbitrary", "arbitrary"), 48), name="hy_filter_s1",
    )(tabs["a1r"], band_row, w1p, b1, w2, b2, w3r, freq, delta_row)


def _filter_stage2_kernel(x_ref, m_ref, o_ref):
    for kl in range(V7X_SUBLANES):
        rhs = jnp.concatenate([x_ref[:, q, kl, :] for q in range(4)], axis=0).astype(bf16)
        o_ref[kl] = jnp.dot(m_ref[...], rhs, preferred_element_type=f32)


def filter_stage2(tabs, hin):
    nb, na = tabs["nb"], tabs["na"]
    ncb = HY_C // V7X_LANES
    return pl.pallas_call(
        _filter_stage2_kernel, grid=(HY_ORDER, ncb, na // V7X_SUBLANES),
        in_specs=[pl.BlockSpec((None, nb, 4, V7X_SUBLANES, V7X_LANES), lambda o, c, j: (o, 0, 0, j, c)),
                  pl.BlockSpec((2 * nb, 4 * nb), lambda o, c, j: (0, 0))],
        out_specs=pl.BlockSpec((None, V7X_SUBLANES, 2 * nb, V7X_LANES), lambda o, c, j: (o, j, 0, c)),
        out_shape=jax.ShapeDtypeStruct((HY_ORDER, na, 2 * nb, HY_C), f32),
        compiler_params=_cparams(("arbitrary", "arbitrary", "arbitrary"), 48), name="hy_filter_s2",
    )(hin, tabs["m2f"])


def _conv_stage1_kernel(zr_ref, zi_ref, a1_ref, o_ref, *, na):
    for l in range(V7X_SUBLANES):
        rhs = jnp.concatenate([zr_ref[:, l, :], zi_ref[:, l, :]], axis=0).astype(bf16)
        y = jnp.dot(a1_ref[l], rhs, preferred_element_type=f32)
        o_ref[l, 0] = y[:na]
        o_ref[l, 1] = y[na:]


def conv_stage1(tabs, z4, col_off):
    nb, na, nah = tabs["nb"], tabs["na"], tabs["nah"]
    ncb = HY_C // V7X_LANES
    c0 = col_off // V7X_LANES
    return pl.pallas_call(
        functools.partial(_conv_stage1_kernel, na=na), grid=(ncb, nb // V7X_SUBLANES),
        in_specs=[pl.BlockSpec((None, nah, V7X_SUBLANES, V7X_LANES), lambda c, j: (0, 0, j, c0 + c)),
                  pl.BlockSpec((None, nah, V7X_SUBLANES, V7X_LANES), lambda c, j: (1, 0, j, c0 + c)),
                  pl.BlockSpec((V7X_SUBLANES, 2 * na, na), lambda c, j: (j, 0, 0))],
        out_specs=pl.BlockSpec((V7X_SUBLANES, 2, na, V7X_LANES), lambda c, j: (j, 0, 0, c)),
        out_shape=jax.ShapeDtypeStruct((nb, 2, na, HY_C), f32),
        compiler_params=_cparams(("arbitrary", "arbitrary"), 48), name="hy_conv_s1",
    )(z4, z4, tabs["a1"])


def _conv_stage2_kernel(x_ref, h_ref, mf_ref, mfc_ref, o_ref, *, nb):
    for kl in range(V7X_SUBLANES):
        rhs = jnp.concatenate([x_ref[:, 0, kl, :], x_ref[:, 1, kl, :]], axis=0).astype(bf16)
        x = jnp.dot(mf_ref[...], rhs, preferred_element_type=f32)
        xr, xi = x[:nb], x[nb:]
        hr, hi = h_ref[kl, :nb, :], h_ref[kl, nb:, :]
        y = jnp.concatenate([xr * hr - xi * hi, xr * hi + xi * hr], axis=0).astype(bf16)
        e = jnp.dot(mfc_ref[...], y, preferred_element_type=f32)
        o_ref[kl, 0] = e[:nb]
        o_ref[kl, 1] = e[nb:]


def conv_stage2(tabs, din, hspec, order):
    nb, na = tabs["nb"], tabs["na"]
    ncb = HY_C // V7X_LANES
    return pl.pallas_call(
        functools.partial(_conv_stage2_kernel, nb=nb), grid=(ncb, na // V7X_SUBLANES),
        in_specs=[pl.BlockSpec((nb, 2, V7X_SUBLANES, V7X_LANES), lambda c, j: (0, 0, j, c)),
                  pl.BlockSpec((None, V7X_SUBLANES, 2 * nb, V7X_LANES), lambda c, j: (order, j, 0, c)),
                  pl.BlockSpec((2 * nb, 2 * nb), lambda c, j: (0, 0)),
                  pl.BlockSpec((2 * nb, 2 * nb), lambda c, j: (0, 0))],
        out_specs=pl.BlockSpec((V7X_SUBLANES, 2, nb, V7X_LANES), lambda c, j: (j, 0, 0, c)),
        out_shape=jax.ShapeDtypeStruct((na, 2, nb, HY_C), f32),
        compiler_params=_cparams(("arbitrary", "arbitrary"), 48), name="hy_conv_s2",
    )(din, hspec, tabs["mf"], tabs["mfc"])


def _conv_stage3_kernel(e_ref, a3_ref, zr_ref, zi_ref, gr_ref, gi_ref, ss_ref, skip_ref, o_ref, *, nah, order):
    rs = lax.rsqrt(ss_ref[order:order + 1, :] + EPS)
    skip = skip_ref[order:order + 1, :]
    for l in range(V7X_SUBLANES):
        rhs = jnp.concatenate([e_ref[:, 0, l, :], e_ref[:, 1, l, :]], axis=0).astype(bf16)
        y = jnp.dot(a3_ref[l], rhs, preferred_element_type=f32) * rs
        o_ref[0, :, l, :] = gr_ref[:, l, :] * (y[:nah] + skip * zr_ref[:, l, :])
        o_ref[1, :, l, :] = gi_ref[:, l, :] * (y[nah:] + skip * zi_ref[:, l, :])


def conv_stage3(tabs, ein, z4, z_off, g4, g_off, ss, skip, order):
    nb, na, nah = tabs["nb"], tabs["na"], tabs["nah"]
    ncb = HY_C // V7X_LANES
    zc, gc = z_off // V7X_LANES, g_off // V7X_LANES

    def slab(b, c0):
        return pl.BlockSpec((None, nah, V7X_SUBLANES, V7X_LANES), lambda c, j: (b, 0, j, c0 + c))

    return pl.pallas_call(
        functools.partial(_conv_stage3_kernel, nah=nah, order=order), grid=(ncb, nb // V7X_SUBLANES),
        in_specs=[pl.BlockSpec((na, 2, V7X_SUBLANES, V7X_LANES), lambda c, j: (0, 0, j, c)),
                  pl.BlockSpec((V7X_SUBLANES, na, 2 * na), lambda c, j: (j, 0, 0)),
                  slab(0, zc), slab(1, zc), slab(0, gc), slab(1, gc),
                  pl.BlockSpec((V7X_SUBLANES, V7X_LANES), lambda c, j: (0, c)),
                  pl.BlockSpec((HY_ORDER, V7X_LANES), lambda c, j: (0, c))],
        out_specs=pl.BlockSpec((2, nah, V7X_SUBLANES, V7X_LANES), lambda c, j: (0, 0, j, c)),
        out_shape=jax.ShapeDtypeStruct((2, nah, nb, HY_C), f32),
        compiler_params=_cparams(("arbitrary", "arbitrary"), 48), name="hy_conv_s3",
    )(ein, tabs["a3"], z4, z4, g4, g4, ss, skip)


def hyena_mixer(uc, length, hf, skip):
    tabs = _dft_tables(length)
    nb, nah = tabs["nb"], tabs["nah"]
    hin, ss = filter_stage1(tabs, length, *hf)
    hspec = filter_stage2(tabs, hin)
    u4 = uc.reshape(2, nah, nb, 3 * HY_C)
    z4, z_off = u4, 0
    for o in range(HY_ORDER):
        din = conv_stage1(tabs, z4, z_off)
        ein = conv_stage2(tabs, din, hspec, o)
        z4 = conv_stage3(tabs, ein, z4, z_off, u4, (o + 1) * HY_C, ss, skip, o)
        z_off = 0
    return z4.reshape(2 * length, HY_C)


def _merge_kernel(gl_ref, a_ref, hy_ref, c_ref, wg_ref, bg_ref, wb_ref, wo_ref, o_ref, acc_ref):
    j = pl.program_id(1)

    @pl.when(j == 0)
    def _():
        acc_ref[...] = jnp.zeros(acc_ref.shape, f32)

    gl = gl_ref[...]
    branches = (a_ref[...], hy_ref[...].astype(bf16), c_ref[...])
    y = None
    for n in range(N_BRANCH):
        z = jnp.dot(gl, wg_ref[n], preferred_element_type=f32) + bg_ref[n]
        g = 1.0 / (1.0 + jnp.exp(-z))
        term = g * jnp.dot(branches[n], wb_ref[n], preferred_element_type=f32)
        y = term if y is None else y + term
    acc_ref[...] += jnp.dot(y.astype(bf16), wo_ref[...], preferred_element_type=f32)

    @pl.when(j == pl.num_programs(1) - 1)
    def _():
        o_ref[...] = acc_ref[...].astype(o_ref.dtype)


def merge_out(p, a, hy, c, wg, bg, wb, wo, tm, tj):
    t = p.shape[0]
    d = wo.shape[1]
    row = lambda w: pl.BlockSpec((tm, w), lambda i, j: (i, 0))
    return pl.pallas_call(
        _merge_kernel, grid=(t // tm, d // tj),
        in_specs=[pl.BlockSpec((tm, GATE_RANK), lambda i, j: (i, GATE_OFF // GATE_RANK)),
                  row(BRANCH_W), row(BRANCH_W), row(BRANCH_W),
                  pl.BlockSpec((N_BRANCH, GATE_RANK, tj), lambda i, j: (0, 0, j)),
                  pl.BlockSpec((N_BRANCH, 1, tj), lambda i, j: (0, 0, j)),
                  pl.BlockSpec((N_BRANCH, BRANCH_W, tj), lambda i, j: (0, 0, j)),
                  pl.BlockSpec((tj, d), lambda i, j: (j, 0))],
        out_specs=pl.BlockSpec((tm, d), lambda i, j: (i, 0)),
        out_shape=jax.ShapeDtypeStruct((t, d), bf16),
        scratch_shapes=[pltpu.VMEM((tm, d), f32)],
        compiler_params=_cparams(("arbitrary", "arbitrary"), 56), name="merge_out",
    )(p, a, hy, c, wg, bg, wb, wo)


def _ffn_kernel(h_ref, wg_ref, wu_ref, wd_ref, o_ref, acc_ref):
    j = pl.program_id(1)

    @pl.when(j == 0)
    def _():
        acc_ref[...] = jnp.zeros(acc_ref.shape, f32)

    h = h_ref[...]
    g = jnp.dot(h, wg_ref[...], preferred_element_type=f32)
    u = jnp.dot(h, wu_ref[...], preferred_element_type=f32)
    a = (g * (1.0 / (1.0 + jnp.exp(-g))) * u).astype(bf16)
    acc_ref[...] += jnp.dot(a, wd_ref[...], preferred_element_type=f32)

    @pl.when(j == pl.num_programs(1) - 1)
    def _():
        o_ref[...] = acc_ref[...].astype(o_ref.dtype)


def ffn(h, w_gu, w_down, tm, th):
    t, d = h.shape
    hidden = w_down.shape[0]
    nh = hidden // th
    return pl.pallas_call(
        _ffn_kernel, grid=(t // tm, nh),
        in_specs=[pl.BlockSpec((tm, d), lambda i, j: (i, 0)),
                  pl.BlockSpec((d, th), lambda i, j: (0, j)),
                  pl.BlockSpec((d, th), lambda i, j: (0, nh + j)),
                  pl.BlockSpec((th, d), lambda i, j: (j, 0))],
        out_specs=pl.BlockSpec((tm, d), lambda i, j: (i, 0)),
        out_shape=jax.ShapeDtypeStruct((t, d), bf16),
        scratch_shapes=[pltpu.VMEM((tm, d), f32)],
        compiler_params=_cparams(("arbitrary", "arbitrary"), 56), name="ffn",
    )(h, w_gu, w_gu, w_down)


def _pack_w_in(w_in):
    depth, d, _ = w_in.shape
    o_qa, o_ka, o_hy = 0, GQA_HEADS * HEAD_DIM, (GQA_HEADS + 2 * GQA_KV_HEADS) * HEAD_DIM
    o_qc = o_hy + 3 * HY_C
    o_ckv = o_qc + MLA_HEADS * MLA_QK
    o_kr = o_ckv + MLA_RANK
    o_gate = o_kr + MLA_ROPE
    qc = w_in[:, :, o_qc:o_ckv].reshape(depth, d, MLA_HEADS, MLA_QK)
    qc = jnp.pad(qc, ((0, 0), (0, 0), (0, 0), (0, MLA_QPAD - MLA_QK))).reshape(depth, d, MLA_HEADS * MLA_QPAD)
    tail = NP_COLS - KR_OFF - MLA_ROPE
    parts = [qc, w_in[:, :, o_qa:o_ka], w_in[:, :, o_ka:o_hy], w_in[:, :, o_hy:o_qc], w_in[:, :, o_ckv:o_kr],
             w_in[:, :, o_gate:o_gate + GATE_RANK], w_in[:, :, o_kr:o_gate], jnp.zeros((depth, d, tail), w_in.dtype)]
    return jnp.concatenate(parts, axis=2).astype(bf16)


def _pack_w_kvb(w):
    depth = w.shape[0]
    w = w.reshape(depth, MLA_RANK, MLA_HEADS, 2, MLA_NOPE)
    return jnp.swapaxes(w, 2, 3).reshape(depth, MLA_RANK, 2 * MLA_HEADS * MLA_NOPE).astype(bf16)


def kernel(x, c, ctx, c_ctx, norm1_g, norm2_g, ada_down, ada_up, ada_b, w_in, gqa_q_norm, gqa_k_norm, hy_conv_w,
           hy_conv_b, hf_w1, hf_b1, hf_w2, hf_b2, hf_w3, hf_freq, hy_skip, mla_kv_norm, mla_w_kvb, mla_q_norm,
           mla_k_norm, w_gate_up, b_gate, w_branch, w_out, ffn_w_gu, ffn_w_down):
    batch, seq, d = x.shape
    ctx_len = ctx.shape[1]
    depth = w_in.shape[0]
    hidden = ffn_w_down.shape[1]
    assert batch == 2, "the Hyena long convolution packs the two batches as one complex sequence"
    assert w_in.shape[2] == IN_COLS and seq % GRID_W == 0
    n_lat, n_ctx = batch * seq, batch * ctx_len
    t_all = n_lat + n_ctx

    tm = _row_tile(seq, n_ctx, (512, 256, 128))
    tm_ew = _row_tile(seq, n_ctx, (256, 128))
    tn_in = 512
    tj = 512
    th = 256
    tq_a = min(256, ctx_len)
    tq_c = min(1024, seq)
    tk = min(512, seq)

    def grp(tile):
        per = seq // tile
        return lambda i: jnp.minimum(i // per, batch)

    w_in_p = _pack_w_in(w_in)
    w_kvb_p = _pack_w_kvb(mla_w_kvb)
    wg_b, wb_b, wo_b = w_gate_up.astype(bf16), w_branch.astype(bf16), w_out.astype(bf16)
    wgu_b, wd_b = ffn_w_gu.astype(bf16), ffn_w_down.astype(bf16)
    bg = b_gate.reshape(depth, N_BRANCH, 1, d)
    gq_pad = jnp.pad(mla_q_norm, ((0, 0), (0, MLA_QPAD - MLA_QK)))
    gk_nope = mla_k_norm[:, :MLA_NOPE]
    gk_rope = jnp.pad(mla_k_norm[:, MLA_NOPE:], ((0, 0), (0, V7X_LANES - MLA_ROPE)))
    w1p = jnp.pad(hf_w1, ((0, 0), (0, V7X_LANES - HY_EMB), (0, 0)))
    w3r = jnp.swapaxes(hf_w3.reshape(depth, HY_FW, 2 * HY_ORDER, HY_C), 1, 2)
    bands = jnp.linspace(1e-4, HY_BANDS - 1, HY_BANDS, dtype=f32)
    band_row = jnp.concatenate([jnp.zeros((1,), f32), bands, bands,
                                jnp.zeros((V7X_LANES - HY_EMB,), f32)]).reshape(1, V7X_LANES)
    delta_row = jnp.abs(jnp.linspace(math.log(HY_DECAY_TARGET) / HY_DECAY_PCT_MIN,
                                     math.log(HY_DECAY_TARGET) / HY_DECAY_PCT_MAX, HY_C, dtype=f32)).reshape(1, HY_C)

    xs = jnp.concatenate([x.reshape(n_lat, d), ctx.reshape(n_ctx, d)], axis=0)
    tpos = jnp.arange(seq, dtype=jnp.int32)
    zpad = jnp.zeros((n_ctx,), f32)
    pos_row = jnp.concatenate([jnp.tile((tpos // GRID_W).astype(f32), batch), zpad]).reshape(t_all, 1)
    pos_col = jnp.concatenate([jnp.tile((tpos % GRID_W).astype(f32), batch), zpad]).reshape(t_all, 1)
    cond8 = jnp.concatenate([c, c_ctx[None, :], jnp.zeros((8 - batch - 1, d), f32)], axis=0)

    mod = ada_modulation(cond8, ada_down, ada_up, ada_b)
    mod5 = mod.reshape(depth, 8, N_MOD, 1, d)
    cos_a, sin_a, cos_c, sin_c = rope_tables(pos_row, pos_col, tm_ew)

    delta = None
    for l in range(depth):
        if l == 0:
            (h,) = prenorm(xs, mod5, l, grp(tm_ew), tm_ew, norm_g=norm1_g[l], shift_idx=0, scale_idx=1)
        else:
            xs, h = prenorm(xs, mod5, l, grp(tm_ew), tm_ew, delta=delta, gate_idx=5, gate_layer=l - 1,
                            norm_g=norm1_g[l], shift_idx=0, scale_idx=1)
        p = matmul(h, w_in_p[l], tm, tn_in)

        qa, ka = gqa_prep(p, gqa_q_norm[l], gqa_k_norm[l], cos_a, sin_a, tm_ew)
        fa = dict(batch=batch, seq=seq, ctx_len=ctx_len, kv_heads=GQA_KV_HEADS, group=GQA_GROUP,
                  dq=HEAD_DIM, dv=HEAD_DIM, tk=tk)
        att_a = flash_attention(qa, ka, p, VA_OFF, tq=tq_a, **fa)
        att_a = flash_attention(qa, ka, p, VA_OFF, tq=ctx_len, prev=att_a, **fa)

        qc, kc, vc = mla_prep(p, mla_kv_norm[l], w_kvb_p[l], gq_pad[l], gk_nope[l], gk_rope[l], cos_c, sin_c, tm_ew)
        fc = dict(batch=batch, seq=seq, ctx_len=ctx_len, kv_heads=MLA_HEADS, group=1, dq=MLA_QPAD, dv=MLA_V, tk=tk)
        att_c = flash_attention(qc, kc, vc, 0, tq=tq_c, **fc)
        att_c = flash_attention(qc, kc, vc, 0, tq=ctx_len, prev=att_c, **fc)

        hf = (band_row, w1p[l], hf_b1[l].reshape(1, -1), hf_w2[l], hf_b2[l].reshape(1, -1), w3r[l], hf_freq[l],
              delta_row)
        uc_lat = short_conv(p, hy_conv_w[l], hy_conv_b[l], 0, n_lat, seq, 512)
        uc_ctx = short_conv(p, hy_conv_w[l], hy_conv_b[l], n_lat, n_ctx, ctx_len, 512)
        hy = jnp.concatenate([hyena_mixer(uc_lat, seq, hf, hy_skip[l]),
                              hyena_mixer(uc_ctx, ctx_len, hf, hy_skip[l])], axis=0)

        delta = merge_out(p, att_a, hy, att_c, wg_b[l], bg[l], wb_b[l], wo_b[l], tm, tj)
        xs, h2 = prenorm(xs, mod5, l, grp(tm_ew), tm_ew, delta=delta, gate_idx=2, gate_layer=l,
                         norm_g=norm2_g[l], shift_idx=3, scale_idx=4)
        delta = ffn(h2, wgu_b[l], wd_b[l], tm, th)

    (out,) = prenorm(xs, mod5, depth - 1, grp(tm_ew), tm_ew, delta=delta, gate_idx=5, gate_layer=depth - 1,
                     rows=n_lat)
    return out.reshape(batch, seq, d)
```

```python
import functools
import math

import jax
import jax.numpy as jnp
import numpy as np
from jax import lax
from jax.experimental import pallas as pl
from jax.experimental.pallas import tpu as pltpu

f32 = jnp.float32
bf16 = jnp.bfloat16

GRID_W = 64
ROPE_THETA = 10000.0
EPS = 1e-6
HEAD_DIM = 128
GQA_HEADS = 8
GQA_KV_HEADS = 2
GQA_GROUP = GQA_HEADS // GQA_KV_HEADS
HY_C = 1024
HY_ORDER = 2
HY_EMB = 33
HY_BANDS = (HY_EMB - 1) // 2
HY_FW = 64
HY_DECAY_TARGET = 1e-2
HY_DECAY_PCT_MIN = 0.3
HY_DECAY_PCT_MAX = 1.5
MLA_HEADS = 8
MLA_NOPE = 128
MLA_ROPE = 64
MLA_QK = MLA_NOPE + MLA_ROPE
MLA_V = 128
MLA_RANK = 512
N_BRANCH = 3
BRANCH_W = 1024
GATE_RANK = 256
ADA_RANK = 256
N_MOD = 6
IN_COLS = 6976

V7X_LANES = 128
V7X_SUBLANES = 8
V7X_VMEM_BYTES = 64 * 1024 * 1024
MIB = 1024 * 1024

MLA_QPAD = 2 * V7X_LANES
QC_OFF = 0
QA_OFF = QC_OFF + MLA_HEADS * MLA_QPAD
KA_OFF = QA_OFF + GQA_HEADS * HEAD_DIM
VA_OFF = KA_OFF + GQA_KV_HEADS * HEAD_DIM
HY_OFF = VA_OFF + GQA_KV_HEADS * HEAD_DIM
CKV_OFF = HY_OFF + 3 * HY_C
GATE_OFF = CKV_OFF + MLA_RANK
KR_OFF = GATE_OFF + GATE_RANK
NP_COLS = 7680

LOG2E = math.log2(math.e)


def _cparams(sem, vmem_mib):
    return pltpu.CompilerParams(dimension_semantics=sem, vmem_limit_bytes=int(vmem_mib * MIB))


def _row_tile(s, nctx, cands):
    for t in cands:
        if s % t == 0 and nctx % t == 0:
            return t
    raise ValueError("no row tile fits")


def _ada_kernel(cond_ref, down_ref, up_ref, b_ref, o_ref, t_ref):
    @pl.when(pl.program_id(1) == 0)
    def _():
        c = cond_ref[...]
        c = c * (1.0 / (1.0 + jnp.exp(-c)))
        t_ref[...] = jnp.dot(c, down_ref[...], precision=lax.Precision.HIGHEST, preferred_element_type=f32)

    o_ref[...] = jnp.dot(t_ref[...], up_ref[...], precision=lax.Precision.HIGHEST,
                         preferred_element_type=f32) + b_ref[...]


def ada_modulation(cond8, ada_down, ada_up, ada_b):
    depth, d, _ = ada_down.shape
    n = ada_up.shape[2]
    tn = 2048 if n % 2048 == 0 else 512
    return pl.pallas_call(
        _ada_kernel,
        grid=(depth, n // tn),
        in_specs=[
            pl.BlockSpec((8, d), lambda l, j: (0, 0)),
            pl.BlockSpec((None, d, ADA_RANK), lambda l, j: (l, 0, 0)),
            pl.BlockSpec((None, ADA_RANK, tn), lambda l, j: (l, 0, j)),
            pl.BlockSpec((None, 1, tn), lambda l, j: (l, 0, j)),
        ],
        out_specs=pl.BlockSpec((None, 8, tn), lambda l, j: (l, 0, j)),
        out_shape=jax.ShapeDtypeStruct((depth, 8, n), f32),
        scratch_shapes=[pltpu.VMEM((8, ADA_RANK), f32)],
        compiler_params=_cparams(("arbitrary", "arbitrary"), 40),
        name="ada_modulation",
    )(cond8, ada_down, ada_up, ada_b.reshape(depth, 1, n))


def _rope_table_kernel(pr_ref, pc_ref, ca_ref, sa_ref, cc_ref, sc_ref):
    shape = ca_ref.shape
    lane = lax.broadcasted_iota(jnp.int32, shape, 1)
    pr = jnp.broadcast_to(pr_ref[...], shape)
    pc = jnp.broadcast_to(pc_ref[...], shape)
    log_theta = math.log(ROPE_THETA)
    fa = jnp.exp((lane & 31).astype(f32) * (-log_theta / 32.0))
    ang = jnp.where(lane < 64, pr, pc) * fa
    ca_ref[...] = jnp.cos(ang)
    sa_ref[...] = jnp.where((lane & 63) < 32, -1.0, 1.0) * jnp.sin(ang)
    fc = jnp.exp((lane & 15).astype(f32) * (-log_theta / 16.0))
    angc = jnp.where(lane < 64, jnp.where(lane < 32, pr, pc) * fc, 0.0)
    cc_ref[...] = jnp.cos(angc)
    sc_ref[...] = jnp.where((lane & 31) < 16, -1.0, 1.0) * jnp.sin(angc)


def rope_tables(pos_row, pos_col, tm):
    t = pos_row.shape[0]
    spec1 = pl.BlockSpec((tm, 1), lambda i: (i, 0))
    spec = pl.BlockSpec((tm, V7X_LANES), lambda i: (i, 0))
    sh = jax.ShapeDtypeStruct((t, V7X_LANES), f32)
    return pl.pallas_call(
        _rope_table_kernel, grid=(t // tm,), in_specs=[spec1, spec1], out_specs=[spec] * 4,
        out_shape=[sh] * 4, compiler_params=_cparams(("arbitrary",), 32), name="rope_tables",
    )(pos_row, pos_col)


def _prenorm_kernel(*refs, has_delta, want_h):
    if has_delta:
        x_ref, d_ref, gate_ref = refs[:3]
        rest = refs[3:]
    else:
        x_ref = refs[0]
        rest = refs[1:]
    x = x_ref[...]
    if has_delta:
        x = x + gate_ref[...] * d_ref[...].astype(f32)
    if want_h:
        g_ref, shift_ref, scale_ref = rest[:3]
        outs = rest[3:]
    else:
        outs = rest
    k = 0
    if has_delta:
        outs[k][...] = x
        k += 1
    if want_h:
        y = x * lax.rsqrt(jnp.mean(x * x, axis=-1, keepdims=True) + EPS)
        y = y * g_ref[...]
        outs[k][...] = (y * (1.0 + scale_ref[...]) + shift_ref[...]).astype(bf16)


def prenorm(x, mod5, layer, grp_of_block, tm, *, delta=None, gate_idx=None, gate_layer=None,
            norm_g=None, shift_idx=None, scale_idx=None, rows=None):
    t, d = x.shape
    rows = t if rows is None else rows
    has_delta = delta is not None
    want_h = norm_g is not None
    row_spec = pl.BlockSpec((tm, d), lambda i: (i, 0))

    def mod_spec(lyr, which):
        return pl.BlockSpec((None, None, None, 1, d), lambda i: (lyr, grp_of_block(i), which, 0, 0))

    in_specs, args = [row_spec], [x]
    if has_delta:
        in_specs += [row_spec, mod_spec(gate_layer, gate_idx)]
        args += [delta, mod5]
    if want_h:
        in_specs += [pl.BlockSpec((1, d), lambda i: (0, 0)), mod_spec(layer, shift_idx), mod_spec(layer, scale_idx)]
        args += [norm_g.reshape(1, d), mod5, mod5]
    out_specs, out_shape = [], []
    if has_delta:
        out_specs.append(row_spec)
        out_shape.append(jax.ShapeDtypeStruct((rows, d), f32))
    if want_h:
        out_specs.append(row_spec)
        out_shape.append(jax.ShapeDtypeStruct((rows, d), bf16))
    return pl.pallas_call(
        functools.partial(_prenorm_kernel, has_delta=has_delta, want_h=want_h),
        grid=(rows // tm,), in_specs=in_specs, out_specs=out_specs, out_shape=out_shape,
        compiler_params=_cparams(("arbitrary",), 48), name="prenorm",
    )(*args)


def _matmul_kernel(a_ref, w_ref, o_ref):
    o_ref[...] = jnp.dot(a_ref[...], w_ref[...], preferred_element_type=f32).astype(o_ref.dtype)


def matmul(a, w, layer, tm, tn):
    t, k = a.shape
    n = w.shape[2]
    return pl.pallas_call(
        _matmul_kernel, grid=(t // tm, n // tn),
        in_specs=[pl.BlockSpec((tm, k), lambda i, j: (i, 0)),
                  pl.BlockSpec((None, k, tn), lambda i, j: (layer, 0, j))],
        out_specs=pl.BlockSpec((tm, tn), lambda i, j: (i, j)),
        out_shape=jax.ShapeDtypeStruct((t, n), bf16),
        compiler_params=_cparams(("arbitrary", "arbitrary"), 48), name="in_proj",
    )(a, w)


def _rope128(x, cos, sin_signed, half):
    lane = lax.broadcasted_iota(jnp.int32, x.shape, 1)
    first = (lane & (2 * half - 1)) < half
    swapped = jnp.where(first, pltpu.roll(x, V7X_LANES - half, 1), pltpu.roll(x, half, 1))
    return x * cos + swapped * sin_signed


def _gqa_prep_kernel(q_ref, kv_ref, gq_ref, gk_ref, cos_ref, sin_ref, qo_ref, ko_ref):
    cos, sin = cos_ref[...], sin_ref[...]
    qscale = HEAD_DIM ** -0.5 * LOG2E
    for h in range(GQA_HEADS):
        sl = slice(h * HEAD_DIM, (h + 1) * HEAD_DIM)
        q = q_ref[:, sl].astype(f32)
        q = q * lax.rsqrt(jnp.mean(q * q, axis=-1, keepdims=True) + EPS) * gq_ref[...]
        qo_ref[:, sl] = (_rope128(q, cos, sin, 32) * qscale).astype(bf16)
    for h in range(GQA_KV_HEADS):
        sl = slice(h * HEAD_DIM, (h + 1) * HEAD_DIM)
        k = kv_ref[:, sl].astype(f32)
        k = k * lax.rsqrt(jnp.mean(k * k, axis=-1, keepdims=True) + EPS) * gk_ref[...]
        ko_ref[:, sl] = _rope128(k, cos, sin, 32).astype(bf16)


def gqa_prep(p, gq, gk, cos_a, sin_a, tm):
    t = p.shape[0]
    qw, kw = GQA_HEADS * HEAD_DIM, GQA_KV_HEADS * HEAD_DIM
    tab = pl.BlockSpec((tm, V7X_LANES), lambda i: (i, 0))
    vec = pl.BlockSpec((1, HEAD_DIM), lambda i: (0, 0))
    return pl.pallas_call(
        _gqa_prep_kernel, grid=(t // tm,),
        in_specs=[pl.BlockSpec((tm, qw), lambda i: (i, QA_OFF // qw)),
                  pl.BlockSpec((tm, 2 * kw), lambda i: (i, KA_OFF // (2 * kw))), vec, vec, tab, tab],
        out_specs=[pl.BlockSpec((tm, qw), lambda i: (i, 0)), pl.BlockSpec((tm, kw), lambda i: (i, 0))],
        out_shape=[jax.ShapeDtypeStruct((t, qw), bf16), jax.ShapeDtypeStruct((t, kw), bf16)],
        compiler_params=_cparams(("arbitrary",), 32), name="gqa_prep",
    )(p, p, gq.reshape(1, -1), gk.reshape(1, -1), cos_a, sin_a)


def _mla_prep_kernel(q_ref, ckv_ref, kr_ref, kvg_ref, wkvb_ref, gq_ref, gkn_ref, gkr_ref, cos_ref, sin_ref,
                     qo_ref, ko_ref, vo_ref):
    cos, sin = cos_ref[...], sin_ref[...]
    qscale = MLA_QK ** -0.5 * LOG2E
    inv_qk = 1.0 / MLA_QK
    for h in range(MLA_HEADS):
        lo = h * MLA_QPAD
        qn = q_ref[:, lo:lo + V7X_LANES].astype(f32)
        qr = q_ref[:, lo + V7X_LANES:lo + MLA_QPAD].astype(f32)
        ss = jnp.sum(qn * qn, axis=-1, keepdims=True) + jnp.sum(qr * qr, axis=-1, keepdims=True)
        r = lax.rsqrt(ss * inv_qk + EPS) * qscale
        qo_ref[:, lo:lo + V7X_LANES] = (qn * r * gq_ref[:, :V7X_LANES]).astype(bf16)
        qo_ref[:, lo + V7X_LANES:lo + MLA_QPAD] = (_rope128(qr * gq_ref[:, V7X_LANES:], cos, sin, 16) * r).astype(bf16)
    c = ckv_ref[...].astype(f32)
    cn = c * lax.rsqrt(jnp.mean(c * c, axis=-1, keepdims=True) + EPS) * kvg_ref[...]
    kv = jnp.dot(cn.astype(bf16), wkvb_ref[...], preferred_element_type=f32)
    kr = kr_ref[...].astype(f32)
    ss_r = jnp.sum(kr * kr, axis=-1, keepdims=True)
    kr_rot = _rope128(kr * gkr_ref[...], cos, sin, 16)
    nv = MLA_HEADS * MLA_NOPE
    for h in range(MLA_HEADS):
        kn = kv[:, h * MLA_NOPE:(h + 1) * MLA_NOPE]
        r = lax.rsqrt((jnp.sum(kn * kn, axis=-1, keepdims=True) + ss_r) * inv_qk + EPS)
        lo = h * MLA_QPAD
        ko_ref[:, lo:lo + V7X_LANES] = (kn * r * gkn_ref[...]).astype(bf16)
        ko_ref[:, lo + V7X_LANES:lo + MLA_QPAD] = (kr_rot * r).astype(bf16)
    vo_ref[...] = kv[:, nv:].astype(bf16)


def mla_prep(p, kv_g, wkvb_p, gq_pad, gk_nope, gk_rope, cos_c, sin_c, tm):
    t = p.shape[0]
    qw = MLA_HEADS * MLA_QPAD
    vw = MLA_HEADS * MLA_V
    tab = pl.BlockSpec((tm, V7X_LANES), lambda i: (i, 0))

    def vec(n):
        return pl.BlockSpec((1, n), lambda i: (0, 0))

    return pl.pallas_call(
        _mla_prep_kernel, grid=(t // tm,),
        in_specs=[pl.BlockSpec((tm, qw), lambda i: (i, QC_OFF // qw)),
                  pl.BlockSpec((tm, MLA_RANK), lambda i: (i, CKV_OFF // MLA_RANK)),
                  pl.BlockSpec((tm, V7X_LANES), lambda i: (i, KR_OFF // V7X_LANES)),
                  vec(MLA_RANK), pl.BlockSpec((MLA_RANK, 2 * vw), lambda i: (0, 0)),
                  vec(MLA_QPAD), vec(V7X_LANES), vec(V7X_LANES), tab, tab],
        out_specs=[pl.BlockSpec((tm, qw), lambda i: (i, 0)), pl.BlockSpec((tm, qw), lambda i: (i, 0)),
                   pl.BlockSpec((tm, vw), lambda i: (i, 0))],
        out_shape=[jax.ShapeDtypeStruct((t, qw), bf16), jax.ShapeDtypeStruct((t, qw), bf16),
                   jax.ShapeDtypeStruct((t, vw), bf16)],
        compiler_params=_cparams(("arbitrary",), 48), name="mla_prep",
    )(p, p, p, kv_g.reshape(1, -1), wkvb_p, gq_pad.reshape(1, -1), gk_nope.reshape(1, -1),
      gk_rope.reshape(1, -1), cos_c, sin_c)


FLASH_COLS = 256


def _scores(k, qt_ref, s_ref):
    s_ref[0:k.shape[0], :] = jnp.dot(k, qt_ref[...], preferred_element_type=f32)


def _softmax_pv(s_ref, tk, v, m_ref, l_ref, acc_ref):
    vt = v.T
    step = min(FLASH_COLS, s_ref.shape[1])
    for c0 in range(0, s_ref.shape[1], step):
        cols = slice(c0, c0 + step)
        s = s_ref[0:tk, cols]
        m_prev = m_ref[:, cols]
        m_new = jnp.maximum(m_prev, jnp.max(s, axis=0, keepdims=True))
        alpha = jnp.exp2(m_prev - m_new)
        pr = jnp.exp2(s - m_new)
        l_ref[:, cols] = alpha * l_ref[:, cols] + jnp.sum(pr, axis=0, keepdims=True)
        acc_ref[:, cols] = alpha * acc_ref[:, cols] + jnp.dot(vt, pr.astype(bf16), preferred_element_type=f32)
        m_ref[:, cols] = m_new


def _flash_kernel(*refs, group, dq, dv, tq, tk, n_lat):
    if n_lat:
        q_ref, kl_ref, vl_ref, kc_ref, vc_ref, o_ref, qt_ref, m_ref, l_ref, acc_ref, s0_ref, s1_ref = refs
    else:
        q_ref, kc_ref, vc_ref, _, o_ref, qt_ref, m_ref, l_ref, acc_ref, s0_ref, s1_ref = refs
    for g in range(group):
        qt_ref[:, g * tq:(g + 1) * tq] = q_ref[:, g * dq:(g + 1) * dq].T
    m_ref[...] = jnp.full(m_ref.shape, -1e30, f32)
    l_ref[...] = jnp.zeros(l_ref.shape, f32)
    acc_ref[...] = jnp.zeros(acc_ref.shape, f32)
    n_ctx = kc_ref.shape[0]
    _scores(kc_ref[...], qt_ref, s1_ref)
    if n_lat:
        def kchunk(j):
            return kl_ref[pl.ds(pl.multiple_of(j * tk, tk), tk), :]

        def vchunk(j):
            return vl_ref[pl.ds(pl.multiple_of(j * tk, tk), tk), :]

        assert n_lat % 2 == 0
        _scores(kchunk(0), qt_ref, s0_ref)
        _softmax_pv(s1_ref, n_ctx, vc_ref[...], m_ref, l_ref, acc_ref)

        def body(i, carry):
            _scores(kchunk(2 * i + 1), qt_ref, s1_ref)
            _softmax_pv(s0_ref, tk, vchunk(2 * i), m_ref, l_ref, acc_ref)
            _scores(kchunk(2 * i + 2), qt_ref, s0_ref)
            _softmax_pv(s1_ref, tk, vchunk(2 * i + 1), m_ref, l_ref, acc_ref)
            return carry
        lax.fori_loop(0, n_lat // 2 - 1, body, 0)
        _scores(kchunk(n_lat - 1), qt_ref, s1_ref)
        _softmax_pv(s0_ref, tk, vchunk(n_lat - 2), m_ref, l_ref, acc_ref)
        _softmax_pv(s1_ref, tk, vchunk(n_lat - 1), m_ref, l_ref, acc_ref)
    else:
        _softmax_pv(s1_ref, n_ctx, vc_ref[...], m_ref, l_ref, acc_ref)
    for g in range(group):
        cols = slice(g * tq, (g + 1) * tq)
        o_ref[:, g * dv:(g + 1) * dv] = (acc_ref[:, cols] / l_ref[:, cols]).T.astype(o_ref.dtype)


def flash_attention(q, k, v, v_col_off, *, batch, seq, ctx_len, kv_heads, group, dq, dv, tq, tk, prev=None):
    t = q.shape[0]
    lat = prev is None
    ctx_blk0 = batch * seq // ctx_len
    voff = v_col_off // dv
    rows = group * tq
    kc_spec = pl.BlockSpec((ctx_len, dq), lambda b, h, i: (ctx_blk0 + b, h))
    vc_spec = pl.BlockSpec((ctx_len, dv), lambda b, h, i: (ctx_blk0 + b, voff + h))
    if lat:
        nq = seq // tq
        in_specs = [pl.BlockSpec((tq, group * dq), lambda b, h, i: (b * nq + i, h)),
                    pl.BlockSpec((seq, dq), lambda b, h, i: (b, h)),
                    pl.BlockSpec((seq, dv), lambda b, h, i: (b, voff + h)), kc_spec, vc_spec]
        args = [q, k, v, k, v]
        out_spec = pl.BlockSpec((tq, group * dv), lambda b, h, i: (b * nq + i, h))
        aliases = {}
        n_lat = seq // tk
    else:
        nq = 1
        assert tq == ctx_len
        in_specs = [pl.BlockSpec((tq, group * dq), lambda b, h, i: (ctx_blk0 + b, h)), kc_spec, vc_spec,
                    pl.BlockSpec(memory_space=pl.ANY)]
        args = [q, k, v, prev]
        out_spec = pl.BlockSpec((tq, group * dv), lambda b, h, i: (ctx_blk0 + b, h))
        aliases = {3: 0}
        n_lat = 0
    return pl.pallas_call(
        functools.partial(_flash_kernel, group=group, dq=dq, dv=dv, tq=tq, tk=tk, n_lat=n_lat),
        grid=(batch, kv_heads, nq), in_specs=in_specs, out_specs=out_spec,
        out_shape=jax.ShapeDtypeStruct((t, kv_heads * group * dv), bf16),
        scratch_shapes=[pltpu.VMEM((dq, rows), bf16), pltpu.VMEM((1, rows), f32), pltpu.VMEM((1, rows), f32),
                        pltpu.VMEM((dv, rows), f32)] + [pltpu.VMEM((max(tk, ctx_len) if lat else ctx_len, rows), f32)] * 2,
        input_output_aliases=aliases,
        compiler_params=_cparams(("arbitrary", "arbitrary", "arbitrary"), 48),
        name="flash_lat" if lat else "flash_ctx",
    )(*args)


CONV_HALO = 16


def _conv3_kernel(prev_ref, cur_ref, next_ref, w_ref, b_ref, o_ref, *, blocks_per_seq):
    rows = cur_ref.shape[0]
    pos = pl.program_id(0) % blocks_per_seq
    x = cur_ref[...].astype(f32)
    row = lax.broadcasted_iota(jnp.int32, x.shape, 0)
    prev_row = jnp.where(pos == 0, 0.0, prev_ref[...].astype(f32)[CONV_HALO - 1:CONV_HALO, :])
    next_row = jnp.where(pos == blocks_per_seq - 1, 0.0, next_ref[...].astype(f32)[0:1, :])
    xm = jnp.where(row == 0, prev_row, pltpu.roll(x, 1, 0))
    xp = jnp.where(row == rows - 1, next_row, pltpu.roll(x, rows - 1, 0))
    o_ref[...] = xm * w_ref[0:1, :] + x * w_ref[1:2, :] + xp * w_ref[2:3, :] + b_ref[...]


def short_conv(p, conv_w, conv_b, row0, nrows, seq_len, rows, cw):
    width = 3 * HY_C
    rb0 = row0 // rows
    sub = rows // CONV_HALO
    last_halo = p.shape[0] // CONV_HALO - 1
    c0 = HY_OFF // cw
    return pl.pallas_call(
        functools.partial(_conv3_kernel, blocks_per_seq=seq_len // rows),
        grid=(nrows // rows, width // cw),
        in_specs=[
            pl.BlockSpec((CONV_HALO, cw), lambda r, c: (jnp.maximum((rb0 + r) * sub - 1, 0), c0 + c)),
            pl.BlockSpec((rows, cw), lambda r, c: (rb0 + r, c0 + c)),
            pl.BlockSpec((CONV_HALO, cw), lambda r, c: (jnp.minimum((rb0 + r + 1) * sub, last_halo), c0 + c)),
            pl.BlockSpec((3, cw), lambda r, c: (0, c)),
            pl.BlockSpec((1, cw), lambda r, c: (0, c)),
        ],
        out_specs=pl.BlockSpec((rows, cw), lambda r, c: (r, c)),
        out_shape=jax.ShapeDtypeStruct((nrows, width), f32),
        compiler_params=_cparams(("arbitrary", "arbitrary"), 32), name="short_conv",
    )(p, p, p, conv_w, conv_b.reshape(1, width))


@functools.lru_cache(maxsize=None)
def _dft_tables(length):
    nb = 128 if length >= 1024 else 16
    n_fft = 2 * length
    na = n_fft // nb
    nah = na // 2
    lo = np.arange(nb, dtype=np.int64)[:, None, None]
    k1 = np.arange(na, dtype=np.int64)[None, :, None]
    hi = np.arange(nah, dtype=np.int64)[None, None, :]
    ang = 2.0 * np.pi * (((nb * hi + lo) * k1) % n_fft) / n_fft
    c, s = np.cos(ang), np.sin(ang)
    a1 = np.concatenate([np.concatenate([c, s], axis=2), np.concatenate([-s, c], axis=2)], axis=1)
    ct, st = np.swapaxes(c, 1, 2) / n_fft, np.swapaxes(s, 1, 2) / n_fft
    a3 = np.concatenate([np.concatenate([ct, -st], axis=2), np.concatenate([st, ct], axis=2)], axis=1)
    kk = np.arange(nb, dtype=np.int64)
    angb = 2.0 * np.pi * ((kk[:, None] * kk[None, :]) % nb) / nb
    cb, sb = np.cos(angb), np.sin(angb)
    mf = np.block([[cb, sb], [-sb, cb]])
    mfc = np.block([[cb, -sb], [sb, cb]])
    return dict(nb=nb, na=na, nah=nah,
                a1=jnp.asarray(a1, dtype=bf16), a1r=jnp.asarray(a1[:, :, :nah], dtype=bf16),
                a3=jnp.asarray(a3, dtype=bf16), mf=jnp.asarray(mf, dtype=bf16), mfc=jnp.asarray(mfc, dtype=bf16),
                m2f=jnp.asarray(np.concatenate([mf, mfc], axis=1), dtype=bf16))


def _tap_times(j, nb, nah):
    ridx = lax.broadcasted_iota(jnp.int32, (V7X_SUBLANES * nah, 1), 0)
    return nb * (ridx & (nah - 1)) + (j * V7X_SUBLANES + (ridx >> (nah.bit_length() - 1)))


def _filter_mlp_kernel(band_ref, w1_ref, b1_ref, w2_ref, b2_ref, freq_ref, o_ref, *, length, nb, nah):
    hp = lax.Precision.HIGHEST
    t = _tap_times(pl.program_id(0), nb, nah).astype(f32)
    t_unit = t / float(max(length - 1, 1))
    lane = lax.broadcasted_iota(jnp.int32, (t.shape[0], V7X_LANES), 1)
    ang = ((2.0 * math.pi / length) * t) * band_ref[...]
    feats = jnp.where(lane == 0, t_unit,
                      jnp.where(lane <= HY_BANDS, jnp.cos(ang), jnp.where(lane <= 2 * HY_BANDS, -jnp.sin(ang), 0.0)))
    h = jnp.sin(freq_ref[0:1, :] * (jnp.dot(feats, w1_ref[...], precision=hp, preferred_element_type=f32) + b1_ref[...]))
    o_ref[...] = jnp.sin(freq_ref[1:2, :] * (jnp.dot(h, w2_ref[...], precision=hp, preferred_element_type=f32)
                                            + b2_ref[...]))


def filter_mlp(tabs, length, band_row, w1p, b1, w2, b2, freq):
    nb, nah = tabs["nb"], tabs["nah"]
    rows = V7X_SUBLANES * nah
    full = lambda *shape: pl.BlockSpec(shape, lambda j: (0,) * len(shape))
    return pl.pallas_call(
        functools.partial(_filter_mlp_kernel, length=length, nb=nb, nah=nah), grid=(nb // V7X_SUBLANES,),
        in_specs=[full(1, V7X_LANES), full(V7X_LANES, HY_FW), full(1, HY_FW), full(HY_FW, HY_FW), full(1, HY_FW),
                  full(2, HY_FW)],
        out_specs=pl.BlockSpec((rows, HY_FW), lambda j: (j, 0)),
        out_shape=jax.ShapeDtypeStruct((length, HY_FW), f32),
        compiler_params=_cparams(("arbitrary",), 32), name="hy_filter_mlp",
    )(band_row, w1p, b1, w2, b2, freq)


def _filter_stage1_kernel(a1r_ref, h_ref, w3_ref, delta_ref, o_ref, ss_ref, *, length, nb, nah):
    j = pl.program_id(1)
    hp = lax.Precision.HIGHEST
    t_int = _tap_times(j, nb, nah)
    t_unit = t_int.astype(f32) / float(max(length - 1, 1))
    decay = jnp.exp(-t_unit * delta_ref[...])
    h = h_ref[...]

    @pl.when(j == 0)
    def _():
        ss_ref[...] = jnp.zeros(ss_ref.shape, f32)

    na = 2 * nah
    for o in range(HY_ORDER):
        fwd = jnp.dot(h, w3_ref[2 * o], precision=hp, preferred_element_type=f32) * decay
        bwd = jnp.dot(h, w3_ref[2 * o + 1], precision=hp, preferred_element_type=f32) * decay
        bwd = jnp.where(t_int == 0, 0.0, bwd)
        ss_ref[o:o + 1, :] += jnp.sum(fwd * fwd + bwd * bwd, axis=0, keepdims=True)
        fb, bb = fwd.astype(bf16), bwd.astype(bf16)
        for l in range(V7X_SUBLANES):
            a = a1r_ref[l]
            ff = jnp.dot(a, fb[l * nah:(l + 1) * nah], preferred_element_type=f32)
            gg = jnp.dot(a, bb[l * nah:(l + 1) * nah], preferred_element_type=f32)
            o_ref[o, l, 0] = ff[:na]
            o_ref[o, l, 1] = ff[na:]
            o_ref[o, l, 2] = gg[:na]
            o_ref[o, l, 3] = -gg[na:]


def filter_stage1(tabs, length, hmlp, w3r, delta_row):
    nb, na, nah = tabs["nb"], tabs["na"], tabs["nah"]
    ncb = HY_C // V7X_LANES
    return pl.pallas_call(
        functools.partial(_filter_stage1_kernel, length=length, nb=nb, nah=nah),
        grid=(ncb, nb // V7X_SUBLANES),
        in_specs=[pl.BlockSpec((V7X_SUBLANES, 2 * na, nah), lambda c, j: (j, 0, 0)),
                  pl.BlockSpec((V7X_SUBLANES * nah, HY_FW), lambda c, j: (j, 0)),
                  pl.BlockSpec((2 * HY_ORDER, HY_FW, V7X_LANES), lambda c, j: (0, 0, c)),
                  pl.BlockSpec((1, V7X_LANES), lambda c, j: (0, c))],
        out_specs=[pl.BlockSpec((HY_ORDER, V7X_SUBLANES, 4, na, V7X_LANES), lambda c, j: (0, j, 0, 0, c)),
                   pl.BlockSpec((V7X_SUBLANES, V7X_LANES), lambda c, j: (0, c))],
        out_shape=[jax.ShapeDtypeStruct((HY_ORDER, nb, 4, na, HY_C), f32),
                   jax.ShapeDtypeStruct((V7X_SUBLANES, HY_C), f32)],
        compiler_params=_cparams(("arbitrary", "arbitrary"), 48), name="hy_filter_s1",
    )(tabs["a1r"], hmlp, w3r, delta_row)


def _filter_stage2_kernel(x_ref, m_ref, o_ref):
    for kl in range(V7X_SUBLANES):
        rhs = jnp.concatenate([x_ref[:, q, kl, :] for q in range(4)], axis=0).astype(bf16)
        o_ref[kl] = jnp.dot(m_ref[...], rhs, preferred_element_type=f32)


def filter_stage2(tabs, hin):
    nb, na = tabs["nb"], tabs["na"]
    ncb = HY_C // V7X_LANES
    return pl.pallas_call(
        _filter_stage2_kernel, grid=(HY_ORDER, ncb, na // V7X_SUBLANES),
        in_specs=[pl.BlockSpec((None, nb, 4, V7X_SUBLANES, V7X_LANES), lambda o, c, j: (o, 0, 0, j, c)),
                  pl.BlockSpec((2 * nb, 4 * nb), lambda o, c, j: (0, 0))],
        out_specs=pl.BlockSpec((None, V7X_SUBLANES, 2 * nb, V7X_LANES), lambda o, c, j: (o, j, 0, c)),
        out_shape=jax.ShapeDtypeStruct((HY_ORDER, na, 2 * nb, HY_C), f32),
        compiler_params=_cparams(("arbitrary", "arbitrary", "arbitrary"), 48), name="hy_filter_s2",
    )(hin, tabs["m2f"])


def _conv_stage1_kernel(zr_ref, zi_ref, a1_ref, o_ref, *, na):
    for l in range(V7X_SUBLANES):
        rhs = jnp.concatenate([zr_ref[:, l, :], zi_ref[:, l, :]], axis=0).astype(bf16)
        y = jnp.dot(a1_ref[l], rhs, preferred_element_type=f32)
        o_ref[l, 0] = y[:na]
        o_ref[l, 1] = y[na:]


def conv_stage1(tabs, z4, col_off):
    nb, na, nah = tabs["nb"], tabs["na"], tabs["nah"]
    ncb = HY_C // V7X_LANES
    c0 = col_off // V7X_LANES
    return pl.pallas_call(
        functools.partial(_conv_stage1_kernel, na=na), grid=(ncb, nb // V7X_SUBLANES),
        in_specs=[pl.BlockSpec((None, nah, V7X_SUBLANES, V7X_LANES), lambda c, j: (0, 0, j, c0 + c)),
                  pl.BlockSpec((None, nah, V7X_SUBLANES, V7X_LANES), lambda c, j: (1, 0, j, c0 + c)),
                  pl.BlockSpec((V7X_SUBLANES, 2 * na, na), lambda c, j: (j, 0, 0))],
        out_specs=pl.BlockSpec((V7X_SUBLANES, 2, na, V7X_LANES), lambda c, j: (j, 0, 0, c)),
        out_shape=jax.ShapeDtypeStruct((nb, 2, na, HY_C), f32),
        compiler_params=_cparams(("arbitrary", "arbitrary"), 48), name="hy_conv_s1",
    )(z4, z4, tabs["a1"])


def _conv_stage2_kernel(x_ref, h_ref, mf_ref, mfc_ref, o_ref, *, nb):
    for kl in range(V7X_SUBLANES):
        rhs = jnp.concatenate([x_ref[:, 0, kl, :], x_ref[:, 1, kl, :]], axis=0).astype(bf16)
        x = jnp.dot(mf_ref[...], rhs, preferred_element_type=f32)
        xr, xi = x[:nb], x[nb:]
        hr, hi = h_ref[kl, :nb, :], h_ref[kl, nb:, :]
        y = jnp.concatenate([xr * hr - xi * hi, xr * hi + xi * hr], axis=0).astype(bf16)
        e = jnp.dot(mfc_ref[...], y, preferred_element_type=f32)
        o_ref[kl, 0] = e[:nb]
        o_ref[kl, 1] = e[nb:]


def conv_stage2(tabs, din, hspec, order):
    nb, na = tabs["nb"], tabs["na"]
    ncb = HY_C // V7X_LANES
    return pl.pallas_call(
        functools.partial(_conv_stage2_kernel, nb=nb), grid=(ncb, na // V7X_SUBLANES),
        in_specs=[pl.BlockSpec((nb, 2, V7X_SUBLANES, V7X_LANES), lambda c, j: (0, 0, j, c)),
                  pl.BlockSpec((None, V7X_SUBLANES, 2 * nb, V7X_LANES), lambda c, j: (order, j, 0, c)),
                  pl.BlockSpec((2 * nb, 2 * nb), lambda c, j: (0, 0)),
                  pl.BlockSpec((2 * nb, 2 * nb), lambda c, j: (0, 0))],
        out_specs=pl.BlockSpec((V7X_SUBLANES, 2, nb, V7X_LANES), lambda c, j: (j, 0, 0, c)),
        out_shape=jax.ShapeDtypeStruct((na, 2, nb, HY_C), f32),
        compiler_params=_cparams(("arbitrary", "arbitrary"), 48), name="hy_conv_s2",
    )(din, hspec, tabs["mf"], tabs["mfc"])


def _conv_stage3_kernel(e_ref, a3_ref, zr_ref, zi_ref, gr_ref, gi_ref, ss_ref, skip_ref, o_ref, *, nah, order):
    rs = lax.rsqrt(ss_ref[order:order + 1, :] + EPS)
    skip = skip_ref[order:order + 1, :]
    for l in range(V7X_SUBLANES):
        rhs = jnp.concatenate([e_ref[:, 0, l, :], e_ref[:, 1, l, :]], axis=0).astype(bf16)
        y = jnp.dot(a3_ref[l], rhs, preferred_element_type=f32) * rs
        o_ref[0, :, l, :] = gr_ref[:, l, :] * (y[:nah] + skip * zr_ref[:, l, :])
        o_ref[1, :, l, :] = gi_ref[:, l, :] * (y[nah:] + skip * zi_ref[:, l, :])


def conv_stage3(tabs, ein, z4, z_off, g4, g_off, ss, skip, order):
    nb, na, nah = tabs["nb"], tabs["na"], tabs["nah"]
    ncb = HY_C // V7X_LANES
    zc, gc = z_off // V7X_LANES, g_off // V7X_LANES

    def slab(b, c0):
        return pl.BlockSpec((None, nah, V7X_SUBLANES, V7X_LANES), lambda c, j: (b, 0, j, c0 + c))

    return pl.pallas_call(
        functools.partial(_conv_stage3_kernel, nah=nah, order=order), grid=(ncb, nb // V7X_SUBLANES),
        in_specs=[pl.BlockSpec((na, 2, V7X_SUBLANES, V7X_LANES), lambda c, j: (0, 0, j, c)),
                  pl.BlockSpec((V7X_SUBLANES, na, 2 * na), lambda c, j: (j, 0, 0)),
                  slab(0, zc), slab(1, zc), slab(0, gc), slab(1, gc),
                  pl.BlockSpec((V7X_SUBLANES, V7X_LANES), lambda c, j: (0, c)),
                  pl.BlockSpec((HY_ORDER, V7X_LANES), lambda c, j: (0, c))],
        out_specs=pl.BlockSpec((2, nah, V7X_SUBLANES, V7X_LANES), lambda c, j: (0, 0, j, c)),
        out_shape=jax.ShapeDtypeStruct((2, nah, nb, HY_C), f32),
        compiler_params=_cparams(("arbitrary", "arbitrary"), 48), name="hy_conv_s3",
    )(ein, tabs["a3"], z4, z4, g4, g4, ss, skip)


def hyena_mixer(uc, length, hf, skip):
    tabs = _dft_tables(length)
    nb, nah = tabs["nb"], tabs["nah"]
    band_row, w1p, b1, w2, b2, w3r, freq, delta_row = hf
    hmlp = filter_mlp(tabs, length, band_row, w1p, b1, w2, b2, freq)
    hin, ss = filter_stage1(tabs, length, hmlp, w3r, delta_row)
    hspec = filter_stage2(tabs, hin)
    u4 = uc.reshape(2, nah, nb, 3 * HY_C)
    z4, z_off = u4, 0
    for o in range(HY_ORDER):
        din = conv_stage1(tabs, z4, z_off)
        ein = conv_stage2(tabs, din, hspec, o)
        z4 = conv_stage3(tabs, ein, z4, z_off, u4, (o + 1) * HY_C, ss, skip, o)
        z_off = 0
    return z4.reshape(2 * length, HY_C)


def _merge_kernel(gl_ref, a_ref, hy_ref, c_ref, wg_ref, bg_ref, wb_ref, wo_ref, o_ref, acc_ref):
    j = pl.program_id(1)

    @pl.when(j == 0)
    def _():
        acc_ref[...] = jnp.zeros(acc_ref.shape, f32)

    gl = gl_ref[...]
    branches = (a_ref[...], hy_ref[...].astype(bf16), c_ref[...])
    y = None
    for n in range(N_BRANCH):
        z = jnp.dot(gl, wg_ref[n], preferred_element_type=f32) + bg_ref[n]
        g = 1.0 / (1.0 + jnp.exp(-z))
        term = g * jnp.dot(branches[n], wb_ref[n], preferred_element_type=f32)
        y = term if y is None else y + term
    acc_ref[...] += jnp.dot(y.astype(bf16), wo_ref[...], preferred_element_type=f32)

    @pl.when(j == pl.num_programs(1) - 1)
    def _():
        o_ref[...] = acc_ref[...].astype(o_ref.dtype)


def merge_out(p, a, hy, c, wg, bg, wb, wo, layer, tm, tj):
    t = p.shape[0]
    d = wo.shape[2]
    row = lambda w: pl.BlockSpec((tm, w), lambda i, j: (i, 0))
    return pl.pallas_call(
        _merge_kernel, grid=(t // tm, d // tj),
        in_specs=[pl.BlockSpec((tm, GATE_RANK), lambda i, j: (i, GATE_OFF // GATE_RANK)),
                  row(BRANCH_W), row(BRANCH_W), row(BRANCH_W),
                  pl.BlockSpec((None, N_BRANCH, GATE_RANK, tj), lambda i, j: (layer, 0, 0, j)),
                  pl.BlockSpec((None, N_BRANCH, 1, tj), lambda i, j: (layer, 0, 0, j)),
                  pl.BlockSpec((None, N_BRANCH, BRANCH_W, tj), lambda i, j: (layer, 0, 0, j)),
                  pl.BlockSpec((None, tj, d), lambda i, j: (layer, j, 0))],
        out_specs=pl.BlockSpec((tm, d), lambda i, j: (i, 0)),
        out_shape=jax.ShapeDtypeStruct((t, d), bf16),
        scratch_shapes=[pltpu.VMEM((tm, d), f32)],
        compiler_params=_cparams(("arbitrary", "arbitrary"), 56), name="merge_out",
    )(p, a, hy, c, wg, bg, wb, wo)


def _ffn_kernel(h_ref, wg_ref, wu_ref, wd_ref, o_ref, acc_ref):
    j = pl.program_id(1)

    @pl.when(j == 0)
    def _():
        acc_ref[...] = jnp.zeros(acc_ref.shape, f32)

    h = h_ref[...]
    g = jnp.dot(h, wg_ref[...], preferred_element_type=f32)
    u = jnp.dot(h, wu_ref[...], preferred_element_type=f32)
    a = (g * (1.0 / (1.0 + jnp.exp(-g))) * u).astype(bf16)
    acc_ref[...] += jnp.dot(a, wd_ref[...], preferred_element_type=f32)

    @pl.when(j == pl.num_programs(1) - 1)
    def _():
        o_ref[...] = acc_ref[...].astype(o_ref.dtype)


def ffn(h, w_gu, w_down, layer, tm, th):
    t, d = h.shape
    hidden = w_down.shape[1]
    nh = hidden // th
    return pl.pallas_call(
        _ffn_kernel, grid=(t // tm, nh),
        in_specs=[pl.BlockSpec((tm, d), lambda i, j: (i, 0)),
                  pl.BlockSpec((None, d, th), lambda i, j: (layer, 0, j)),
                  pl.BlockSpec((None, d, th), lambda i, j: (layer, 0, nh + j)),
                  pl.BlockSpec((None, th, d), lambda i, j: (layer, j, 0))],
        out_specs=pl.BlockSpec((tm, d), lambda i, j: (i, 0)),
        out_shape=jax.ShapeDtypeStruct((t, d), bf16),
        scratch_shapes=[pltpu.VMEM((tm, d), f32)],
        compiler_params=_cparams(("arbitrary", "arbitrary"), 56), name="ffn",
    )(h, w_gu, w_gu, w_down)


def _pack_w_in(w_in):
    depth, d, _ = w_in.shape
    o_qa, o_ka, o_hy = 0, GQA_HEADS * HEAD_DIM, (GQA_HEADS + 2 * GQA_KV_HEADS) * HEAD_DIM
    o_qc = o_hy + 3 * HY_C
    o_ckv = o_qc + MLA_HEADS * MLA_QK
    o_kr = o_ckv + MLA_RANK
    o_gate = o_kr + MLA_ROPE
    qc = w_in[:, :, o_qc:o_ckv].reshape(depth, d, MLA_HEADS, MLA_QK)
    qc = jnp.pad(qc, ((0, 0), (0, 0), (0, 0), (0, MLA_QPAD - MLA_QK))).reshape(depth, d, MLA_HEADS * MLA_QPAD)
    tail = NP_COLS - KR_OFF - MLA_ROPE
    parts = [qc, w_in[:, :, o_qa:o_ka], w_in[:, :, o_ka:o_hy], w_in[:, :, o_hy:o_qc], w_in[:, :, o_ckv:o_kr],
             w_in[:, :, o_gate:o_gate + GATE_RANK], w_in[:, :, o_kr:o_gate], jnp.zeros((depth, d, tail), w_in.dtype)]
    return jnp.concatenate(parts, axis=2).astype(bf16)


def _pack_w_kvb(w):
    depth = w.shape[0]
    w = w.reshape(depth, MLA_RANK, MLA_HEADS, 2, MLA_NOPE)
    return jnp.swapaxes(w, 2, 3).reshape(depth, MLA_RANK, 2 * MLA_HEADS * MLA_NOPE).astype(bf16)


def kernel(x, c, ctx, c_ctx, norm1_g, norm2_g, ada_down, ada_up, ada_b, w_in, gqa_q_norm, gqa_k_norm, hy_conv_w,
           hy_conv_b, hf_w1, hf_b1, hf_w2, hf_b2, hf_w3, hf_freq, hy_skip, mla_kv_norm, mla_w_kvb, mla_q_norm,
           mla_k_norm, w_gate_up, b_gate, w_branch, w_out, ffn_w_gu, ffn_w_down):
    batch, seq, d = x.shape
    ctx_len = ctx.shape[1]
    depth = w_in.shape[0]
    hidden = ffn_w_down.shape[1]
    assert batch == 2, "the Hyena long convolution packs the two batches as one complex sequence"
    assert w_in.shape[2] == IN_COLS and seq % GRID_W == 0
    n_lat, n_ctx = batch * seq, batch * ctx_len
    t_all = n_lat + n_ctx

    tm = _row_tile(seq, n_ctx, (512, 256, 128))
    tm_ew = _row_tile(seq, n_ctx, (256, 128))
    tn_in = 512
    tj = 512
    th = 256
    tq_a = min(256, ctx_len)
    tq_c = min(1024, seq)
    tk = min(512, seq)

    def grp(tile):
        per = seq // tile
        return lambda i: jnp.minimum(i // per, batch)

    w_in_p = _pack_w_in(w_in)
    w_kvb_p = _pack_w_kvb(mla_w_kvb)
    wg_b, wb_b, wo_b = w_gate_up.astype(bf16), w_branch.astype(bf16), w_out.astype(bf16)
    wgu_b, wd_b = ffn_w_gu.astype(bf16), ffn_w_down.astype(bf16)
    bg = b_gate.reshape(depth, N_BRANCH, 1, d)
    gq_pad = jnp.pad(mla_q_norm, ((0, 0), (0, MLA_QPAD - MLA_QK)))
    gk_nope = mla_k_norm[:, :MLA_NOPE]
    gk_rope = jnp.pad(mla_k_norm[:, MLA_NOPE:], ((0, 0), (0, V7X_LANES - MLA_ROPE)))
    w1p = jnp.pad(hf_w1, ((0, 0), (0, V7X_LANES - HY_EMB), (0, 0)))
    w3r = jnp.swapaxes(hf_w3.reshape(depth, HY_FW, 2 * HY_ORDER, HY_C), 1, 2)
    bands = jnp.linspace(1e-4, HY_BANDS - 1, HY_BANDS, dtype=f32)
    band_row = jnp.concatenate([jnp.zeros((1,), f32), bands, bands,
                                jnp.zeros((V7X_LANES - HY_EMB,), f32)]).reshape(1, V7X_LANES)
    delta_row = jnp.abs(jnp.linspace(math.log(HY_DECAY_TARGET) / HY_DECAY_PCT_MIN,
                                     math.log(HY_DECAY_TARGET) / HY_DECAY_PCT_MAX, HY_C, dtype=f32)).reshape(1, HY_C)

    xs = jnp.concatenate([x.reshape(n_lat, d), ctx.reshape(n_ctx, d)], axis=0)
    tpos = jnp.arange(seq, dtype=jnp.int32)
    zpad = jnp.zeros((n_ctx,), f32)
    pos_row = jnp.concatenate([jnp.tile((tpos // GRID_W).astype(f32), batch), zpad]).reshape(t_all, 1)
    pos_col = jnp.concatenate([jnp.tile((tpos % GRID_W).astype(f32), batch), zpad]).reshape(t_all, 1)
    cond8 = jnp.concatenate([c, c_ctx[None, :], jnp.zeros((8 - batch - 1, d), f32)], axis=0)

    mod = ada_modulation(cond8, ada_down, ada_up, ada_b)
    mod5 = mod.reshape(depth, 8, N_MOD, 1, d)
    cos_a, sin_a, cos_c, sin_c = rope_tables(pos_row, pos_col, tm_ew)

    delta = None
    for l in range(depth):
        if l == 0:
            (h,) = prenorm(xs, mod5, l, grp(tm_ew), tm_ew, norm_g=norm1_g[l], shift_idx=0, scale_idx=1)
        else:
            xs, h = prenorm(xs, mod5, l, grp(tm_ew), tm_ew, delta=delta, gate_idx=5, gate_layer=l - 1,
                            norm_g=norm1_g[l], shift_idx=0, scale_idx=1)
        p = matmul(h, w_in_p, l, tm, tn_in)

        qa, ka = gqa_prep(p, gqa_q_norm[l], gqa_k_norm[l], cos_a, sin_a, tm_ew)
        fa = dict(batch=batch, seq=seq, ctx_len=ctx_len, kv_heads=GQA_KV_HEADS, group=GQA_GROUP,
                  dq=HEAD_DIM, dv=HEAD_DIM, tk=tk)
        att_a = flash_attention(qa, ka, p, VA_OFF, tq=tq_a, **fa)
        att_a = flash_attention(qa, ka, p, VA_OFF, tq=ctx_len, prev=att_a, **fa)

        qc, kc, vc = mla_prep(p, mla_kv_norm[l], w_kvb_p[l], gq_pad[l], gk_nope[l], gk_rope[l], cos_c, sin_c, tm_ew)
        fc = dict(batch=batch, seq=seq, ctx_len=ctx_len, kv_heads=MLA_HEADS, group=1, dq=MLA_QPAD, dv=MLA_V, tk=tk)
        att_c = flash_attention(qc, kc, vc, 0, tq=tq_c, **fc)
        att_c = flash_attention(qc, kc, vc, 0, tq=ctx_len, prev=att_c, **fc)

        hf = (band_row, w1p[l], hf_b1[l].reshape(1, -1), hf_w2[l], hf_b2[l].reshape(1, -1), w3r[l], hf_freq[l],
              delta_row)
        uc_lat = short_conv(p, hy_conv_w[l], hy_conv_b[l], 0, n_lat, seq, min(512, seq), 512)
        uc_ctx = short_conv(p, hy_conv_w[l], hy_conv_b[l], n_lat, n_ctx, ctx_len, min(256, ctx_len), 512)
        hy = jnp.concatenate([hyena_mixer(uc_lat, seq, hf, hy_skip[l]),
                              hyena_mixer(uc_ctx, ctx_len, hf, hy_skip[l])], axis=0)

        delta = merge_out(p, att_a, hy, att_c, wg_b, bg, wb_b, wo_b, l, tm, tj)
        xs, h2 = prenorm(xs, mod5, l, grp(tm_ew), tm_ew, delta=delta, gate_idx=2, gate_layer=l,
                         norm_g=norm2_g[l], shift_idx=3, scale_idx=4)
        delta = ffn(h2, wgu_b, wd_b, l, tm, th)

    (out,) = prenorm(xs, mod5, depth - 1, grp(tm_ew), tm_ew, delta=delta, gate_idx=5, gate_layer=depth - 1,
                     rows=n_lat)
    return out.reshape(batch, seq, d)
```

```python
import functools
import math

import jax
import jax.numpy as jnp
import numpy as np
from jax import lax
from jax.experimental import pallas as pl
from jax.experimental.pallas import tpu as pltpu

f32 = jnp.float32
bf16 = jnp.bfloat16

GRID_W = 64
ROPE_THETA = 10000.0
EPS = 1e-6
HEAD_DIM = 128
GQA_HEADS = 8
GQA_KV_HEADS = 2
GQA_GROUP = GQA_HEADS // GQA_KV_HEADS
HY_C = 1024
HY_ORDER = 2
HY_EMB = 33
HY_BANDS = (HY_EMB - 1) // 2
HY_FW = 64
HY_DECAY_TARGET = 1e-2
HY_DECAY_PCT_MIN = 0.3
HY_DECAY_PCT_MAX = 1.5
MLA_HEADS = 8
MLA_NOPE = 128
MLA_ROPE = 64
MLA_QK = MLA_NOPE + MLA_ROPE
MLA_V = 128
MLA_RANK = 512
N_BRANCH = 3
BRANCH_W = 1024
GATE_RANK = 256
ADA_RANK = 256
N_MOD = 6
IN_COLS = 6976

V7X_LANES = 128
V7X_SUBLANES = 8
V7X_VMEM_BYTES = 64 * 1024 * 1024
MIB = 1024 * 1024

MLA_QPAD = 2 * V7X_LANES
QC_OFF = 0
QA_OFF = QC_OFF + MLA_HEADS * MLA_QPAD
KA_OFF = QA_OFF + GQA_HEADS * HEAD_DIM
VA_OFF = KA_OFF + GQA_KV_HEADS * HEAD_DIM
HY_OFF = VA_OFF + GQA_KV_HEADS * HEAD_DIM
CKV_OFF = HY_OFF + 3 * HY_C
GATE_OFF = CKV_OFF + MLA_RANK
KR_OFF = GATE_OFF + GATE_RANK
NP_COLS = 7680

LOG2E = math.log2(math.e)


def _cparams(sem, vmem_mib):
    return pltpu.CompilerParams(dimension_semantics=sem, vmem_limit_bytes=int(vmem_mib * MIB))


def _row_tile(s, nctx, cands):
    for t in cands:
        if s % t == 0 and nctx % t == 0:
            return t
    raise ValueError("no row tile fits")


def _ada_kernel(cond_ref, down_ref, up_ref, b_ref, o_ref, t_ref):
    @pl.when(pl.program_id(1) == 0)
    def _():
        c = cond_ref[...]
        c = c * (1.0 / (1.0 + jnp.exp(-c)))
        t_ref[...] = jnp.dot(c, down_ref[...], precision=lax.Precision.HIGHEST, preferred_element_type=f32)

    o_ref[...] = jnp.dot(t_ref[...], up_ref[...], precision=lax.Precision.HIGHEST,
                         preferred_element_type=f32) + b_ref[...]


def ada_modulation(cond8, ada_down, ada_up, ada_b):
    depth, d, _ = ada_down.shape
    n = ada_up.shape[2]
    tn = 2048 if n % 2048 == 0 else 512
    return pl.pallas_call(
        _ada_kernel,
        grid=(depth, n // tn),
        in_specs=[
            pl.BlockSpec((8, d), lambda l, j: (0, 0)),
            pl.BlockSpec((None, d, ADA_RANK), lambda l, j: (l, 0, 0)),
            pl.BlockSpec((None, ADA_RANK, tn), lambda l, j: (l, 0, j)),
            pl.BlockSpec((None, 1, tn), lambda l, j: (l, 0, j)),
        ],
        out_specs=pl.BlockSpec((None, 8, tn), lambda l, j: (l, 0, j)),
        out_shape=jax.ShapeDtypeStruct((depth, 8, n), f32),
        scratch_shapes=[pltpu.VMEM((8, ADA_RANK), f32)],
        compiler_params=_cparams(("arbitrary", "arbitrary"), 40),
        name="ada_modulation",
    )(cond8, ada_down, ada_up, ada_b.reshape(depth, 1, n))


def _rope_table_kernel(pr_ref, pc_ref, ca_ref, sa_ref, cc_ref, sc_ref):
    shape = ca_ref.shape
    lane = lax.broadcasted_iota(jnp.int32, shape, 1)
    pr = jnp.broadcast_to(pr_ref[...], shape)
    pc = jnp.broadcast_to(pc_ref[...], shape)
    log_theta = math.log(ROPE_THETA)
    fa = jnp.exp((lane & 31).astype(f32) * (-log_theta / 32.0))
    ang = jnp.where(lane < 64, pr, pc) * fa
    ca_ref[...] = jnp.cos(ang)
    sa_ref[...] = jnp.where((lane & 63) < 32, -1.0, 1.0) * jnp.sin(ang)
    fc = jnp.exp((lane & 15).astype(f32) * (-log_theta / 16.0))
    angc = jnp.where(lane < 64, jnp.where(lane < 32, pr, pc) * fc, 0.0)
    cc_ref[...] = jnp.cos(angc)
    sc_ref[...] = jnp.where((lane & 31) < 16, -1.0, 1.0) * jnp.sin(angc)


def rope_tables(pos_row, pos_col, tm):
    t = pos_row.shape[0]
    spec1 = pl.BlockSpec((tm, 1), lambda i: (i, 0))
    spec = pl.BlockSpec((tm, V7X_LANES), lambda i: (i, 0))
    sh = jax.ShapeDtypeStruct((t, V7X_LANES), f32)
    return pl.pallas_call(
        _rope_table_kernel, grid=(t // tm,), in_specs=[spec1, spec1], out_specs=[spec] * 4,
        out_shape=[sh] * 4, compiler_params=_cparams(("arbitrary",), 32), name="rope_tables",
    )(pos_row, pos_col)


def _prenorm_kernel(*refs, has_delta, want_h):
    if has_delta:
        x_ref, d_ref, gate_ref = refs[:3]
        rest = refs[3:]
    else:
        x_ref = refs[0]
        rest = refs[1:]
    x = x_ref[...]
    if has_delta:
        x = x + gate_ref[...] * d_ref[...].astype(f32)
    if want_h:
        g_ref, shift_ref, scale_ref = rest[:3]
        outs = rest[3:]
    else:
        outs = rest
    k = 0
    if has_delta:
        outs[k][...] = x
        k += 1
    if want_h:
        y = x * lax.rsqrt(jnp.mean(x * x, axis=-1, keepdims=True) + EPS)
        y = y * g_ref[...]
        outs[k][...] = (y * (1.0 + scale_ref[...]) + shift_ref[...]).astype(bf16)


def prenorm(x, mod5, layer, grp_of_block, tm, *, delta=None, gate_idx=None, gate_layer=None,
            norm_g=None, shift_idx=None, scale_idx=None, rows=None):
    t, d = x.shape
    rows = t if rows is None else rows
    has_delta = delta is not None
    want_h = norm_g is not None
    row_spec = pl.BlockSpec((tm, d), lambda i: (i, 0))

    def mod_spec(lyr, which):
        return pl.BlockSpec((None, None, None, 1, d), lambda i: (lyr, grp_of_block(i), which, 0, 0))

    in_specs, args = [row_spec], [x]
    if has_delta:
        in_specs += [row_spec, mod_spec(gate_layer, gate_idx)]
        args += [delta, mod5]
    if want_h:
        in_specs += [pl.BlockSpec((1, d), lambda i: (0, 0)), mod_spec(layer, shift_idx), mod_spec(layer, scale_idx)]
        args += [norm_g.reshape(1, d), mod5, mod5]
    out_specs, out_shape = [], []
    if has_delta:
        out_specs.append(row_spec)
        out_shape.append(jax.ShapeDtypeStruct((rows, d), f32))
    if want_h:
        out_specs.append(row_spec)
        out_shape.append(jax.ShapeDtypeStruct((rows, d), bf16))
    return pl.pallas_call(
        functools.partial(_prenorm_kernel, has_delta=has_delta, want_h=want_h),
        grid=(rows // tm,), in_specs=in_specs, out_specs=out_specs, out_shape=out_shape,
        compiler_params=_cparams(("arbitrary",), 48), name="prenorm",
    )(*args)


def _matmul_kernel(a_ref, w_ref, o_ref):
    o_ref[...] = jnp.dot(a_ref[...], w_ref[...], preferred_element_type=f32).astype(o_ref.dtype)


def matmul(a, w, layer, tm, tn):
    t, k = a.shape
    n = w.shape[2]
    return pl.pallas_call(
        _matmul_kernel, grid=(t // tm, n // tn),
        in_specs=[pl.BlockSpec((tm, k), lambda i, j: (i, 0)),
                  pl.BlockSpec((None, k, tn), lambda i, j: (layer, 0, j))],
        out_specs=pl.BlockSpec((tm, tn), lambda i, j: (i, j)),
        out_shape=jax.ShapeDtypeStruct((t, n), bf16),
        compiler_params=_cparams(("arbitrary", "arbitrary"), 48), name="in_proj",
    )(a, w)


def _rope128(x, cos, sin_signed, half):
    lane = lax.broadcasted_iota(jnp.int32, x.shape, 1)
    first = (lane & (2 * half - 1)) < half
    swapped = jnp.where(first, pltpu.roll(x, V7X_LANES - half, 1), pltpu.roll(x, half, 1))
    return x * cos + swapped * sin_signed


def _gqa_prep_kernel(q_ref, kv_ref, gq_ref, gk_ref, cos_ref, sin_ref, qo_ref, ko_ref):
    cos, sin = cos_ref[...], sin_ref[...]
    qscale = HEAD_DIM ** -0.5 * LOG2E
    for h in range(GQA_HEADS):
        sl = slice(h * HEAD_DIM, (h + 1) * HEAD_DIM)
        q = q_ref[:, sl].astype(f32)
        q = q * lax.rsqrt(jnp.mean(q * q, axis=-1, keepdims=True) + EPS) * gq_ref[...]
        qo_ref[:, sl] = (_rope128(q, cos, sin, 32) * qscale).astype(bf16)
    for h in range(GQA_KV_HEADS):
        sl = slice(h * HEAD_DIM, (h + 1) * HEAD_DIM)
        k = kv_ref[:, sl].astype(f32)
        k = k * lax.rsqrt(jnp.mean(k * k, axis=-1, keepdims=True) + EPS) * gk_ref[...]
        ko_ref[:, sl] = _rope128(k, cos, sin, 32).astype(bf16)


def gqa_prep(p, gq, gk, cos_a, sin_a, tm):
    t = p.shape[0]
    qw, kw = GQA_HEADS * HEAD_DIM, GQA_KV_HEADS * HEAD_DIM
    tab = pl.BlockSpec((tm, V7X_LANES), lambda i: (i, 0))
    vec = pl.BlockSpec((1, HEAD_DIM), lambda i: (0, 0))
    return pl.pallas_call(
        _gqa_prep_kernel, grid=(t // tm,),
        in_specs=[pl.BlockSpec((tm, qw), lambda i: (i, QA_OFF // qw)),
                  pl.BlockSpec((tm, 2 * kw), lambda i: (i, KA_OFF // (2 * kw))), vec, vec, tab, tab],
        out_specs=[pl.BlockSpec((tm, qw), lambda i: (i, 0)), pl.BlockSpec((tm, kw), lambda i: (i, 0))],
        out_shape=[jax.ShapeDtypeStruct((t, qw), bf16), jax.ShapeDtypeStruct((t, kw), bf16)],
        compiler_params=_cparams(("arbitrary",), 32), name="gqa_prep",
    )(p, p, gq.reshape(1, -1), gk.reshape(1, -1), cos_a, sin_a)


def _mla_prep_kernel(q_ref, ckv_ref, kr_ref, kvg_ref, wkvb_ref, gq_ref, gkn_ref, gkr_ref, cos_ref, sin_ref,
                     qo_ref, ko_ref, vo_ref):
    cos, sin = cos_ref[...], sin_ref[...]
    qscale = MLA_QK ** -0.5 * LOG2E
    inv_qk = 1.0 / MLA_QK
    for h in range(MLA_HEADS):
        lo = h * MLA_QPAD
        qn = q_ref[:, lo:lo + V7X_LANES].astype(f32)
        qr = q_ref[:, lo + V7X_LANES:lo + MLA_QPAD].astype(f32)
        ss = jnp.sum(qn * qn, axis=-1, keepdims=True) + jnp.sum(qr * qr, axis=-1, keepdims=True)
        r = lax.rsqrt(ss * inv_qk + EPS) * qscale
        qo_ref[:, lo:lo + V7X_LANES] = (qn * r * gq_ref[:, :V7X_LANES]).astype(bf16)
        qo_ref[:, lo + V7X_LANES:lo + MLA_QPAD] = (_rope128(qr * gq_ref[:, V7X_LANES:], cos, sin, 16) * r).astype(bf16)
    c = ckv_ref[...].astype(f32)
    cn = c * lax.rsqrt(jnp.mean(c * c, axis=-1, keepdims=True) + EPS) * kvg_ref[...]
    kv = jnp.dot(cn.astype(bf16), wkvb_ref[...], preferred_element_type=f32)
    kr = kr_ref[...].astype(f32)
    ss_r = jnp.sum(kr * kr, axis=-1, keepdims=True)
    kr_rot = _rope128(kr * gkr_ref[...], cos, sin, 16)
    nv = MLA_HEADS * MLA_NOPE
    for h in range(MLA_HEADS):
        kn = kv[:, h * MLA_NOPE:(h + 1) * MLA_NOPE]
        r = lax.rsqrt((jnp.sum(kn * kn, axis=-1, keepdims=True) + ss_r) * inv_qk + EPS)
        lo = h * MLA_QPAD
        ko_ref[:, lo:lo + V7X_LANES] = (kn * r * gkn_ref[...]).astype(bf16)
        ko_ref[:, lo + V7X_LANES:lo + MLA_QPAD] = (kr_rot * r).astype(bf16)
    vo_ref[...] = kv[:, nv:].astype(bf16)


def mla_prep(p, kv_g, wkvb_p, gq_pad, gk_nope, gk_rope, cos_c, sin_c, tm):
    t = p.shape[0]
    qw = MLA_HEADS * MLA_QPAD
    vw = MLA_HEADS * MLA_V
    tab = pl.BlockSpec((tm, V7X_LANES), lambda i: (i, 0))

    def vec(n):
        return pl.BlockSpec((1, n), lambda i: (0, 0))

    return pl.pallas_call(
        _mla_prep_kernel, grid=(t // tm,),
        in_specs=[pl.BlockSpec((tm, qw), lambda i: (i, QC_OFF // qw)),
                  pl.BlockSpec((tm, MLA_RANK), lambda i: (i, CKV_OFF // MLA_RANK)),
                  pl.BlockSpec((tm, V7X_LANES), lambda i: (i, KR_OFF // V7X_LANES)),
                  vec(MLA_RANK), pl.BlockSpec((MLA_RANK, 2 * vw), lambda i: (0, 0)),
                  vec(MLA_QPAD), vec(V7X_LANES), vec(V7X_LANES), tab, tab],
        out_specs=[pl.BlockSpec((tm, qw), lambda i: (i, 0)), pl.BlockSpec((tm, qw), lambda i: (i, 0)),
                   pl.BlockSpec((tm, vw), lambda i: (i, 0))],
        out_shape=[jax.ShapeDtypeStruct((t, qw), bf16), jax.ShapeDtypeStruct((t, qw), bf16),
                   jax.ShapeDtypeStruct((t, vw), bf16)],
        compiler_params=_cparams(("arbitrary",), 48), name="mla_prep",
    )(p, p, p, kv_g.reshape(1, -1), wkvb_p, gq_pad.reshape(1, -1), gk_nope.reshape(1, -1),
      gk_rope.reshape(1, -1), cos_c, sin_c)


FLASH_COLS = 256


def _scores(k, qt_ref, s_ref):
    s_ref[0:k.shape[0], :] = jnp.dot(k, qt_ref[...], preferred_element_type=f32)


def _softmax_pv(s_ref, tk, v, m_ref, l_ref, acc_ref):
    vt = v.T
    step = min(FLASH_COLS, s_ref.shape[1])
    for c0 in range(0, s_ref.shape[1], step):
        cols = slice(c0, c0 + step)
        s = s_ref[0:tk, cols]
        m_prev = m_ref[:, cols]
        m_new = jnp.maximum(m_prev, jnp.max(s, axis=0, keepdims=True))
        alpha = jnp.exp2(m_prev - m_new)
        pr = jnp.exp2(s - m_new)
        l_ref[:, cols] = alpha * l_ref[:, cols] + jnp.sum(pr, axis=0, keepdims=True)
        acc_ref[:, cols] = alpha * acc_ref[:, cols] + jnp.dot(vt, pr.astype(bf16), preferred_element_type=f32)
        m_ref[:, cols] = m_new


def _flash_kernel(*refs, group, dq, dv, tq, tk, n_lat):
    if n_lat:
        q_ref, kl_ref, vl_ref, kc_ref, vc_ref, o_ref, qt_ref, m_ref, l_ref, acc_ref, s0_ref, s1_ref = refs
    else:
        q_ref, kc_ref, vc_ref, o_ref, qt_ref, m_ref, l_ref, acc_ref, s0_ref, s1_ref = refs
    for g in range(group):
        qt_ref[:, g * tq:(g + 1) * tq] = q_ref[:, g * dq:(g + 1) * dq].T
    m_ref[...] = jnp.full(m_ref.shape, -1e30, f32)
    l_ref[...] = jnp.zeros(l_ref.shape, f32)
    acc_ref[...] = jnp.zeros(acc_ref.shape, f32)
    n_ctx = kc_ref.shape[0]
    _scores(kc_ref[...], qt_ref, s1_ref)
    if n_lat:
        def kchunk(j):
            return kl_ref[pl.ds(pl.multiple_of(j * tk, tk), tk), :]

        def vchunk(j):
            return vl_ref[pl.ds(pl.multiple_of(j * tk, tk), tk), :]

        assert n_lat % 2 == 0
        _scores(kchunk(0), qt_ref, s0_ref)
        _softmax_pv(s1_ref, n_ctx, vc_ref[...], m_ref, l_ref, acc_ref)

        def body(i, carry):
            _scores(kchunk(2 * i + 1), qt_ref, s1_ref)
            _softmax_pv(s0_ref, tk, vchunk(2 * i), m_ref, l_ref, acc_ref)
            _scores(kchunk(2 * i + 2), qt_ref, s0_ref)
            _softmax_pv(s1_ref, tk, vchunk(2 * i + 1), m_ref, l_ref, acc_ref)
            return carry
        lax.fori_loop(0, n_lat // 2 - 1, body, 0)
        _scores(kchunk(n_lat - 1), qt_ref, s1_ref)
        _softmax_pv(s0_ref, tk, vchunk(n_lat - 2), m_ref, l_ref, acc_ref)
        _softmax_pv(s1_ref, tk, vchunk(n_lat - 1), m_ref, l_ref, acc_ref)
    else:
        _softmax_pv(s1_ref, n_ctx, vc_ref[...], m_ref, l_ref, acc_ref)
    for g in range(group):
        cols = slice(g * tq, (g + 1) * tq)
        o_ref[:, g * dv:(g + 1) * dv] = (acc_ref[:, cols] / l_ref[:, cols]).T.astype(o_ref.dtype)


def flash_attention(q, k, v, v_col_off, *, batch, seq, ctx_len, kv_heads, group, dq, dv, tq, tk, lat):
    ctx_blk0 = batch * seq // ctx_len
    voff = v_col_off // dv
    rows = group * tq
    kc_spec = pl.BlockSpec((ctx_len, dq), lambda b, h, i: (ctx_blk0 + b, h))
    vc_spec = pl.BlockSpec((ctx_len, dv), lambda b, h, i: (ctx_blk0 + b, voff + h))
    if lat:
        nq = seq // tq
        in_specs = [pl.BlockSpec((tq, group * dq), lambda b, h, i: (b * nq + i, h)),
                    pl.BlockSpec((seq, dq), lambda b, h, i: (b, h)),
                    pl.BlockSpec((seq, dv), lambda b, h, i: (b, voff + h)), kc_spec, vc_spec]
        args = [q, k, v, k, v]
        out_spec = pl.BlockSpec((tq, group * dv), lambda b, h, i: (b * nq + i, h))
        out_rows = batch * seq
        n_lat = seq // tk
    else:
        nq = 1
        assert tq == ctx_len
        in_specs = [pl.BlockSpec((tq, group * dq), lambda b, h, i: (ctx_blk0 + b, h)), kc_spec, vc_spec]
        args = [q, k, v]
        out_spec = pl.BlockSpec((tq, group * dv), lambda b, h, i: (b, h))
        out_rows = batch * ctx_len
        n_lat = 0
    return pl.pallas_call(
        functools.partial(_flash_kernel, group=group, dq=dq, dv=dv, tq=tq, tk=tk, n_lat=n_lat),
        grid=(batch, kv_heads, nq), in_specs=in_specs, out_specs=out_spec,
        out_shape=jax.ShapeDtypeStruct((out_rows, kv_heads * group * dv), bf16),
        scratch_shapes=[pltpu.VMEM((dq, rows), bf16), pltpu.VMEM((1, rows), f32), pltpu.VMEM((1, rows), f32),
                        pltpu.VMEM((dv, rows), f32)] + [pltpu.VMEM((max(tk, ctx_len) if lat else ctx_len, rows), f32)] * 2,
        compiler_params=_cparams(("arbitrary", "arbitrary", "arbitrary"), 48),
        name="flash_lat" if lat else "flash_ctx",
    )(*args)


CONV_HALO = 16


def _conv3_kernel(prev_ref, cur_ref, next_ref, w_ref, b_ref, o_ref, *, blocks_per_seq):
    rows = cur_ref.shape[0]
    pos = pl.program_id(0) % blocks_per_seq
    x = cur_ref[...].astype(f32)
    row = lax.broadcasted_iota(jnp.int32, x.shape, 0)
    prev_row = jnp.where(pos == 0, 0.0, prev_ref[...].astype(f32)[CONV_HALO - 1:CONV_HALO, :])
    next_row = jnp.where(pos == blocks_per_seq - 1, 0.0, next_ref[...].astype(f32)[0:1, :])
    xm = jnp.where(row == 0, prev_row, pltpu.roll(x, 1, 0))
    xp = jnp.where(row == rows - 1, next_row, pltpu.roll(x, rows - 1, 0))
    o_ref[...] = xm * w_ref[0:1, :] + x * w_ref[1:2, :] + xp * w_ref[2:3, :] + b_ref[...]


def short_conv(p, conv_w, conv_b, row0, nrows, seq_len, rows, cw):
    width = 3 * HY_C
    rb0 = row0 // rows
    sub = rows // CONV_HALO
    last_halo = p.shape[0] // CONV_HALO - 1
    c0 = HY_OFF // cw
    return pl.pallas_call(
        functools.partial(_conv3_kernel, blocks_per_seq=seq_len // rows),
        grid=(nrows // rows, width // cw),
        in_specs=[
            pl.BlockSpec((CONV_HALO, cw), lambda r, c: (jnp.maximum((rb0 + r) * sub - 1, 0), c0 + c)),
            pl.BlockSpec((rows, cw), lambda r, c: (rb0 + r, c0 + c)),
            pl.BlockSpec((CONV_HALO, cw), lambda r, c: (jnp.minimum((rb0 + r + 1) * sub, last_halo), c0 + c)),
            pl.BlockSpec((3, cw), lambda r, c: (0, c)),
            pl.BlockSpec((1, cw), lambda r, c: (0, c)),
        ],
        out_specs=pl.BlockSpec((rows, cw), lambda r, c: (r, c)),
        out_shape=jax.ShapeDtypeStruct((nrows, width), f32),
        compiler_params=_cparams(("arbitrary", "arbitrary"), 32), name="short_conv",
    )(p, p, p, conv_w, conv_b.reshape(1, width))


@functools.lru_cache(maxsize=None)
def _dft_tables(length):
    nb = 128 if length >= 1024 else 16
    n_fft = 2 * length
    na = n_fft // nb
    nah = na // 2
    lo = np.arange(nb, dtype=np.int64)[:, None, None]
    k1 = np.arange(na, dtype=np.int64)[None, :, None]
    hi = np.arange(nah, dtype=np.int64)[None, None, :]
    ang = 2.0 * np.pi * (((nb * hi + lo) * k1) % n_fft) / n_fft
    c, s = np.cos(ang), np.sin(ang)
    a1 = np.concatenate([np.concatenate([c, s], axis=2), np.concatenate([-s, c], axis=2)], axis=1)
    ct, st = np.swapaxes(c, 1, 2) / n_fft, np.swapaxes(s, 1, 2) / n_fft
    a3 = np.concatenate([np.concatenate([ct, -st], axis=2), np.concatenate([st, ct], axis=2)], axis=1)
    kk = np.arange(nb, dtype=np.int64)
    angb = 2.0 * np.pi * ((kk[:, None] * kk[None, :]) % nb) / nb
    cb, sb = np.cos(angb), np.sin(angb)
    mf = np.block([[cb, sb], [-sb, cb]])
    mfc = np.block([[cb, -sb], [sb, cb]])
    return dict(nb=nb, na=na, nah=nah,
                a1=jnp.asarray(a1, dtype=bf16), a1r=jnp.asarray(a1[:, :, :nah], dtype=bf16),
                a3=jnp.asarray(a3, dtype=bf16), mf=jnp.asarray(mf, dtype=bf16), mfc=jnp.asarray(mfc, dtype=bf16),
                m2f=jnp.asarray(np.concatenate([mf, mfc], axis=1), dtype=bf16))


def _tap_times(j, nb, nah):
    ridx = lax.broadcasted_iota(jnp.int32, (V7X_SUBLANES * nah, 1), 0)
    return nb * (ridx & (nah - 1)) + (j * V7X_SUBLANES + (ridx >> (nah.bit_length() - 1)))


def _filter_mlp_kernel(band_ref, w1_ref, b1_ref, w2_ref, b2_ref, freq_ref, o_ref, *, length, nb, nah):
    hp = lax.Precision.HIGHEST
    t = _tap_times(pl.program_id(0), nb, nah).astype(f32)
    t_unit = t / float(max(length - 1, 1))
    lane = lax.broadcasted_iota(jnp.int32, (t.shape[0], V7X_LANES), 1)
    ang = ((2.0 * math.pi / length) * t) * band_ref[...]
    feats = jnp.where(lane == 0, t_unit,
                      jnp.where(lane <= HY_BANDS, jnp.cos(ang), jnp.where(lane <= 2 * HY_BANDS, -jnp.sin(ang), 0.0)))
    h = jnp.sin(freq_ref[0:1, :] * (jnp.dot(feats, w1_ref[...], precision=hp, preferred_element_type=f32) + b1_ref[...]))
    o_ref[...] = jnp.sin(freq_ref[1:2, :] * (jnp.dot(h, w2_ref[...], precision=hp, preferred_element_type=f32)
                                            + b2_ref[...]))


def filter_mlp(tabs, length, band_row, w1p, b1, w2, b2, freq):
    nb, nah = tabs["nb"], tabs["nah"]
    rows = V7X_SUBLANES * nah
    full = lambda *shape: pl.BlockSpec(shape, lambda j: (0,) * len(shape))
    return pl.pallas_call(
        functools.partial(_filter_mlp_kernel, length=length, nb=nb, nah=nah), grid=(nb // V7X_SUBLANES,),
        in_specs=[full(1, V7X_LANES), full(V7X_LANES, HY_FW), full(1, HY_FW), full(HY_FW, HY_FW), full(1, HY_FW),
                  full(2, HY_FW)],
        out_specs=pl.BlockSpec((rows, HY_FW), lambda j: (j, 0)),
        out_shape=jax.ShapeDtypeStruct((length, HY_FW), f32),
        compiler_params=_cparams(("arbitrary",), 32), name="hy_filter_mlp",
    )(band_row, w1p, b1, w2, b2, freq)


def _filter_stage1_kernel(a1r_ref, h_ref, w3_ref, delta_ref, o_ref, ss_ref, *, length, nb, nah):
    j = pl.program_id(1)
    hp = lax.Precision.HIGHEST
    t_int = _tap_times(j, nb, nah)
    t_unit = t_int.astype(f32) / float(max(length - 1, 1))
    decay = jnp.exp(-t_unit * delta_ref[...])
    h = h_ref[...]

    @pl.when(j == 0)
    def _():
        ss_ref[...] = jnp.zeros(ss_ref.shape, f32)

    na = 2 * nah
    for o in range(HY_ORDER):
        fwd = jnp.dot(h, w3_ref[2 * o], precision=hp, preferred_element_type=f32) * decay
        bwd = jnp.dot(h, w3_ref[2 * o + 1], precision=hp, preferred_element_type=f32) * decay
        bwd = jnp.where(t_int == 0, 0.0, bwd)
        ss_ref[o:o + 1, :] += jnp.sum(fwd * fwd + bwd * bwd, axis=0, keepdims=True)
        fb, bb = fwd.astype(bf16), bwd.astype(bf16)
        for l in range(V7X_SUBLANES):
            a = a1r_ref[l]
            ff = jnp.dot(a, fb[l * nah:(l + 1) * nah], preferred_element_type=f32)
            gg = jnp.dot(a, bb[l * nah:(l + 1) * nah], preferred_element_type=f32)
            o_ref[o, 0, l] = ff[:na]
            o_ref[o, 1, l] = ff[na:]
            o_ref[o, 2, l] = gg[:na]
            o_ref[o, 3, l] = -gg[na:]


def filter_stage1(tabs, length, hmlp, w3r, delta_row):
    nb, na, nah = tabs["nb"], tabs["na"], tabs["nah"]
    ncb = HY_C // V7X_LANES
    return pl.pallas_call(
        functools.partial(_filter_stage1_kernel, length=length, nb=nb, nah=nah),
        grid=(ncb, nb // V7X_SUBLANES),
        in_specs=[pl.BlockSpec((V7X_SUBLANES, 2 * na, nah), lambda c, j: (j, 0, 0)),
                  pl.BlockSpec((V7X_SUBLANES * nah, HY_FW), lambda c, j: (j, 0)),
                  pl.BlockSpec((2 * HY_ORDER, HY_FW, V7X_LANES), lambda c, j: (0, 0, c)),
                  pl.BlockSpec((1, V7X_LANES), lambda c, j: (0, c))],
        out_specs=[pl.BlockSpec((HY_ORDER, 4, V7X_SUBLANES, na, V7X_LANES), lambda c, j: (0, 0, j, 0, c)),
                   pl.BlockSpec((V7X_SUBLANES, V7X_LANES), lambda c, j: (0, c))],
        out_shape=[jax.ShapeDtypeStruct((HY_ORDER, 4, nb, na, HY_C), f32),
                   jax.ShapeDtypeStruct((V7X_SUBLANES, HY_C), f32)],
        compiler_params=_cparams(("arbitrary", "arbitrary"), 48), name="hy_filter_s1",
    )(tabs["a1r"], hmlp, w3r, delta_row)


def _tile_rows(ref, l):
    tiles = math.prod(ref.shape[:-2])
    flat = ref.reshape(tiles * V7X_SUBLANES, ref.shape[-1])
    return flat.at[pl.ds(l, tiles, stride=V7X_SUBLANES), :]


def _filter_stage2_kernel(x_ref, m_ref, o_ref):
    for kl in range(V7X_SUBLANES):
        rhs = _tile_rows(x_ref, kl)[...].astype(bf16)
        o_ref[kl] = jnp.dot(m_ref[...], rhs, preferred_element_type=f32)


def filter_stage2(tabs, hin):
    nb, na = tabs["nb"], tabs["na"]
    ncb = HY_C // V7X_LANES
    return pl.pallas_call(
        _filter_stage2_kernel, grid=(HY_ORDER, ncb, na // V7X_SUBLANES),
        in_specs=[pl.BlockSpec((None, 4, nb, V7X_SUBLANES, V7X_LANES), lambda o, c, j: (o, 0, 0, j, c)),
                  pl.BlockSpec((2 * nb, 4 * nb), lambda o, c, j: (0, 0))],
        out_specs=pl.BlockSpec((None, V7X_SUBLANES, 2 * nb, V7X_LANES), lambda o, c, j: (o, j, 0, c)),
        out_shape=jax.ShapeDtypeStruct((HY_ORDER, na, 2 * nb, HY_C), f32),
        compiler_params=_cparams(("arbitrary", "arbitrary", "arbitrary"), 48), name="hy_filter_s2",
    )(hin, tabs["m2f"])


def _conv_stage1_kernel(zr_ref, zi_ref, a1_ref, o_ref, *, na):
    for l in range(V7X_SUBLANES):
        rhs = jnp.concatenate([_tile_rows(zr_ref, l)[...], _tile_rows(zi_ref, l)[...]], axis=0).astype(bf16)
        y = jnp.dot(a1_ref[l], rhs, preferred_element_type=f32)
        o_ref[0, l] = y[:na]
        o_ref[1, l] = y[na:]


def conv_stage1(tabs, z4, col_off):
    nb, na, nah = tabs["nb"], tabs["na"], tabs["nah"]
    ncb = HY_C // V7X_LANES
    c0 = col_off // V7X_LANES
    return pl.pallas_call(
        functools.partial(_conv_stage1_kernel, na=na), grid=(ncb, nb // V7X_SUBLANES),
        in_specs=[pl.BlockSpec((None, nah, V7X_SUBLANES, V7X_LANES), lambda c, j: (0, 0, j, c0 + c)),
                  pl.BlockSpec((None, nah, V7X_SUBLANES, V7X_LANES), lambda c, j: (1, 0, j, c0 + c)),
                  pl.BlockSpec((V7X_SUBLANES, 2 * na, na), lambda c, j: (j, 0, 0))],
        out_specs=pl.BlockSpec((2, V7X_SUBLANES, na, V7X_LANES), lambda c, j: (0, j, 0, c)),
        out_shape=jax.ShapeDtypeStruct((2, nb, na, HY_C), f32),
        compiler_params=_cparams(("arbitrary", "arbitrary"), 48), name="hy_conv_s1",
    )(z4, z4, tabs["a1"])


def _conv_stage2_kernel(x_ref, h_ref, mf_ref, mfc_ref, o_ref, *, nb):
    for kl in range(V7X_SUBLANES):
        rhs = _tile_rows(x_ref, kl)[...].astype(bf16)
        x = jnp.dot(mf_ref[...], rhs, preferred_element_type=f32)
        xr, xi = x[:nb], x[nb:]
        hr, hi = h_ref[kl, :nb, :], h_ref[kl, nb:, :]
        y = jnp.concatenate([xr * hr - xi * hi, xr * hi + xi * hr], axis=0).astype(bf16)
        e = jnp.dot(mfc_ref[...], y, preferred_element_type=f32)
        o_ref[0, kl] = e[:nb]
        o_ref[1, kl] = e[nb:]


def conv_stage2(tabs, din, hspec, order):
    nb, na = tabs["nb"], tabs["na"]
    ncb = HY_C // V7X_LANES
    return pl.pallas_call(
        functools.partial(_conv_stage2_kernel, nb=nb), grid=(ncb, na // V7X_SUBLANES),
        in_specs=[pl.BlockSpec((2, nb, V7X_SUBLANES, V7X_LANES), lambda c, j: (0, 0, j, c)),
                  pl.BlockSpec((None, V7X_SUBLANES, 2 * nb, V7X_LANES), lambda c, j: (order, j, 0, c)),
                  pl.BlockSpec((2 * nb, 2 * nb), lambda c, j: (0, 0)),
                  pl.BlockSpec((2 * nb, 2 * nb), lambda c, j: (0, 0))],
        out_specs=pl.BlockSpec((2, V7X_SUBLANES, nb, V7X_LANES), lambda c, j: (0, j, 0, c)),
        out_shape=jax.ShapeDtypeStruct((2, na, nb, HY_C), f32),
        compiler_params=_cparams(("arbitrary", "arbitrary"), 48), name="hy_conv_s2",
    )(din, hspec, tabs["mf"], tabs["mfc"])


def _conv_stage3_kernel(e_ref, a3_ref, zr_ref, zi_ref, gr_ref, gi_ref, ss_ref, skip_ref, o_ref, *, order):
    rs = lax.rsqrt(ss_ref[order:order + 1, :] + EPS)
    skip = skip_ref[order:order + 1, :]
    for l in range(V7X_SUBLANES):
        rhs = _tile_rows(e_ref, l)[...].astype(bf16)
        y = jnp.dot(a3_ref[l], rhs, preferred_element_type=f32) * rs
        z = jnp.concatenate([_tile_rows(zr_ref, l)[...], _tile_rows(zi_ref, l)[...]], axis=0)
        g = jnp.concatenate([_tile_rows(gr_ref, l)[...], _tile_rows(gi_ref, l)[...]], axis=0)
        _tile_rows(o_ref, l)[...] = g * (y + skip * z)


def conv_stage3(tabs, ein, z4, z_off, g4, g_off, ss, skip, order):
    nb, na, nah = tabs["nb"], tabs["na"], tabs["nah"]
    ncb = HY_C // V7X_LANES
    zc, gc = z_off // V7X_LANES, g_off // V7X_LANES

    def slab(b, c0):
        return pl.BlockSpec((None, nah, V7X_SUBLANES, V7X_LANES), lambda c, j: (b, 0, j, c0 + c))

    return pl.pallas_call(
        functools.partial(_conv_stage3_kernel, order=order), grid=(ncb, nb // V7X_SUBLANES),
        in_specs=[pl.BlockSpec((2, na, V7X_SUBLANES, V7X_LANES), lambda c, j: (0, 0, j, c)),
                  pl.BlockSpec((V7X_SUBLANES, na, 2 * na), lambda c, j: (j, 0, 0)),
                  slab(0, zc), slab(1, zc), slab(0, gc), slab(1, gc),
                  pl.BlockSpec((V7X_SUBLANES, V7X_LANES), lambda c, j: (0, c)),
                  pl.BlockSpec((HY_ORDER, V7X_LANES), lambda c, j: (0, c))],
        out_specs=pl.BlockSpec((2, nah, V7X_SUBLANES, V7X_LANES), lambda c, j: (0, 0, j, c)),
        out_shape=jax.ShapeDtypeStruct((2, nah, nb, HY_C), f32),
        compiler_params=_cparams(("arbitrary", "arbitrary"), 48), name="hy_conv_s3",
    )(ein, tabs["a3"], z4, z4, g4, g4, ss, skip)


def hyena_mixer(uc, length, hf, skip):
    tabs = _dft_tables(length)
    nb, nah = tabs["nb"], tabs["nah"]
    band_row, w1p, b1, w2, b2, w3r, freq, delta_row = hf
    hmlp = filter_mlp(tabs, length, band_row, w1p, b1, w2, b2, freq)
    hin, ss = filter_stage1(tabs, length, hmlp, w3r, delta_row)
    hspec = filter_stage2(tabs, hin)
    u4 = uc.reshape(2, nah, nb, 3 * HY_C)
    z4, z_off = u4, 0
    for o in range(HY_ORDER):
        din = conv_stage1(tabs, z4, z_off)
        ein = conv_stage2(tabs, din, hspec, o)
        z4 = conv_stage3(tabs, ein, z4, z_off, u4, (o + 1) * HY_C, ss, skip, o)
        z_off = 0
    return z4.reshape(2 * length, HY_C)


def _merge_kernel(gl_ref, al_ref, ac_ref, hl_ref, hc_ref, cl_ref, cc_ref, wg_ref, bg_ref, wb_ref, wo_ref,
                  o_ref, acc_ref, br_ref, *, lat_blocks):
    i, j = pl.program_id(0), pl.program_id(1)

    @pl.when(j == 0)
    def _():
        acc_ref[...] = jnp.zeros(acc_ref.shape, f32)

    @pl.when((j == 0) & (i < lat_blocks))
    def _():
        br_ref[0] = al_ref[...]
        br_ref[1] = hl_ref[...].astype(bf16)
        br_ref[2] = cl_ref[...]

    @pl.when((j == 0) & (i >= lat_blocks))
    def _():
        br_ref[0] = ac_ref[...]
        br_ref[1] = hc_ref[...].astype(bf16)
        br_ref[2] = cc_ref[...]

    gl = gl_ref[...]
    y = None
    for n in range(N_BRANCH):
        z = jnp.dot(gl, wg_ref[n], preferred_element_type=f32) + bg_ref[n]
        g = 1.0 / (1.0 + jnp.exp(-z))
        term = g * jnp.dot(br_ref[n], wb_ref[n], preferred_element_type=f32)
        y = term if y is None else y + term
    acc_ref[...] += jnp.dot(y.astype(bf16), wo_ref[...], preferred_element_type=f32)

    @pl.when(j == pl.num_programs(1) - 1)
    def _():
        o_ref[...] = acc_ref[...].astype(o_ref.dtype)


def merge_out(p, a, hy, c, wg, bg, wb, wo, layer, tm, tj):
    t = p.shape[0]
    d = wo.shape[2]
    nl = a[0].shape[0] // tm
    assert a[1].shape[0] % tm == 0 and t == a[0].shape[0] + a[1].shape[0]
    lat = pl.BlockSpec((tm, BRANCH_W), lambda i, j: (jnp.minimum(i, nl - 1), 0))
    ctx = pl.BlockSpec((tm, BRANCH_W), lambda i, j: (jnp.maximum(i - nl, 0), 0))
    return pl.pallas_call(
        functools.partial(_merge_kernel, lat_blocks=nl), grid=(t // tm, d // tj),
        in_specs=[pl.BlockSpec((tm, GATE_RANK), lambda i, j: (i, GATE_OFF // GATE_RANK)),
                  lat, ctx, lat, ctx, lat, ctx,
                  pl.BlockSpec((None, N_BRANCH, GATE_RANK, tj), lambda i, j: (layer, 0, 0, j)),
                  pl.BlockSpec((None, N_BRANCH, 1, tj), lambda i, j: (layer, 0, 0, j)),
                  pl.BlockSpec((None, N_BRANCH, BRANCH_W, tj), lambda i, j: (layer, 0, 0, j)),
                  pl.BlockSpec((None, tj, d), lambda i, j: (layer, j, 0))],
        out_specs=pl.BlockSpec((tm, d), lambda i, j: (i, 0)),
        out_shape=jax.ShapeDtypeStruct((t, d), bf16),
        scratch_shapes=[pltpu.VMEM((tm, d), f32), pltpu.VMEM((N_BRANCH, tm, BRANCH_W), bf16)],
        compiler_params=_cparams(("arbitrary", "arbitrary"), 58), name="merge_out",
    )(p, a[0], a[1], hy[0], hy[1], c[0], c[1], wg, bg, wb, wo)


def _ffn_kernel(h_ref, wg_ref, wu_ref, wd_ref, o_ref, acc_ref):
    j = pl.program_id(1)

    @pl.when(j == 0)
    def _():
        acc_ref[...] = jnp.zeros(acc_ref.shape, f32)

    h = h_ref[...]
    g = jnp.dot(h, wg_ref[...], preferred_element_type=f32)
    u = jnp.dot(h, wu_ref[...], preferred_element_type=f32)
    a = (g * (1.0 / (1.0 + jnp.exp(-g))) * u).astype(bf16)
    acc_ref[...] += jnp.dot(a, wd_ref[...], preferred_element_type=f32)

    @pl.when(j == pl.num_programs(1) - 1)
    def _():
        o_ref[...] = acc_ref[...].astype(o_ref.dtype)


def ffn(h, w_gu, w_down, layer, tm, th):
    t, d = h.shape
    hidden = w_down.shape[1]
    nh = hidden // th
    return pl.pallas_call(
        _ffn_kernel, grid=(t // tm, nh),
        in_specs=[pl.BlockSpec((tm, d), lambda i, j: (i, 0)),
                  pl.BlockSpec((None, d, th), lambda i, j: (layer, 0, j)),
                  pl.BlockSpec((None, d, th), lambda i, j: (layer, 0, nh + j)),
                  pl.BlockSpec((None, th, d), lambda i, j: (layer, j, 0))],
        out_specs=pl.BlockSpec((tm, d), lambda i, j: (i, 0)),
        out_shape=jax.ShapeDtypeStruct((t, d), bf16),
        scratch_shapes=[pltpu.VMEM((tm, d), f32)],
        compiler_params=_cparams(("arbitrary", "arbitrary"), 56), name="ffn",
    )(h, w_gu, w_gu, w_down)


def _pack_w_in(w_in):
    depth, d, _ = w_in.shape
    o_qa, o_ka, o_hy = 0, GQA_HEADS * HEAD_DIM, (GQA_HEADS + 2 * GQA_KV_HEADS) * HEAD_DIM
    o_qc = o_hy + 3 * HY_C
    o_ckv = o_qc + MLA_HEADS * MLA_QK
    o_kr = o_ckv + MLA_RANK
    o_gate = o_kr + MLA_ROPE
    qc = w_in[:, :, o_qc:o_ckv].reshape(depth, d, MLA_HEADS, MLA_QK)
    qc = jnp.pad(qc, ((0, 0), (0, 0), (0, 0), (0, MLA_QPAD - MLA_QK))).reshape(depth, d, MLA_HEADS * MLA_QPAD)
    tail = NP_COLS - KR_OFF - MLA_ROPE
    parts = [qc, w_in[:, :, o_qa:o_ka], w_in[:, :, o_ka:o_hy], w_in[:, :, o_hy:o_qc], w_in[:, :, o_ckv:o_kr],
             w_in[:, :, o_gate:o_gate + GATE_RANK], w_in[:, :, o_kr:o_gate], jnp.zeros((depth, d, tail), w_in.dtype)]
    return jnp.concatenate(parts, axis=2).astype(bf16)


def _pack_w_kvb(w):
    depth = w.shape[0]
    w = w.reshape(depth, MLA_RANK, MLA_HEADS, 2, MLA_NOPE)
    return jnp.swapaxes(w, 2, 3).reshape(depth, MLA_RANK, 2 * MLA_HEADS * MLA_NOPE).astype(bf16)


def kernel(x, c, ctx, c_ctx, norm1_g, norm2_g, ada_down, ada_up, ada_b, w_in, gqa_q_norm, gqa_k_norm, hy_conv_w,
           hy_conv_b, hf_w1, hf_b1, hf_w2, hf_b2, hf_w3, hf_freq, hy_skip, mla_kv_norm, mla_w_kvb, mla_q_norm,
           mla_k_norm, w_gate_up, b_gate, w_branch, w_out, ffn_w_gu, ffn_w_down):
    batch, seq, d = x.shape
    ctx_len = ctx.shape[1]
    depth = w_in.shape[0]
    hidden = ffn_w_down.shape[1]
    assert batch == 2, "the Hyena long convolution packs the two batches as one complex sequence"
    assert w_in.shape[2] == IN_COLS and seq % GRID_W == 0
    n_lat, n_ctx = batch * seq, batch * ctx_len
    t_all = n_lat + n_ctx

    tm = _row_tile(seq, n_ctx, (512, 256, 128))
    tm_ew = _row_tile(seq, n_ctx, (256, 128))
    tn_in = 1536
    tj = 512
    th = 256
    tq_a = min(256, ctx_len)
    tq_c = min(1024, seq)
    tk = min(512, seq)

    def grp(tile):
        per = seq // tile
        return lambda i: jnp.minimum(i // per, batch)

    w_in_p = _pack_w_in(w_in)
    w_kvb_p = _pack_w_kvb(mla_w_kvb)
    wg_b, wb_b, wo_b = w_gate_up.astype(bf16), w_branch.astype(bf16), w_out.astype(bf16)
    wgu_b, wd_b = ffn_w_gu.astype(bf16), ffn_w_down.astype(bf16)
    bg = b_gate.reshape(depth, N_BRANCH, 1, d)
    gq_pad = jnp.pad(mla_q_norm, ((0, 0), (0, MLA_QPAD - MLA_QK)))
    gk_nope = mla_k_norm[:, :MLA_NOPE]
    gk_rope = jnp.pad(mla_k_norm[:, MLA_NOPE:], ((0, 0), (0, V7X_LANES - MLA_ROPE)))
    w1p = jnp.pad(hf_w1, ((0, 0), (0, V7X_LANES - HY_EMB), (0, 0)))
    w3r = jnp.swapaxes(hf_w3.reshape(depth, HY_FW, 2 * HY_ORDER, HY_C), 1, 2)
    bands = jnp.linspace(1e-4, HY_BANDS - 1, HY_BANDS, dtype=f32)
    band_row = jnp.concatenate([jnp.zeros((1,), f32), bands, bands,
                                jnp.zeros((V7X_LANES - HY_EMB,), f32)]).reshape(1, V7X_LANES)
    delta_row = jnp.abs(jnp.linspace(math.log(HY_DECAY_TARGET) / HY_DECAY_PCT_MIN,
                                     math.log(HY_DECAY_TARGET) / HY_DECAY_PCT_MAX, HY_C, dtype=f32)).reshape(1, HY_C)

    xs = jnp.concatenate([x.reshape(n_lat, d), ctx.reshape(n_ctx, d)], axis=0)
    tpos = jnp.arange(seq, dtype=jnp.int32)
    zpad = jnp.zeros((n_ctx,), f32)
    pos_row = jnp.concatenate([jnp.tile((tpos // GRID_W).astype(f32), batch), zpad]).reshape(t_all, 1)
    pos_col = jnp.concatenate([jnp.tile((tpos % GRID_W).astype(f32), batch), zpad]).reshape(t_all, 1)
    cond8 = jnp.concatenate([c, c_ctx[None, :], jnp.zeros((8 - batch - 1, d), f32)], axis=0)

    mod = ada_modulation(cond8, ada_down, ada_up, ada_b)
    mod5 = mod.reshape(depth, 8, N_MOD, 1, d)
    cos_a, sin_a, cos_c, sin_c = rope_tables(pos_row, pos_col, tm_ew)

    delta = None
    for l in range(depth):
        if l == 0:
            (h,) = prenorm(xs, mod5, l, grp(tm_ew), tm_ew, norm_g=norm1_g[l], shift_idx=0, scale_idx=1)
        else:
            xs, h = prenorm(xs, mod5, l, grp(tm_ew), tm_ew, delta=delta, gate_idx=5, gate_layer=l - 1,
                            norm_g=norm1_g[l], shift_idx=0, scale_idx=1)
        p = matmul(h, w_in_p, l, tm, tn_in)

        qa, ka = gqa_prep(p, gqa_q_norm[l], gqa_k_norm[l], cos_a, sin_a, tm_ew)
        fa = dict(batch=batch, seq=seq, ctx_len=ctx_len, kv_heads=GQA_KV_HEADS, group=GQA_GROUP,
                  dq=HEAD_DIM, dv=HEAD_DIM, tk=tk)
        att_a = (flash_attention(qa, ka, p, VA_OFF, tq=tq_a, lat=True, **fa),
                 flash_attention(qa, ka, p, VA_OFF, tq=ctx_len, lat=False, **fa))

        qc, kc, vc = mla_prep(p, mla_kv_norm[l], w_kvb_p[l], gq_pad[l], gk_nope[l], gk_rope[l], cos_c, sin_c, tm_ew)
        fc = dict(batch=batch, seq=seq, ctx_len=ctx_len, kv_heads=MLA_HEADS, group=1, dq=MLA_QPAD, dv=MLA_V, tk=tk)
        att_c = (flash_attention(qc, kc, vc, 0, tq=tq_c, lat=True, **fc),
                 flash_attention(qc, kc, vc, 0, tq=ctx_len, lat=False, **fc))

        hf = (band_row, w1p[l], hf_b1[l].reshape(1, -1), hf_w2[l], hf_b2[l].reshape(1, -1), w3r[l], hf_freq[l],
              delta_row)
        uc_lat = short_conv(p, hy_conv_w[l], hy_conv_b[l], 0, n_lat, seq, min(512, seq), 512)
        uc_ctx = short_conv(p, hy_conv_w[l], hy_conv_b[l], n_lat, n_ctx, ctx_len, min(256, ctx_len), 512)
        hy = (hyena_mixer(uc_lat, seq, hf, hy_skip[l]), hyena_mixer(uc_ctx, ctx_len, hf, hy_skip[l]))

        delta = merge_out(p, att_a, hy, att_c, wg_b, bg, wb_b, wo_b, l, tm, tj)
        xs, h2 = prenorm(xs, mod5, l, grp(tm_ew), tm_ew, delta=delta, gate_idx=2, gate_layer=l,
                         norm_g=norm2_g[l], shift_idx=3, scale_idx=4)
        delta = ffn(h2, wgu_b, wd_b, l, tm, th)

    (out,) = prenorm(xs, mod5, depth - 1, grp(tm_ew), tm_ew, delta=delta, gate_idx=5, gate_layer=depth - 1,
                     rows=n_lat)
    return out.reshape(batch, seq, d)
```

```python
import functools
import math

import jax
import jax.numpy as jnp
import numpy as np
from jax import lax
from jax.experimental import pallas as pl
from jax.experimental.pallas import tpu as pltpu

f32 = jnp.float32
bf16 = jnp.bfloat16
u32 = jnp.uint32

GRID_W = 64
ROPE_THETA = 10000.0
EPS = 1e-6
HEAD_DIM = 128
GQA_HEADS = 8
GQA_KV_HEADS = 2
GQA_GROUP = GQA_HEADS // GQA_KV_HEADS
HY_C = 1024
HY_ORDER = 2
HY_EMB = 33
HY_BANDS = (HY_EMB - 1) // 2
HY_FW = 64
HY_DECAY_TARGET = 1e-2
HY_DECAY_PCT_MIN = 0.3
HY_DECAY_PCT_MAX = 1.5
MLA_HEADS = 8
MLA_NOPE = 128
MLA_ROPE = 64
MLA_QK = MLA_NOPE + MLA_ROPE
MLA_V = 128
MLA_RANK = 512
N_BRANCH = 3
BRANCH_W = 1024
GATE_RANK = 256
ADA_RANK = 256
N_MOD = 6
IN_COLS = 6976

V7X_LANES = 128
V7X_SUBLANES = 8
V7X_VMEM_BYTES = 64 * 1024 * 1024
MIB = 1024 * 1024

MLA_QPAD = 2 * V7X_LANES
QC_OFF = 0
QA_OFF = QC_OFF + MLA_HEADS * MLA_QPAD
KA_OFF = QA_OFF + GQA_HEADS * HEAD_DIM
VA_OFF = KA_OFF + GQA_KV_HEADS * HEAD_DIM
HY_OFF = VA_OFF + GQA_KV_HEADS * HEAD_DIM
CKV_OFF = HY_OFF + 3 * HY_C
GATE_OFF = CKV_OFF + MLA_RANK
KR_OFF = GATE_OFF + GATE_RANK
NP_COLS = 7680

LOG2E = math.log2(math.e)


def _cparams(sem, vmem_mib):
    return pltpu.CompilerParams(dimension_semantics=sem, vmem_limit_bytes=int(vmem_mib * MIB))


def _row_tile(s, nctx, cands):
    for t in cands:
        if s % t == 0 and nctx % t == 0:
            return t
    raise ValueError("no row tile fits")


def _ada_kernel(cond_ref, down_ref, up_ref, b_ref, o_ref, t_ref):
    @pl.when(pl.program_id(1) == 0)
    def _():
        c = cond_ref[...]
        c = c * (1.0 / (1.0 + jnp.exp(-c)))
        t_ref[...] = jnp.dot(c, down_ref[...], precision=lax.Precision.HIGHEST, preferred_element_type=f32)

    o_ref[...] = jnp.dot(t_ref[...], up_ref[...], precision=lax.Precision.HIGHEST,
                         preferred_element_type=f32) + b_ref[...]


def ada_modulation(cond8, ada_down, ada_up, ada_b):
    depth, d, _ = ada_down.shape
    n = ada_up.shape[2]
    tn = 2048 if n % 2048 == 0 else 512
    return pl.pallas_call(
        _ada_kernel,
        grid=(depth, n // tn),
        in_specs=[
            pl.BlockSpec((8, d), lambda l, j: (0, 0)),
            pl.BlockSpec((None, d, ADA_RANK), lambda l, j: (l, 0, 0)),
            pl.BlockSpec((None, ADA_RANK, tn), lambda l, j: (l, 0, j)),
            pl.BlockSpec((None, 1, tn), lambda l, j: (l, 0, j)),
        ],
        out_specs=pl.BlockSpec((None, 8, tn), lambda l, j: (l, 0, j)),
        out_shape=jax.ShapeDtypeStruct((depth, 8, n), f32),
        scratch_shapes=[pltpu.VMEM((8, ADA_RANK), f32)],
        compiler_params=_cparams(("arbitrary", "arbitrary"), 40),
        name="ada_modulation",
    )(cond8, ada_down, ada_up, ada_b.reshape(depth, 1, n))


def _rope_table_kernel(pr_ref, pc_ref, ca_ref, sa_ref, cc_ref, sc_ref):
    shape = ca_ref.shape
    lane = lax.broadcasted_iota(jnp.int32, shape, 1)
    pr = jnp.broadcast_to(pr_ref[...], shape)
    pc = jnp.broadcast_to(pc_ref[...], shape)
    log_theta = math.log(ROPE_THETA)
    fa = jnp.exp((lane & 31).astype(f32) * (-log_theta / 32.0))
    ang = jnp.where(lane < 64, pr, pc) * fa
    ca_ref[...] = jnp.cos(ang)
    sa_ref[...] = jnp.where((lane & 63) < 32, -1.0, 1.0) * jnp.sin(ang)
    fc = jnp.exp((lane & 15).astype(f32) * (-log_theta / 16.0))
    angc = jnp.where(lane < 64, jnp.where(lane < 32, pr, pc) * fc, 0.0)
    cc_ref[...] = jnp.cos(angc)
    sc_ref[...] = jnp.where((lane & 31) < 16, -1.0, 1.0) * jnp.sin(angc)


def rope_tables(pos_row, pos_col, tm):
    t = pos_row.shape[0]
    spec1 = pl.BlockSpec((tm, 1), lambda i: (i, 0))
    spec = pl.BlockSpec((tm, V7X_LANES), lambda i: (i, 0))
    sh = jax.ShapeDtypeStruct((t, V7X_LANES), f32)
    return pl.pallas_call(
        _rope_table_kernel, grid=(t // tm,), in_specs=[spec1, spec1], out_specs=[spec] * 4,
        out_shape=[sh] * 4, compiler_params=_cparams(("arbitrary",), 32), name="rope_tables",
    )(pos_row, pos_col)


def _prenorm_kernel(*refs, has_delta, want_h):
    if has_delta:
        x_ref, d_ref, gate_ref = refs[:3]
        rest = refs[3:]
    else:
        x_ref = refs[0]
        rest = refs[1:]
    x = x_ref[...]
    if has_delta:
        x = x + gate_ref[...] * d_ref[...].astype(f32)
    if want_h:
        g_ref, shift_ref, scale_ref = rest[:3]
        outs = rest[3:]
    else:
        outs = rest
    k = 0
    if has_delta:
        outs[k][...] = x
        k += 1
    if want_h:
        y = x * lax.rsqrt(jnp.mean(x * x, axis=-1, keepdims=True) + EPS)
        y = y * g_ref[...]
        outs[k][...] = (y * (1.0 + scale_ref[...]) + shift_ref[...]).astype(bf16)


def prenorm(x, mod5, layer, grp_of_block, tm, *, delta=None, gate_idx=None, gate_layer=None,
            norm_g=None, shift_idx=None, scale_idx=None, rows=None):
    t, d = x.shape
    rows = t if rows is None else rows
    has_delta = delta is not None
    want_h = norm_g is not None
    row_spec = pl.BlockSpec((tm, d), lambda i: (i, 0))

    def mod_spec(lyr, which):
        return pl.BlockSpec((None, None, None, 1, d), lambda i: (lyr, grp_of_block(i), which, 0, 0))

    in_specs, args = [row_spec], [x]
    if has_delta:
        in_specs += [row_spec, mod_spec(gate_layer, gate_idx)]
        args += [delta, mod5]
    if want_h:
        in_specs += [pl.BlockSpec((1, d), lambda i: (0, 0)), mod_spec(layer, shift_idx), mod_spec(layer, scale_idx)]
        args += [norm_g.reshape(1, d), mod5, mod5]
    out_specs, out_shape = [], []
    if has_delta:
        out_specs.append(row_spec)
        out_shape.append(jax.ShapeDtypeStruct((rows, d), f32))
    if want_h:
        out_specs.append(row_spec)
        out_shape.append(jax.ShapeDtypeStruct((rows, d), bf16))
    return pl.pallas_call(
        functools.partial(_prenorm_kernel, has_delta=has_delta, want_h=want_h),
        grid=(rows // tm,), in_specs=in_specs, out_specs=out_specs, out_shape=out_shape,
        compiler_params=_cparams(("arbitrary",), 48), name="prenorm",
    )(*args)


def _matmul_kernel(a_ref, w_ref, o_ref):
    o_ref[...] = jnp.dot(a_ref[...], w_ref[...], preferred_element_type=f32).astype(o_ref.dtype)


def matmul(a, w, layer, tm, tn):
    t, k = a.shape
    n = w.shape[2]
    return pl.pallas_call(
        _matmul_kernel, grid=(t // tm, n // tn),
        in_specs=[pl.BlockSpec((tm, k), lambda i, j: (i, 0)),
                  pl.BlockSpec((None, k, tn), lambda i, j: (layer, 0, j))],
        out_specs=pl.BlockSpec((tm, tn), lambda i, j: (i, j)),
        out_shape=jax.ShapeDtypeStruct((t, n), bf16),
        compiler_params=_cparams(("arbitrary", "arbitrary"), 48), name="in_proj",
    )(a, w)


def _rope128(x, cos, sin_signed, half):
    lane = lax.broadcasted_iota(jnp.int32, x.shape, 1)
    first = (lane & (2 * half - 1)) < half
    swapped = jnp.where(first, pltpu.roll(x, V7X_LANES - half, 1), pltpu.roll(x, half, 1))
    return x * cos + swapped * sin_signed


def _gqa_prep_kernel(q_ref, kv_ref, gq_ref, gk_ref, cos_ref, sin_ref, qo_ref, ko_ref):
    cos, sin = cos_ref[...], sin_ref[...]
    qscale = HEAD_DIM ** -0.5 * LOG2E
    for h in range(GQA_HEADS):
        sl = slice(h * HEAD_DIM, (h + 1) * HEAD_DIM)
        q = q_ref[:, sl].astype(f32)
        q = q * lax.rsqrt(jnp.mean(q * q, axis=-1, keepdims=True) + EPS) * gq_ref[...]
        qo_ref[:, sl] = (_rope128(q, cos, sin, 32) * qscale).astype(bf16)
    for h in range(GQA_KV_HEADS):
        sl = slice(h * HEAD_DIM, (h + 1) * HEAD_DIM)
        k = kv_ref[:, sl].astype(f32)
        k = k * lax.rsqrt(jnp.mean(k * k, axis=-1, keepdims=True) + EPS) * gk_ref[...]
        ko_ref[:, sl] = _rope128(k, cos, sin, 32).astype(bf16)


def gqa_prep(p, gq, gk, cos_a, sin_a, tm):
    t = p.shape[0]
    qw, kw = GQA_HEADS * HEAD_DIM, GQA_KV_HEADS * HEAD_DIM
    tab = pl.BlockSpec((tm, V7X_LANES), lambda i: (i, 0))
    vec = pl.BlockSpec((1, HEAD_DIM), lambda i: (0, 0))
    return pl.pallas_call(
        _gqa_prep_kernel, grid=(t // tm,),
        in_specs=[pl.BlockSpec((tm, qw), lambda i: (i, QA_OFF // qw)),
                  pl.BlockSpec((tm, 2 * kw), lambda i: (i, KA_OFF // (2 * kw))), vec, vec, tab, tab],
        out_specs=[pl.BlockSpec((tm, qw), lambda i: (i, 0)), pl.BlockSpec((tm, kw), lambda i: (i, 0))],
        out_shape=[jax.ShapeDtypeStruct((t, qw), bf16), jax.ShapeDtypeStruct((t, kw), bf16)],
        compiler_params=_cparams(("arbitrary",), 32), name="gqa_prep",
    )(p, p, gq.reshape(1, -1), gk.reshape(1, -1), cos_a, sin_a)


def _mla_prep_kernel(q_ref, ckv_ref, kr_ref, kvg_ref, wkvb_ref, gq_ref, gkn_ref, gkr_ref, cos_ref, sin_ref,
                     qo_ref, ko_ref, vo_ref):
    cos, sin = cos_ref[...], sin_ref[...]
    qscale = MLA_QK ** -0.5 * LOG2E
    inv_qk = 1.0 / MLA_QK
    for h in range(MLA_HEADS):
        lo = h * MLA_QPAD
        qn = q_ref[:, lo:lo + V7X_LANES].astype(f32)
        qr = q_ref[:, lo + V7X_LANES:lo + MLA_QPAD].astype(f32)
        ss = jnp.sum(qn * qn, axis=-1, keepdims=True) + jnp.sum(qr * qr, axis=-1, keepdims=True)
        r = lax.rsqrt(ss * inv_qk + EPS) * qscale
        qo_ref[:, lo:lo + V7X_LANES] = (qn * r * gq_ref[:, :V7X_LANES]).astype(bf16)
        qo_ref[:, lo + V7X_LANES:lo + MLA_QPAD] = (_rope128(qr * gq_ref[:, V7X_LANES:], cos, sin, 16) * r).astype(bf16)
    c = ckv_ref[...].astype(f32)
    cn = c * lax.rsqrt(jnp.mean(c * c, axis=-1, keepdims=True) + EPS) * kvg_ref[...]
    kv = jnp.dot(cn.astype(bf16), wkvb_ref[...], preferred_element_type=f32)
    kr = kr_ref[...].astype(f32)
    ss_r = jnp.sum(kr * kr, axis=-1, keepdims=True)
    kr_rot = _rope128(kr * gkr_ref[...], cos, sin, 16)
    nv = MLA_HEADS * MLA_NOPE
    for h in range(MLA_HEADS):
        kn = kv[:, h * MLA_NOPE:(h + 1) * MLA_NOPE]
        r = lax.rsqrt((jnp.sum(kn * kn, axis=-1, keepdims=True) + ss_r) * inv_qk + EPS)
        lo = h * MLA_QPAD
        ko_ref[:, lo:lo + V7X_LANES] = (kn * r * gkn_ref[...]).astype(bf16)
        ko_ref[:, lo + V7X_LANES:lo + MLA_QPAD] = (kr_rot * r).astype(bf16)
    vo_ref[...] = kv[:, nv:].astype(bf16)


def mla_prep(p, kv_g, wkvb_p, gq_pad, gk_nope, gk_rope, cos_c, sin_c, tm):
    t = p.shape[0]
    qw = MLA_HEADS * MLA_QPAD
    vw = MLA_HEADS * MLA_V
    tab = pl.BlockSpec((tm, V7X_LANES), lambda i: (i, 0))

    def vec(n):
        return pl.BlockSpec((1, n), lambda i: (0, 0))

    return pl.pallas_call(
        _mla_prep_kernel, grid=(t // tm,),
        in_specs=[pl.BlockSpec((tm, qw), lambda i: (i, QC_OFF // qw)),
                  pl.BlockSpec((tm, MLA_RANK), lambda i: (i, CKV_OFF // MLA_RANK)),
                  pl.BlockSpec((tm, V7X_LANES), lambda i: (i, KR_OFF // V7X_LANES)),
                  vec(MLA_RANK), pl.BlockSpec((MLA_RANK, 2 * vw), lambda i: (0, 0)),
                  vec(MLA_QPAD), vec(V7X_LANES), vec(V7X_LANES), tab, tab],
        out_specs=[pl.BlockSpec((tm, qw), lambda i: (i, 0)), pl.BlockSpec((tm, qw), lambda i: (i, 0)),
                   pl.BlockSpec((tm, vw), lambda i: (i, 0))],
        out_shape=[jax.ShapeDtypeStruct((t, qw), bf16), jax.ShapeDtypeStruct((t, qw), bf16),
                   jax.ShapeDtypeStruct((t, vw), bf16)],
        compiler_params=_cparams(("arbitrary",), 48), name="mla_prep",
    )(p, p, p, kv_g.reshape(1, -1), wkvb_p, gq_pad.reshape(1, -1), gk_nope.reshape(1, -1),
      gk_rope.reshape(1, -1), cos_c, sin_c)


FLASH_COLS = 256


def _scores(k, qt_ref, s_ref):
    s_ref[0:k.shape[0], :] = jnp.dot(k, qt_ref[...], preferred_element_type=f32)


def _softmax_pv(s_ref, tk, v, m_ref, l_ref, acc_ref):
    vt = v.T
    step = min(FLASH_COLS, s_ref.shape[1])
    for c0 in range(0, s_ref.shape[1], step):
        cols = slice(c0, c0 + step)
        s = s_ref[0:tk, cols]
        m_prev = m_ref[:, cols]
        m_new = jnp.maximum(m_prev, jnp.max(s, axis=0, keepdims=True))
        alpha = jnp.exp2(m_prev - m_new)
        pr = jnp.exp2(s - m_new)
        l_ref[:, cols] = alpha * l_ref[:, cols] + jnp.sum(pr, axis=0, keepdims=True)
        acc_ref[:, cols] = alpha * acc_ref[:, cols] + jnp.dot(vt, pr.astype(bf16), preferred_element_type=f32)
        m_ref[:, cols] = m_new


def _flash_kernel(*refs, group, dq, dv, tq, tk, n_lat):
    if n_lat:
        q_ref, kl_ref, vl_ref, kc_ref, vc_ref, o_ref, qt_ref, m_ref, l_ref, acc_ref, s0_ref, s1_ref = refs
    else:
        q_ref, kc_ref, vc_ref, o_ref, qt_ref, m_ref, l_ref, acc_ref, s0_ref, s1_ref = refs
    for g in range(group):
        qt_ref[:, g * tq:(g + 1) * tq] = q_ref[:, g * dq:(g + 1) * dq].T
    m_ref[...] = jnp.full(m_ref.shape, -1e30, f32)
    l_ref[...] = jnp.zeros(l_ref.shape, f32)
    acc_ref[...] = jnp.zeros(acc_ref.shape, f32)
    n_ctx = kc_ref.shape[0]
    _scores(kc_ref[...], qt_ref, s1_ref)
    if n_lat:
        def kchunk(j):
            return kl_ref[pl.ds(pl.multiple_of(j * tk, tk), tk), :]

        def vchunk(j):
            return vl_ref[pl.ds(pl.multiple_of(j * tk, tk), tk), :]

        assert n_lat % 2 == 0
        _scores(kchunk(0), qt_ref, s0_ref)
        _softmax_pv(s1_ref, n_ctx, vc_ref[...], m_ref, l_ref, acc_ref)

        def body(i, carry):
            _scores(kchunk(2 * i + 1), qt_ref, s1_ref)
            _softmax_pv(s0_ref, tk, vchunk(2 * i), m_ref, l_ref, acc_ref)
            _scores(kchunk(2 * i + 2), qt_ref, s0_ref)
            _softmax_pv(s1_ref, tk, vchunk(2 * i + 1), m_ref, l_ref, acc_ref)
            return carry
        lax.fori_loop(0, n_lat // 2 - 1, body, 0)
        _scores(kchunk(n_lat - 1), qt_ref, s1_ref)
        _softmax_pv(s0_ref, tk, vchunk(n_lat - 2), m_ref, l_ref, acc_ref)
        _softmax_pv(s1_ref, tk, vchunk(n_lat - 1), m_ref, l_ref, acc_ref)
    else:
        _softmax_pv(s1_ref, n_ctx, vc_ref[...], m_ref, l_ref, acc_ref)
    for g in range(group):
        cols = slice(g * tq, (g + 1) * tq)
        o_ref[:, g * dv:(g + 1) * dv] = (acc_ref[:, cols] / l_ref[:, cols]).T.astype(o_ref.dtype)


def flash_attention(q, k, v, v_col_off, *, batch, seq, ctx_len, kv_heads, group, dq, dv, tq, tk, lat):
    ctx_blk0 = batch * seq // ctx_len
    voff = v_col_off // dv
    rows = group * tq
    kc_spec = pl.BlockSpec((ctx_len, dq), lambda b, h, i: (ctx_blk0 + b, h))
    vc_spec = pl.BlockSpec((ctx_len, dv), lambda b, h, i: (ctx_blk0 + b, voff + h))
    if lat:
        nq = seq // tq
        in_specs = [pl.BlockSpec((tq, group * dq), lambda b, h, i: (b * nq + i, h)),
                    pl.BlockSpec((seq, dq), lambda b, h, i: (b, h)),
                    pl.BlockSpec((seq, dv), lambda b, h, i: (b, voff + h)), kc_spec, vc_spec]
        args = [q, k, v, k, v]
        out_spec = pl.BlockSpec((tq, group * dv), lambda b, h, i: (b * nq + i, h))
        out_rows = batch * seq
        n_lat = seq // tk
    else:
        nq = 1
        assert tq == ctx_len
        in_specs = [pl.BlockSpec((tq, group * dq), lambda b, h, i: (ctx_blk0 + b, h)), kc_spec, vc_spec]
        args = [q, k, v]
        out_spec = pl.BlockSpec((tq, group * dv), lambda b, h, i: (b, h))
        out_rows = batch * ctx_len
        n_lat = 0
    return pl.pallas_call(
        functools.partial(_flash_kernel, group=group, dq=dq, dv=dv, tq=tq, tk=tk, n_lat=n_lat),
        grid=(batch, kv_heads, nq), in_specs=in_specs, out_specs=out_spec,
        out_shape=jax.ShapeDtypeStruct((out_rows, kv_heads * group * dv), bf16),
        scratch_shapes=[pltpu.VMEM((dq, rows), bf16), pltpu.VMEM((1, rows), f32), pltpu.VMEM((1, rows), f32),
                        pltpu.VMEM((dv, rows), f32)] + [pltpu.VMEM((max(tk, ctx_len) if lat else ctx_len, rows), f32)] * 2,
        compiler_params=_cparams(("arbitrary", "arbitrary", "arbitrary"), 48),
        name="flash_lat" if lat else "flash_ctx",
    )(*args)


CONV_HALO = 16


def _conv3_kernel(prev_ref, cur_ref, next_ref, w_ref, b_ref, o_ref, *, blocks_per_seq):
    rows = cur_ref.shape[0]
    pos = pl.program_id(0) % blocks_per_seq
    x = cur_ref[...].astype(f32)
    row = lax.broadcasted_iota(jnp.int32, x.shape, 0)
    prev_row = jnp.where(pos == 0, 0.0, prev_ref[...].astype(f32)[CONV_HALO - 1:CONV_HALO, :])
    next_row = jnp.where(pos == blocks_per_seq - 1, 0.0, next_ref[...].astype(f32)[0:1, :])
    xm = jnp.where(row == 0, prev_row, pltpu.roll(x, 1, 0))
    xp = jnp.where(row == rows - 1, next_row, pltpu.roll(x, rows - 1, 0))
    o_ref[...] = xm * w_ref[0:1, :] + x * w_ref[1:2, :] + xp * w_ref[2:3, :] + b_ref[...]


def short_conv(p, conv_w, conv_b, row0, nrows, seq_len, rows, cw):
    width = 3 * HY_C
    rb0 = row0 // rows
    sub = rows // CONV_HALO
    last_halo = p.shape[0] // CONV_HALO - 1
    c0 = HY_OFF // cw
    return pl.pallas_call(
        functools.partial(_conv3_kernel, blocks_per_seq=seq_len // rows),
        grid=(nrows // rows, width // cw),
        in_specs=[
            pl.BlockSpec((CONV_HALO, cw), lambda r, c: (jnp.maximum((rb0 + r) * sub - 1, 0), c0 + c)),
            pl.BlockSpec((rows, cw), lambda r, c: (rb0 + r, c0 + c)),
            pl.BlockSpec((CONV_HALO, cw), lambda r, c: (jnp.minimum((rb0 + r + 1) * sub, last_halo), c0 + c)),
            pl.BlockSpec((3, cw), lambda r, c: (0, c)),
            pl.BlockSpec((1, cw), lambda r, c: (0, c)),
        ],
        out_specs=pl.BlockSpec((rows, cw), lambda r, c: (r, c)),
        out_shape=jax.ShapeDtypeStruct((nrows, width), f32),
        compiler_params=_cparams(("arbitrary", "arbitrary"), 32), name="short_conv",
    )(p, p, p, conv_w, conv_b.reshape(1, width))


@functools.lru_cache(maxsize=None)
def _dft_tables(length):
    nb = 128 if length >= 1024 else 16
    n_fft = 2 * length
    na = n_fft // nb
    nah = na // 2
    lo = np.arange(nb, dtype=np.int64)[:, None, None]
    k1 = np.arange(na, dtype=np.int64)[None, :, None]
    hi = np.arange(nah, dtype=np.int64)[None, None, :]
    ang = 2.0 * np.pi * (((nb * hi + lo) * k1) % n_fft) / n_fft
    c, s = np.cos(ang), np.sin(ang)
    a1 = np.concatenate([np.concatenate([c, s], axis=2), np.concatenate([-s, c], axis=2)], axis=1)
    ct, st = np.swapaxes(c, 1, 2) / n_fft, np.swapaxes(s, 1, 2) / n_fft
    a3 = np.concatenate([np.concatenate([ct, -st], axis=2), np.concatenate([st, ct], axis=2)], axis=1)
    kk = np.arange(nb, dtype=np.int64)
    angb = 2.0 * np.pi * ((kk[:, None] * kk[None, :]) % nb) / nb
    cb, sb = np.cos(angb), np.sin(angb)
    mf = np.block([[cb, sb], [-sb, cb]])
    mfc = np.block([[cb, -sb], [sb, cb]])
    return dict(nb=nb, na=na, nah=nah,
                a1=jnp.asarray(a1, dtype=bf16), a1r=jnp.asarray(a1[:, :, :nah], dtype=bf16),
                a3=jnp.asarray(a3, dtype=bf16), mf=jnp.asarray(mf, dtype=bf16), mfc=jnp.asarray(mfc, dtype=bf16),
                m2f=jnp.asarray(np.concatenate([mf, mfc], axis=1), dtype=bf16))


def _tap_times(j, nb, nah):
    ridx = lax.broadcasted_iota(jnp.int32, (V7X_SUBLANES * nah, 1), 0)
    return nb * (ridx & (nah - 1)) + (j * V7X_SUBLANES + (ridx >> (nah.bit_length() - 1)))


def _filter_mlp_kernel(band_ref, w1_ref, b1_ref, w2_ref, b2_ref, freq_ref, o_ref, *, length, nb, nah):
    hp = lax.Precision.HIGHEST
    t = _tap_times(pl.program_id(0), nb, nah).astype(f32)
    t_unit = t / float(max(length - 1, 1))
    lane = lax.broadcasted_iota(jnp.int32, (t.shape[0], V7X_LANES), 1)
    ang = ((2.0 * math.pi / length) * t) * band_ref[...]
    feats = jnp.where(lane == 0, t_unit,
                      jnp.where(lane <= HY_BANDS, jnp.cos(ang), jnp.where(lane <= 2 * HY_BANDS, -jnp.sin(ang), 0.0)))
    h = jnp.sin(freq_ref[0:1, :] * (jnp.dot(feats, w1_ref[...], precision=hp, preferred_element_type=f32) + b1_ref[...]))
    o_ref[...] = jnp.sin(freq_ref[1:2, :] * (jnp.dot(h, w2_ref[...], precision=hp, preferred_element_type=f32)
                                            + b2_ref[...]))


def filter_mlp(tabs, length, band_row, w1p, b1, w2, b2, freq):
    nb, nah = tabs["nb"], tabs["nah"]
    rows = V7X_SUBLANES * nah
    full = lambda *shape: pl.BlockSpec(shape, lambda j: (0,) * len(shape))
    return pl.pallas_call(
        functools.partial(_filter_mlp_kernel, length=length, nb=nb, nah=nah), grid=(nb // V7X_SUBLANES,),
        in_specs=[full(1, V7X_LANES), full(V7X_LANES, HY_FW), full(1, HY_FW), full(HY_FW, HY_FW), full(1, HY_FW),
                  full(2, HY_FW)],
        out_specs=pl.BlockSpec((rows, HY_FW), lambda j: (j, 0)),
        out_shape=jax.ShapeDtypeStruct((length, HY_FW), f32),
        compiler_params=_cparams(("arbitrary",), 32), name="hy_filter_mlp",
    )(band_row, w1p, b1, w2, b2, freq)


def _filter_stage1_kernel(a1r_ref, h_ref, w3_ref, delta_ref, o_ref, ss_ref, *, length, nb, nah):
    j = pl.program_id(1)
    hp = lax.Precision.HIGHEST
    t_int = _tap_times(j, nb, nah)
    t_unit = t_int.astype(f32) / float(max(length - 1, 1))
    decay = jnp.exp(-t_unit * delta_ref[...])
    h = h_ref[...]

    @pl.when(j == 0)
    def _():
        ss_ref[...] = jnp.zeros(ss_ref.shape, f32)

    na = 2 * nah
    for o in range(HY_ORDER):
        fwd = jnp.dot(h, w3_ref[2 * o], precision=hp, preferred_element_type=f32) * decay
        bwd = jnp.dot(h, w3_ref[2 * o + 1], precision=hp, preferred_element_type=f32) * decay
        bwd = jnp.where(t_int == 0, 0.0, bwd)
        ss_ref[o:o + 1, :] += jnp.sum(fwd * fwd + bwd * bwd, axis=0, keepdims=True)
        fb, bb = fwd.astype(bf16), bwd.astype(bf16)
        for l in range(V7X_SUBLANES):
            a = a1r_ref[l]
            ff = jnp.dot(a, fb[l * nah:(l + 1) * nah], preferred_element_type=f32)
            gg = jnp.dot(a, bb[l * nah:(l + 1) * nah], preferred_element_type=f32)
            o_ref[o, 0, l] = _pack_complex(ff[:na], ff[na:])
            o_ref[o, 1, l] = _pack_complex(gg[:na], -gg[na:])


def filter_stage1(tabs, length, hmlp, w3r, delta_row):
    nb, na, nah = tabs["nb"], tabs["na"], tabs["nah"]
    ncb = HY_C // V7X_LANES
    return pl.pallas_call(
        functools.partial(_filter_stage1_kernel, length=length, nb=nb, nah=nah),
        grid=(ncb, nb // V7X_SUBLANES),
        in_specs=[pl.BlockSpec((V7X_SUBLANES, 2 * na, nah), lambda c, j: (j, 0, 0)),
                  pl.BlockSpec((V7X_SUBLANES * nah, HY_FW), lambda c, j: (j, 0)),
                  pl.BlockSpec((2 * HY_ORDER, HY_FW, V7X_LANES), lambda c, j: (0, 0, c)),
                  pl.BlockSpec((1, V7X_LANES), lambda c, j: (0, c))],
        out_specs=[pl.BlockSpec((HY_ORDER, 2, V7X_SUBLANES, na, V7X_LANES), lambda c, j: (0, 0, j, 0, c)),
                   pl.BlockSpec((V7X_SUBLANES, V7X_LANES), lambda c, j: (0, c))],
        out_shape=[jax.ShapeDtypeStruct((HY_ORDER, 2, nb, na, HY_C), u32),
                   jax.ShapeDtypeStruct((V7X_SUBLANES, HY_C), f32)],
        compiler_params=_cparams(("arbitrary", "arbitrary"), 48), name="hy_filter_s1",
    )(tabs["a1r"], hmlp, w3r, delta_row)


def _tile_rows(ref, l):
    tiles = math.prod(ref.shape[:-2])
    flat = ref.reshape(tiles * V7X_SUBLANES, ref.shape[-1])
    return flat.at[pl.ds(l, tiles, stride=V7X_SUBLANES), :]


def _pack_complex(re, im):
    def rne(x):
        b = pltpu.bitcast(x, u32)
        return b + (jnp.uint32(0x7FFF) + ((b >> 16) & jnp.uint32(1)))
    return (rne(re) & jnp.uint32(0xFFFF0000)) | (rne(im) >> 16)


def _unpack_complex(w):
    return pltpu.bitcast(w & jnp.uint32(0xFFFF0000), f32), pltpu.bitcast(w << 16, f32)


def _filter_stage2_kernel(x_ref, m_ref, o_ref, *, nb):
    for kl in range(V7X_SUBLANES):
        re, im = _unpack_complex(_tile_rows(x_ref, kl)[...])
        rhs = jnp.concatenate([re[:nb], im[:nb], re[nb:], im[nb:]], axis=0).astype(bf16)
        h = jnp.dot(m_ref[...], rhs, preferred_element_type=f32)
        o_ref[kl] = _pack_complex(h[:nb], h[nb:])


def filter_stage2(tabs, hin):
    nb, na = tabs["nb"], tabs["na"]
    ncb = HY_C // V7X_LANES
    return pl.pallas_call(
        functools.partial(_filter_stage2_kernel, nb=nb), grid=(HY_ORDER, ncb, na // V7X_SUBLANES),
        in_specs=[pl.BlockSpec((None, 2, nb, V7X_SUBLANES, V7X_LANES), lambda o, c, j: (o, 0, 0, j, c)),
                  pl.BlockSpec((2 * nb, 4 * nb), lambda o, c, j: (0, 0))],
        out_specs=pl.BlockSpec((None, V7X_SUBLANES, nb, V7X_LANES), lambda o, c, j: (o, j, 0, c)),
        out_shape=jax.ShapeDtypeStruct((HY_ORDER, na, nb, HY_C), u32),
        compiler_params=_cparams(("arbitrary", "arbitrary", "arbitrary"), 48), name="hy_filter_s2",
    )(hin, tabs["m2f"])


def _conv_stage1_kernel(zr_ref, zi_ref, a1_ref, o_ref, *, na):
    for l in range(V7X_SUBLANES):
        rhs = jnp.concatenate([_tile_rows(zr_ref, l)[...], _tile_rows(zi_ref, l)[...]], axis=0).astype(bf16)
        y = jnp.dot(a1_ref[l], rhs, preferred_element_type=f32)
        o_ref[l] = _pack_complex(y[:na], y[na:])


def conv_stage1(tabs, z4, col_off):
    nb, na, nah = tabs["nb"], tabs["na"], tabs["nah"]
    ncb = HY_C // V7X_LANES
    c0 = col_off // V7X_LANES
    return pl.pallas_call(
        functools.partial(_conv_stage1_kernel, na=na), grid=(ncb, nb // V7X_SUBLANES),
        in_specs=[pl.BlockSpec((None, nah, V7X_SUBLANES, V7X_LANES), lambda c, j: (0, 0, j, c0 + c)),
                  pl.BlockSpec((None, nah, V7X_SUBLANES, V7X_LANES), lambda c, j: (1, 0, j, c0 + c)),
                  pl.BlockSpec((V7X_SUBLANES, 2 * na, na), lambda c, j: (j, 0, 0))],
        out_specs=pl.BlockSpec((V7X_SUBLANES, na, V7X_LANES), lambda c, j: (j, 0, c)),
        out_shape=jax.ShapeDtypeStruct((nb, na, HY_C), u32),
        compiler_params=_cparams(("arbitrary", "arbitrary"), 48), name="hy_conv_s1",
    )(z4, z4, tabs["a1"])


def _conv_stage2_kernel(x_ref, h_ref, mf_ref, mfc_ref, o_ref, *, nb):
    for kl in range(V7X_SUBLANES):
        re, im = _unpack_complex(_tile_rows(x_ref, kl)[...])
        rhs = jnp.concatenate([re, im], axis=0).astype(bf16)
        x = jnp.dot(mf_ref[...], rhs, preferred_element_type=f32)
        xr, xi = x[:nb], x[nb:]
        hr, hi = _unpack_complex(h_ref[kl])
        y = jnp.concatenate([xr * hr - xi * hi, xr * hi + xi * hr], axis=0).astype(bf16)
        e = jnp.dot(mfc_ref[...], y, preferred_element_type=f32)
        o_ref[kl] = _pack_complex(e[:nb], e[nb:])


def conv_stage2(tabs, din, hspec, order):
    nb, na = tabs["nb"], tabs["na"]
    ncb = HY_C // V7X_LANES
    return pl.pallas_call(
        functools.partial(_conv_stage2_kernel, nb=nb), grid=(ncb, na // V7X_SUBLANES),
        in_specs=[pl.BlockSpec((nb, V7X_SUBLANES, V7X_LANES), lambda c, j: (0, j, c)),
                  pl.BlockSpec((None, V7X_SUBLANES, nb, V7X_LANES), lambda c, j: (order, j, 0, c)),
                  pl.BlockSpec((2 * nb, 2 * nb), lambda c, j: (0, 0)),
                  pl.BlockSpec((2 * nb, 2 * nb), lambda c, j: (0, 0))],
        out_specs=pl.BlockSpec((V7X_SUBLANES, nb, V7X_LANES), lambda c, j: (j, 0, c)),
        out_shape=jax.ShapeDtypeStruct((na, nb, HY_C), u32),
        compiler_params=_cparams(("arbitrary", "arbitrary"), 48), name="hy_conv_s2",
    )(din, hspec, tabs["mf"], tabs["mfc"])


def _conv_stage3_kernel(e_ref, a3_ref, zr_ref, zi_ref, gr_ref, gi_ref, ss_ref, skip_ref, o_ref, *, order):
    rs = lax.rsqrt(ss_ref[order:order + 1, :] + EPS)
    skip = skip_ref[order:order + 1, :]
    for l in range(V7X_SUBLANES):
        re, im = _unpack_complex(_tile_rows(e_ref, l)[...])
        rhs = jnp.concatenate([re, im], axis=0).astype(bf16)
        y = jnp.dot(a3_ref[l], rhs, preferred_element_type=f32) * rs
        z = jnp.concatenate([_tile_rows(zr_ref, l)[...], _tile_rows(zi_ref, l)[...]], axis=0)
        g = jnp.concatenate([_tile_rows(gr_ref, l)[...], _tile_rows(gi_ref, l)[...]], axis=0)
        _tile_rows(o_ref, l)[...] = g * (y + skip * z)


def conv_stage3(tabs, ein, z4, z_off, g4, g_off, ss, skip, order):
    nb, na, nah = tabs["nb"], tabs["na"], tabs["nah"]
    ncb = HY_C // V7X_LANES
    zc, gc = z_off // V7X_LANES, g_off // V7X_LANES

    def slab(b, c0):
        return pl.BlockSpec((None, nah, V7X_SUBLANES, V7X_LANES), lambda c, j: (b, 0, j, c0 + c))

    return pl.pallas_call(
        functools.partial(_conv_stage3_kernel, order=order), grid=(ncb, nb // V7X_SUBLANES),
        in_specs=[pl.BlockSpec((na, V7X_SUBLANES, V7X_LANES), lambda c, j: (0, j, c)),
                  pl.BlockSpec((V7X_SUBLANES, na, 2 * na), lambda c, j: (j, 0, 0)),
                  slab(0, zc), slab(1, zc), slab(0, gc), slab(1, gc),
                  pl.BlockSpec((V7X_SUBLANES, V7X_LANES), lambda c, j: (0, c)),
                  pl.BlockSpec((HY_ORDER, V7X_LANES), lambda c, j: (0, c))],
        out_specs=pl.BlockSpec((2, nah, V7X_SUBLANES, V7X_LANES), lambda c, j: (0, 0, j, c)),
        out_shape=jax.ShapeDtypeStruct((2, nah, nb, HY_C), f32),
        compiler_params=_cparams(("arbitrary", "arbitrary"), 48), name="hy_conv_s3",
    )(ein, tabs["a3"], z4, z4, g4, g4, ss, skip)


def hyena_mixer(uc, length, hf, skip):
    tabs = _dft_tables(length)
    nb, nah = tabs["nb"], tabs["nah"]
    band_row, w1p, b1, w2, b2, w3r, freq, delta_row = hf
    hmlp = filter_mlp(tabs, length, band_row, w1p, b1, w2, b2, freq)
    hin, ss = filter_stage1(tabs, length, hmlp, w3r, delta_row)
    hspec = filter_stage2(tabs, hin)
    u4 = uc.reshape(2, nah, nb, 3 * HY_C)
    z4, z_off = u4, 0
    for o in range(HY_ORDER):
        din = conv_stage1(tabs, z4, z_off)
        ein = conv_stage2(tabs, din, hspec, o)
        z4 = conv_stage3(tabs, ein, z4, z_off, u4, (o + 1) * HY_C, ss, skip, o)
        z_off = 0
    return z4.reshape(2 * length, HY_C)


def _merge_kernel(gl_ref, al_ref, ac_ref, hl_ref, hc_ref, cl_ref, cc_ref, wg_ref, bg_ref, wb_ref, wo_ref,
                  o_ref, acc_ref, br_ref, *, lat_blocks):
    i, j = pl.program_id(0), pl.program_id(1)

    @pl.when(j == 0)
    def _():
        acc_ref[...] = jnp.zeros(acc_ref.shape, f32)

    @pl.when((j == 0) & (i < lat_blocks))
    def _():
        br_ref[0] = al_ref[...]
        br_ref[1] = hl_ref[...].astype(bf16)
        br_ref[2] = cl_ref[...]

    @pl.when((j == 0) & (i >= lat_blocks))
    def _():
        br_ref[0] = ac_ref[...]
        br_ref[1] = hc_ref[...].astype(bf16)
        br_ref[2] = cc_ref[...]

    gl = gl_ref[...]
    y = None
    for n in range(N_BRANCH):
        z = jnp.dot(gl, wg_ref[n], preferred_element_type=f32) + bg_ref[n]
        g = 1.0 / (1.0 + jnp.exp(-z))
        term = g * jnp.dot(br_ref[n], wb_ref[n], preferred_element_type=f32)
        y = term if y is None else y + term
    acc_ref[...] += jnp.dot(y.astype(bf16), wo_ref[...], preferred_element_type=f32)

    @pl.when(j == pl.num_programs(1) - 1)
    def _():
        o_ref[...] = acc_ref[...].astype(o_ref.dtype)


def merge_out(p, a, hy, c, wg, bg, wb, wo, layer, tm, tj):
    t = p.shape[0]
    d = wo.shape[2]
    nl = a[0].shape[0] // tm
    assert a[1].shape[0] % tm == 0 and t == a[0].shape[0] + a[1].shape[0]
    lat = pl.BlockSpec((tm, BRANCH_W), lambda i, j: (jnp.minimum(i, nl - 1), 0))
    ctx = pl.BlockSpec((tm, BRANCH_W), lambda i, j: (jnp.maximum(i - nl, 0), 0))
    return pl.pallas_call(
        functools.partial(_merge_kernel, lat_blocks=nl), grid=(t // tm, d // tj),
        in_specs=[pl.BlockSpec((tm, GATE_RANK), lambda i, j: (i, GATE_OFF // GATE_RANK)),
                  lat, ctx, lat, ctx, lat, ctx,
                  pl.BlockSpec((None, N_BRANCH, GATE_RANK, tj), lambda i, j: (layer, 0, 0, j)),
                  pl.BlockSpec((None, N_BRANCH, 1, tj), lambda i, j: (layer, 0, 0, j)),
                  pl.BlockSpec((None, N_BRANCH, BRANCH_W, tj), lambda i, j: (layer, 0, 0, j)),
                  pl.BlockSpec((None, tj, d), lambda i, j: (layer, j, 0))],
        out_specs=pl.BlockSpec((tm, d), lambda i, j: (i, 0)),
        out_shape=jax.ShapeDtypeStruct((t, d), bf16),
        scratch_shapes=[pltpu.VMEM((tm, d), f32), pltpu.VMEM((N_BRANCH, tm, BRANCH_W), bf16)],
        compiler_params=_cparams(("arbitrary", "arbitrary"), 58), name="merge_out",
    )(p, a[0], a[1], hy[0], hy[1], c[0], c[1], wg, bg, wb, wo)


def _ffn_kernel(h_ref, wg_ref, wu_ref, wd_ref, o_ref, acc_ref):
    j = pl.program_id(1)

    @pl.when(j == 0)
    def _():
        acc_ref[...] = jnp.zeros(acc_ref.shape, f32)

    h = h_ref[...]
    g = jnp.dot(h, wg_ref[...], preferred_element_type=f32)
    u = jnp.dot(h, wu_ref[...], preferred_element_type=f32)
    a = (g * (1.0 / (1.0 + jnp.exp(-g))) * u).astype(bf16)
    acc_ref[...] += jnp.dot(a, wd_ref[...], preferred_element_type=f32)

    @pl.when(j == pl.num_programs(1) - 1)
    def _():
        o_ref[...] = acc_ref[...].astype(o_ref.dtype)


def ffn(h, w_gu, w_down, layer, tm, th):
    t, d = h.shape
    hidden = w_down.shape[1]
    nh = hidden // th
    return pl.pallas_call(
        _ffn_kernel, grid=(t // tm, nh),
        in_specs=[pl.BlockSpec((tm, d), lambda i, j: (i, 0)),
                  pl.BlockSpec((None, d, th), lambda i, j: (layer, 0, j)),
                  pl.BlockSpec((None, d, th), lambda i, j: (layer, 0, nh + j)),
                  pl.BlockSpec((None, th, d), lambda i, j: (layer, j, 0))],
        out_specs=pl.BlockSpec((tm, d), lambda i, j: (i, 0)),
        out_shape=jax.ShapeDtypeStruct((t, d), bf16),
        scratch_shapes=[pltpu.VMEM((tm, d), f32)],
        compiler_params=_cparams(("arbitrary", "arbitrary"), 56), name="ffn",
    )(h, w_gu, w_gu, w_down)


def _pack_w_in(w_in):
    depth, d, _ = w_in.shape
    o_qa, o_ka, o_hy = 0, GQA_HEADS * HEAD_DIM, (GQA_HEADS + 2 * GQA_KV_HEADS) * HEAD_DIM
    o_qc = o_hy + 3 * HY_C
    o_ckv = o_qc + MLA_HEADS * MLA_QK
    o_kr = o_ckv + MLA_RANK
    o_gate = o_kr + MLA_ROPE
    qc = w_in[:, :, o_qc:o_ckv].reshape(depth, d, MLA_HEADS, MLA_QK)
    qc = jnp.pad(qc, ((0, 0), (0, 0), (0, 0), (0, MLA_QPAD - MLA_QK))).reshape(depth, d, MLA_HEADS * MLA_QPAD)
    tail = NP_COLS - KR_OFF - MLA_ROPE
    parts = [qc, w_in[:, :, o_qa:o_ka], w_in[:, :, o_ka:o_hy], w_in[:, :, o_hy:o_qc], w_in[:, :, o_ckv:o_kr],
             w_in[:, :, o_gate:o_gate + GATE_RANK], w_in[:, :, o_kr:o_gate], jnp.zeros((depth, d, tail), w_in.dtype)]
    return jnp.concatenate(parts, axis=2).astype(bf16)


def _pack_w_kvb(w):
    depth = w.shape[0]
    w = w.reshape(depth, MLA_RANK, MLA_HEADS, 2, MLA_NOPE)
    return jnp.swapaxes(w, 2, 3).reshape(depth, MLA_RANK, 2 * MLA_HEADS * MLA_NOPE).astype(bf16)


def kernel(x, c, ctx, c_ctx, norm1_g, norm2_g, ada_down, ada_up, ada_b, w_in, gqa_q_norm, gqa_k_norm, hy_conv_w,
           hy_conv_b, hf_w1, hf_b1, hf_w2, hf_b2, hf_w3, hf_freq, hy_skip, mla_kv_norm, mla_w_kvb, mla_q_norm,
           mla_k_norm, w_gate_up, b_gate, w_branch, w_out, ffn_w_gu, ffn_w_down):
    batch, seq, d = x.shape
    ctx_len = ctx.shape[1]
    depth = w_in.shape[0]
    hidden = ffn_w_down.shape[1]
    assert batch == 2, "the Hyena long convolution packs the two batches as one complex sequence"
    assert w_in.shape[2] == IN_COLS and seq % GRID_W == 0
    n_lat, n_ctx = batch * seq, batch * ctx_len
    t_all = n_lat + n_ctx

    tm = _row_tile(seq, n_ctx, (512, 256, 128))
    tm_ew = _row_tile(seq, n_ctx, (256, 128))
    tm_mm = 768 if t_all % 768 == 0 else tm
    tn_in = 1536
    tj = 512
    th = 256
    tq_a = min(256, ctx_len)
    tq_c = min(1024, seq)
    tk = min(512, seq)

    def grp(tile):
        per = seq // tile
        return lambda i: jnp.minimum(i // per, batch)

    w_in_p = _pack_w_in(w_in)
    w_kvb_p = _pack_w_kvb(mla_w_kvb)
    wg_b, wb_b, wo_b = w_gate_up.astype(bf16), w_branch.astype(bf16), w_out.astype(bf16)
    wgu_b, wd_b = ffn_w_gu.astype(bf16), ffn_w_down.astype(bf16)
    bg = b_gate.reshape(depth, N_BRANCH, 1, d)
    gq_pad = jnp.pad(mla_q_norm, ((0, 0), (0, MLA_QPAD - MLA_QK)))
    gk_nope = mla_k_norm[:, :MLA_NOPE]
    gk_rope = jnp.pad(mla_k_norm[:, MLA_NOPE:], ((0, 0), (0, V7X_LANES - MLA_ROPE)))
    w1p = jnp.pad(hf_w1, ((0, 0), (0, V7X_LANES - HY_EMB), (0, 0)))
    w3r = jnp.swapaxes(hf_w3.reshape(depth, HY_FW, 2 * HY_ORDER, HY_C), 1, 2)
    bands = jnp.linspace(1e-4, HY_BANDS - 1, HY_BANDS, dtype=f32)
    band_row = jnp.concatenate([jnp.zeros((1,), f32), bands, bands,
                                jnp.zeros((V7X_LANES - HY_EMB,), f32)]).reshape(1, V7X_LANES)
    delta_row = jnp.abs(jnp.linspace(math.log(HY_DECAY_TARGET) / HY_DECAY_PCT_MIN,
                                     math.log(HY_DECAY_TARGET) / HY_DECAY_PCT_MAX, HY_C, dtype=f32)).reshape(1, HY_C)

    xs = jnp.concatenate([x.reshape(n_lat, d), ctx.reshape(n_ctx, d)], axis=0)
    tpos = jnp.arange(seq, dtype=jnp.int32)
    zpad = jnp.zeros((n_ctx,), f32)
    pos_row = jnp.concatenate([jnp.tile((tpos // GRID_W).astype(f32), batch), zpad]).reshape(t_all, 1)
    pos_col = jnp.concatenate([jnp.tile((tpos % GRID_W).astype(f32), batch), zpad]).reshape(t_all, 1)
    cond8 = jnp.concatenate([c, c_ctx[None, :], jnp.zeros((8 - batch - 1, d), f32)], axis=0)

    mod = ada_modulation(cond8, ada_down, ada_up, ada_b)
    mod5 = mod.reshape(depth, 8, N_MOD, 1, d)
    cos_a, sin_a, cos_c, sin_c = rope_tables(pos_row, pos_col, tm_ew)

    delta = None
    for l in range(depth):
        if l == 0:
            (h,) = prenorm(xs, mod5, l, grp(tm_ew), tm_ew, norm_g=norm1_g[l], shift_idx=0, scale_idx=1)
        else:
            xs, h = prenorm(xs, mod5, l, grp(tm_ew), tm_ew, delta=delta, gate_idx=5, gate_layer=l - 1,
                            norm_g=norm1_g[l], shift_idx=0, scale_idx=1)
        p = matmul(h, w_in_p, l, tm_mm, tn_in)

        qa, ka = gqa_prep(p, gqa_q_norm[l], gqa_k_norm[l], cos_a, sin_a, tm_ew)
        fa = dict(batch=batch, seq=seq, ctx_len=ctx_len, kv_heads=GQA_KV_HEADS, group=GQA_GROUP,
                  dq=HEAD_DIM, dv=HEAD_DIM, tk=tk)
        att_a = (flash_attention(qa, ka, p, VA_OFF, tq=tq_a, lat=True, **fa),
                 flash_attention(qa, ka, p, VA_OFF, tq=ctx_len, lat=False, **fa))

        qc, kc, vc = mla_prep(p, mla_kv_norm[l], w_kvb_p[l], gq_pad[l], gk_nope[l], gk_rope[l], cos_c, sin_c, tm_ew)
        fc = dict(batch=batch, seq=seq, ctx_len=ctx_len, kv_heads=MLA_HEADS, group=1, dq=MLA_QPAD, dv=MLA_V, tk=tk)
        att_c = (flash_attention(qc, kc, vc, 0, tq=tq_c, lat=True, **fc),
                 flash_attention(qc, kc, vc, 0, tq=ctx_len, lat=False, **fc))

        hf = (band_row, w1p[l], hf_b1[l].reshape(1, -1), hf_w2[l], hf_b2[l].reshape(1, -1), w3r[l], hf_freq[l],
              delta_row)
        uc_lat = short_conv(p, hy_conv_w[l], hy_conv_b[l], 0, n_lat, seq, min(512, seq), 512)
        uc_ctx = short_conv(p, hy_conv_w[l], hy_conv_b[l], n_lat, n_ctx, ctx_len, min(256, ctx_len), 512)
        hy = (hyena_mixer(uc_lat, seq, hf, hy_skip[l]), hyena_mixer(uc_ctx, ctx_len, hf, hy_skip[l]))

        delta = merge_out(p, att_a, hy, att_c, wg_b, bg, wb_b, wo_b, l, tm, tj)
        xs, h2 = prenorm(xs, mod5, l, grp(tm_ew), tm_ew, delta=delta, gate_idx=2, gate_layer=l,
                         norm_g=norm2_g[l], shift_idx=3, scale_idx=4)
        delta = ffn(h2, wgu_b, wd_b, l, tm_mm, th)

    (out,) = prenorm(xs, mod5, depth - 1, grp(tm_ew), tm_ew, delta=delta, gate_idx=5, gate_layer=depth - 1,
                     rows=n_lat)
    return out.reshape(batch, seq, d)
```

```python
import functools
import math

import jax
import jax.numpy as jnp
import numpy as np
from jax import lax
from jax.experimental import pallas as pl
from jax.experimental.pallas import tpu as pltpu

f32 = jnp.float32
bf16 = jnp.bfloat16
u32 = jnp.uint32

GRID_W = 64
ROPE_THETA = 10000.0
EPS = 1e-6
HEAD_DIM = 128
GQA_HEADS = 8
GQA_KV_HEADS = 2
GQA_GROUP = GQA_HEADS // GQA_KV_HEADS
HY_C = 1024
HY_ORDER = 2
HY_EMB = 33
HY_BANDS = (HY_EMB - 1) // 2
HY_FW = 64
HY_DECAY_TARGET = 1e-2
HY_DECAY_PCT_MIN = 0.3
HY_DECAY_PCT_MAX = 1.5
MLA_HEADS = 8
MLA_NOPE = 128
MLA_ROPE = 64
MLA_QK = MLA_NOPE + MLA_ROPE
MLA_V = 128
MLA_RANK = 512
N_BRANCH = 3
BRANCH_W = 1024
GATE_RANK = 256
ADA_RANK = 256
N_MOD = 6
IN_COLS = 6976

V7X_LANES = 128
V7X_SUBLANES = 8
V7X_VMEM_BYTES = 64 * 1024 * 1024
MIB = 1024 * 1024

MLA_QPAD = 2 * V7X_LANES
QC_OFF = 0
QA_OFF = QC_OFF + MLA_HEADS * MLA_QPAD
KA_OFF = QA_OFF + GQA_HEADS * HEAD_DIM
VA_OFF = KA_OFF + GQA_KV_HEADS * HEAD_DIM
HY_OFF = VA_OFF + GQA_KV_HEADS * HEAD_DIM
CKV_OFF = HY_OFF + 3 * HY_C
GATE_OFF = CKV_OFF + MLA_RANK
KR_OFF = GATE_OFF + GATE_RANK
NP_COLS = 7680

LOG2E = math.log2(math.e)


def _cparams(sem, vmem_mib):
    return pltpu.CompilerParams(dimension_semantics=sem, vmem_limit_bytes=int(vmem_mib * MIB))


def _row_tile(s, nctx, cands):
    for t in cands:
        if s % t == 0 and nctx % t == 0:
            return t
    raise ValueError("no row tile fits")


def _ada_kernel(cond_ref, down_ref, up_ref, b_ref, o_ref, t_ref):
    @pl.when(pl.program_id(1) == 0)
    def _():
        c = cond_ref[...]
        c = c * (1.0 / (1.0 + jnp.exp(-c)))
        t_ref[...] = jnp.dot(c, down_ref[...], precision=lax.Precision.HIGHEST, preferred_element_type=f32)

    o_ref[...] = jnp.dot(t_ref[...], up_ref[...], precision=lax.Precision.HIGHEST,
                         preferred_element_type=f32) + b_ref[...]


def ada_modulation(cond8, ada_down, ada_up, ada_b):
    depth, d, _ = ada_down.shape
    n = ada_up.shape[2]
    tn = 2048 if n % 2048 == 0 else 512
    return pl.pallas_call(
        _ada_kernel,
        grid=(depth, n // tn),
        in_specs=[
            pl.BlockSpec((8, d), lambda l, j: (0, 0)),
            pl.BlockSpec((None, d, ADA_RANK), lambda l, j: (l, 0, 0)),
            pl.BlockSpec((None, ADA_RANK, tn), lambda l, j: (l, 0, j)),
            pl.BlockSpec((None, 1, tn), lambda l, j: (l, 0, j)),
        ],
        out_specs=pl.BlockSpec((None, 8, tn), lambda l, j: (l, 0, j)),
        out_shape=jax.ShapeDtypeStruct((depth, 8, n), f32),
        scratch_shapes=[pltpu.VMEM((8, ADA_RANK), f32)],
        compiler_params=_cparams(("arbitrary", "arbitrary"), 40),
        name="ada_modulation",
    )(cond8, ada_down, ada_up, ada_b.reshape(depth, 1, n))


def _rope_table_kernel(pr_ref, pc_ref, ca_ref, sa_ref, cc_ref, sc_ref):
    shape = ca_ref.shape
    lane = lax.broadcasted_iota(jnp.int32, shape, 1)
    pr = jnp.broadcast_to(pr_ref[...], shape)
    pc = jnp.broadcast_to(pc_ref[...], shape)
    log_theta = math.log(ROPE_THETA)
    fa = jnp.exp((lane & 31).astype(f32) * (-log_theta / 32.0))
    ang = jnp.where(lane < 64, pr, pc) * fa
    ca_ref[...] = jnp.cos(ang)
    sa_ref[...] = jnp.where((lane & 63) < 32, -1.0, 1.0) * jnp.sin(ang)
    fc = jnp.exp((lane & 15).astype(f32) * (-log_theta / 16.0))
    angc = jnp.where(lane < 64, jnp.where(lane < 32, pr, pc) * fc, 0.0)
    cc_ref[...] = jnp.cos(angc)
    sc_ref[...] = jnp.where((lane & 31) < 16, -1.0, 1.0) * jnp.sin(angc)


def rope_tables(pos_row, pos_col, tm):
    t = pos_row.shape[0]
    spec1 = pl.BlockSpec((tm, 1), lambda i: (i, 0))
    spec = pl.BlockSpec((tm, V7X_LANES), lambda i: (i, 0))
    sh = jax.ShapeDtypeStruct((t, V7X_LANES), f32)
    return pl.pallas_call(
        _rope_table_kernel, grid=(t // tm,), in_specs=[spec1, spec1], out_specs=[spec] * 4,
        out_shape=[sh] * 4, compiler_params=_cparams(("arbitrary",), 32), name="rope_tables",
    )(pos_row, pos_col)


def _prenorm_kernel(*refs, has_delta, want_h):
    if has_delta:
        x_ref, d_ref, gate_ref = refs[:3]
        rest = refs[3:]
    else:
        x_ref = refs[0]
        rest = refs[1:]
    x = x_ref[...]
    if has_delta:
        x = x + gate_ref[...] * d_ref[...].astype(f32)
    if want_h:
        g_ref, shift_ref, scale_ref = rest[:3]
        outs = rest[3:]
    else:
        outs = rest
    k = 0
    if has_delta:
        outs[k][...] = x
        k += 1
    if want_h:
        y = x * lax.rsqrt(jnp.mean(x * x, axis=-1, keepdims=True) + EPS)
        y = y * g_ref[...]
        outs[k][...] = (y * (1.0 + scale_ref[...]) + shift_ref[...]).astype(bf16)


def prenorm(x, mod5, layer, grp_of_block, tm, *, delta=None, gate_idx=None, gate_layer=None,
            norm_g=None, shift_idx=None, scale_idx=None, rows=None):
    t, d = x.shape
    rows = t if rows is None else rows
    has_delta = delta is not None
    want_h = norm_g is not None
    row_spec = pl.BlockSpec((tm, d), lambda i: (i, 0))

    def mod_spec(lyr, which):
        return pl.BlockSpec((None, None, None, 1, d), lambda i: (lyr, grp_of_block(i), which, 0, 0))

    in_specs, args = [row_spec], [x]
    if has_delta:
        in_specs += [row_spec, mod_spec(gate_layer, gate_idx)]
        args += [delta, mod5]
    if want_h:
        in_specs += [pl.BlockSpec((1, d), lambda i: (0, 0)), mod_spec(layer, shift_idx), mod_spec(layer, scale_idx)]
        args += [norm_g.reshape(1, d), mod5, mod5]
    out_specs, out_shape = [], []
    if has_delta:
        out_specs.append(row_spec)
        out_shape.append(jax.ShapeDtypeStruct((rows, d), f32))
    if want_h:
        out_specs.append(row_spec)
        out_shape.append(jax.ShapeDtypeStruct((rows, d), bf16))
    return pl.pallas_call(
        functools.partial(_prenorm_kernel, has_delta=has_delta, want_h=want_h),
        grid=(rows // tm,), in_specs=in_specs, out_specs=out_specs, out_shape=out_shape,
        compiler_params=_cparams(("arbitrary",), 48), name="prenorm",
    )(*args)


def _matmul_kernel(a_ref, w_ref, o_ref):
    o_ref[...] = jnp.dot(a_ref[...], w_ref[...], preferred_element_type=f32).astype(o_ref.dtype)


def matmul(a, w, layer, tm, tn):
    t, k = a.shape
    n = w.shape[2]
    return pl.pallas_call(
        _matmul_kernel, grid=(t // tm, n // tn),
        in_specs=[pl.BlockSpec((tm, k), lambda i, j: (i, 0)),
                  pl.BlockSpec((None, k, tn), lambda i, j: (layer, 0, j))],
        out_specs=pl.BlockSpec((tm, tn), lambda i, j: (i, j)),
        out_shape=jax.ShapeDtypeStruct((t, n), bf16),
        compiler_params=_cparams(("arbitrary", "arbitrary"), 48), name="in_proj",
    )(a, w)


def _rope128(x, cos, sin_signed, half):
    lane = lax.broadcasted_iota(jnp.int32, x.shape, 1)
    first = (lane & (2 * half - 1)) < half
    swapped = jnp.where(first, pltpu.roll(x, V7X_LANES - half, 1), pltpu.roll(x, half, 1))
    return x * cos + swapped * sin_signed


def _gqa_prep_kernel(q_ref, kv_ref, gq_ref, gk_ref, cos_ref, sin_ref, qo_ref, ko_ref):
    cos, sin = cos_ref[...], sin_ref[...]
    qscale = HEAD_DIM ** -0.5 * LOG2E
    for h in range(GQA_HEADS):
        sl = slice(h * HEAD_DIM, (h + 1) * HEAD_DIM)
        q = q_ref[:, sl].astype(f32)
        q = q * lax.rsqrt(jnp.mean(q * q, axis=-1, keepdims=True) + EPS) * gq_ref[...]
        qo_ref[:, sl] = (_rope128(q, cos, sin, 32) * qscale).astype(bf16)
    for h in range(GQA_KV_HEADS):
        sl = slice(h * HEAD_DIM, (h + 1) * HEAD_DIM)
        k = kv_ref[:, sl].astype(f32)
        k = k * lax.rsqrt(jnp.mean(k * k, axis=-1, keepdims=True) + EPS) * gk_ref[...]
        ko_ref[:, sl] = _rope128(k, cos, sin, 32).astype(bf16)


def gqa_prep(p, gq, gk, cos_a, sin_a, tm):
    t = p.shape[0]
    qw, kw = GQA_HEADS * HEAD_DIM, GQA_KV_HEADS * HEAD_DIM
    tab = pl.BlockSpec((tm, V7X_LANES), lambda i: (i, 0))
    vec = pl.BlockSpec((1, HEAD_DIM), lambda i: (0, 0))
    return pl.pallas_call(
        _gqa_prep_kernel, grid=(t // tm,),
        in_specs=[pl.BlockSpec((tm, qw), lambda i: (i, QA_OFF // qw)),
                  pl.BlockSpec((tm, 2 * kw), lambda i: (i, KA_OFF // (2 * kw))), vec, vec, tab, tab],
        out_specs=[pl.BlockSpec((tm, qw), lambda i: (i, 0)), pl.BlockSpec((tm, kw), lambda i: (i, 0))],
        out_shape=[jax.ShapeDtypeStruct((t, qw), bf16), jax.ShapeDtypeStruct((t, kw), bf16)],
        compiler_params=_cparams(("arbitrary",), 32), name="gqa_prep",
    )(p, p, gq.reshape(1, -1), gk.reshape(1, -1), cos_a, sin_a)


def _mla_prep_kernel(q_ref, ckv_ref, kr_ref, kvg_ref, wkvb_ref, gq_ref, gkn_ref, gkr_ref, cos_ref, sin_ref,
                     qo_ref, ko_ref, vo_ref):
    cos, sin = cos_ref[...], sin_ref[...]
    qscale = MLA_QK ** -0.5 * LOG2E
    inv_qk = 1.0 / MLA_QK
    for h in range(MLA_HEADS):
        lo = h * MLA_QPAD
        qn = q_ref[:, lo:lo + V7X_LANES].astype(f32)
        qr = q_ref[:, lo + V7X_LANES:lo + MLA_QPAD].astype(f32)
        ss = jnp.sum(qn * qn, axis=-1, keepdims=True) + jnp.sum(qr * qr, axis=-1, keepdims=True)
        r = lax.rsqrt(ss * inv_qk + EPS) * qscale
        qo_ref[:, lo:lo + V7X_LANES] = (qn * r * gq_ref[:, :V7X_LANES]).astype(bf16)
        qo_ref[:, lo + V7X_LANES:lo + MLA_QPAD] = (_rope128(qr * gq_ref[:, V7X_LANES:], cos, sin, 16) * r).astype(bf16)
    c = ckv_ref[...].astype(f32)
    cn = c * lax.rsqrt(jnp.mean(c * c, axis=-1, keepdims=True) + EPS) * kvg_ref[...]
    kv = jnp.dot(cn.astype(bf16), wkvb_ref[...], preferred_element_type=f32)
    kr = kr_ref[...].astype(f32)
    ss_r = jnp.sum(kr * kr, axis=-1, keepdims=True)
    kr_rot = _rope128(kr * gkr_ref[...], cos, sin, 16)
    nv = MLA_HEADS * MLA_NOPE
    for h in range(MLA_HEADS):
        kn = kv[:, h * MLA_NOPE:(h + 1) * MLA_NOPE]
        r = lax.rsqrt((jnp.sum(kn * kn, axis=-1, keepdims=True) + ss_r) * inv_qk + EPS)
        lo = h * MLA_QPAD
        ko_ref[:, lo:lo + V7X_LANES] = (kn * r * gkn_ref[...]).astype(bf16)
        ko_ref[:, lo + V7X_LANES:lo + MLA_QPAD] = (kr_rot * r).astype(bf16)
    vo_ref[...] = kv[:, nv:].astype(bf16)


def mla_prep(p, kv_g, wkvb_p, gq_pad, gk_nope, gk_rope, cos_c, sin_c, tm):
    t = p.shape[0]
    qw = MLA_HEADS * MLA_QPAD
    vw = MLA_HEADS * MLA_V
    tab = pl.BlockSpec((tm, V7X_LANES), lambda i: (i, 0))

    def vec(n):
        return pl.BlockSpec((1, n), lambda i: (0, 0))

    return pl.pallas_call(
        _mla_prep_kernel, grid=(t // tm,),
        in_specs=[pl.BlockSpec((tm, qw), lambda i: (i, QC_OFF // qw)),
                  pl.BlockSpec((tm, MLA_RANK), lambda i: (i, CKV_OFF // MLA_RANK)),
                  pl.BlockSpec((tm, V7X_LANES), lambda i: (i, KR_OFF // V7X_LANES)),
                  vec(MLA_RANK), pl.BlockSpec((MLA_RANK, 2 * vw), lambda i: (0, 0)),
                  vec(MLA_QPAD), vec(V7X_LANES), vec(V7X_LANES), tab, tab],
        out_specs=[pl.BlockSpec((tm, qw), lambda i: (i, 0)), pl.BlockSpec((tm, qw), lambda i: (i, 0)),
                   pl.BlockSpec((tm, vw), lambda i: (i, 0))],
        out_shape=[jax.ShapeDtypeStruct((t, qw), bf16), jax.ShapeDtypeStruct((t, qw), bf16),
                   jax.ShapeDtypeStruct((t, vw), bf16)],
        compiler_params=_cparams(("arbitrary",), 48), name="mla_prep",
    )(p, p, p, kv_g.reshape(1, -1), wkvb_p, gq_pad.reshape(1, -1), gk_nope.reshape(1, -1),
      gk_rope.reshape(1, -1), cos_c, sin_c)


FLASH_COLS = 256


def _scores(k, qt_ref, s_ref):
    s_ref[0:k.shape[0], :] = jnp.dot(k, qt_ref[...], preferred_element_type=f32)


def _softmax_pv(s_ref, tk, v, m_ref, l_ref, acc_ref):
    vt = v.T
    step = min(FLASH_COLS, s_ref.shape[1])
    for c0 in range(0, s_ref.shape[1], step):
        cols = slice(c0, c0 + step)
        s = s_ref[0:tk, cols]
        m_prev = m_ref[:, cols]
        m_new = jnp.maximum(m_prev, jnp.max(s, axis=0, keepdims=True))
        alpha = jnp.exp2(m_prev - m_new)
        pr = jnp.exp2(s - m_new)
        l_ref[:, cols] = alpha * l_ref[:, cols] + jnp.sum(pr, axis=0, keepdims=True)
        acc_ref[:, cols] = alpha * acc_ref[:, cols] + jnp.dot(vt, pr.astype(bf16), preferred_element_type=f32)
        m_ref[:, cols] = m_new


def _flash_kernel(*refs, group, dq, dv, tq, tk, n_lat):
    if n_lat:
        q_ref, kl_ref, vl_ref, kc_ref, vc_ref, o_ref, qt_ref, m_ref, l_ref, acc_ref, s0_ref, s1_ref = refs
    else:
        q_ref, kc_ref, vc_ref, o_ref, qt_ref, m_ref, l_ref, acc_ref, s0_ref, s1_ref = refs
    for g in range(group):
        qt_ref[:, g * tq:(g + 1) * tq] = q_ref[:, g * dq:(g + 1) * dq].T
    m_ref[...] = jnp.full(m_ref.shape, -1e30, f32)
    l_ref[...] = jnp.zeros(l_ref.shape, f32)
    acc_ref[...] = jnp.zeros(acc_ref.shape, f32)
    n_ctx = kc_ref.shape[0]
    _scores(kc_ref[...], qt_ref, s1_ref)
    if n_lat:
        def kchunk(j):
            return kl_ref[j * tk:(j + 1) * tk, :]

        def vchunk(j):
            return vl_ref[j * tk:(j + 1) * tk, :]

        assert n_lat % 2 == 0
        _scores(kchunk(0), qt_ref, s0_ref)
        _softmax_pv(s1_ref, n_ctx, vc_ref[...], m_ref, l_ref, acc_ref)
        for i in range(n_lat // 2 - 1):
            _scores(kchunk(2 * i + 1), qt_ref, s1_ref)
            _softmax_pv(s0_ref, tk, vchunk(2 * i), m_ref, l_ref, acc_ref)
            _scores(kchunk(2 * i + 2), qt_ref, s0_ref)
            _softmax_pv(s1_ref, tk, vchunk(2 * i + 1), m_ref, l_ref, acc_ref)
        _scores(kchunk(n_lat - 1), qt_ref, s1_ref)
        _softmax_pv(s0_ref, tk, vchunk(n_lat - 2), m_ref, l_ref, acc_ref)
        _softmax_pv(s1_ref, tk, vchunk(n_lat - 1), m_ref, l_ref, acc_ref)
    else:
        _softmax_pv(s1_ref, n_ctx, vc_ref[...], m_ref, l_ref, acc_ref)
    for g in range(group):
        cols = slice(g * tq, (g + 1) * tq)
        o_ref[:, g * dv:(g + 1) * dv] = (acc_ref[:, cols] / l_ref[:, cols]).T.astype(o_ref.dtype)


def flash_attention(q, k, v, v_col_off, *, batch, seq, ctx_len, kv_heads, group, dq, dv, tq, tk, lat):
    ctx_blk0 = batch * seq // ctx_len
    voff = v_col_off // dv
    rows = group * tq
    kc_spec = pl.BlockSpec((ctx_len, dq), lambda b, h, i: (ctx_blk0 + b, h))
    vc_spec = pl.BlockSpec((ctx_len, dv), lambda b, h, i: (ctx_blk0 + b, voff + h))
    if lat:
        nq = seq // tq
        in_specs = [pl.BlockSpec((tq, group * dq), lambda b, h, i: (b * nq + i, h)),
                    pl.BlockSpec((seq, dq), lambda b, h, i: (b, h)),
                    pl.BlockSpec((seq, dv), lambda b, h, i: (b, voff + h)), kc_spec, vc_spec]
        args = [q, k, v, k, v]
        out_spec = pl.BlockSpec((tq, group * dv), lambda b, h, i: (b * nq + i, h))
        out_rows = batch * seq
        n_lat = seq // tk
    else:
        nq = 1
        assert tq == ctx_len
        in_specs = [pl.BlockSpec((tq, group * dq), lambda b, h, i: (ctx_blk0 + b, h)), kc_spec, vc_spec]
        args = [q, k, v]
        out_spec = pl.BlockSpec((tq, group * dv), lambda b, h, i: (b, h))
        out_rows = batch * ctx_len
        n_lat = 0
    return pl.pallas_call(
        functools.partial(_flash_kernel, group=group, dq=dq, dv=dv, tq=tq, tk=tk, n_lat=n_lat),
        grid=(batch, kv_heads, nq), in_specs=in_specs, out_specs=out_spec,
        out_shape=jax.ShapeDtypeStruct((out_rows, kv_heads * group * dv), bf16),
        scratch_shapes=[pltpu.VMEM((dq, rows), bf16), pltpu.VMEM((1, rows), f32), pltpu.VMEM((1, rows), f32),
                        pltpu.VMEM((dv, rows), f32)] + [pltpu.VMEM((max(tk, ctx_len) if lat else ctx_len, rows), f32)] * 2,
        compiler_params=_cparams(("arbitrary", "arbitrary", "arbitrary"), 48),
        name="flash_lat" if lat else "flash_ctx",
    )(*args)


CONV_HALO = 16


def _conv3_kernel(prev_ref, cur_ref, next_ref, w_ref, b_ref, o_ref, *, blocks_per_seq):
    rows = cur_ref.shape[0]
    pos = pl.program_id(0) % blocks_per_seq
    x = cur_ref[...].astype(f32)
    row = lax.broadcasted_iota(jnp.int32, x.shape, 0)
    prev_row = jnp.where(pos == 0, 0.0, prev_ref[...].astype(f32)[CONV_HALO - 1:CONV_HALO, :])
    next_row = jnp.where(pos == blocks_per_seq - 1, 0.0, next_ref[...].astype(f32)[0:1, :])
    xm = jnp.where(row == 0, prev_row, pltpu.roll(x, 1, 0))
    xp = jnp.where(row == rows - 1, next_row, pltpu.roll(x, rows - 1, 0))
    o_ref[...] = xm * w_ref[0:1, :] + x * w_ref[1:2, :] + xp * w_ref[2:3, :] + b_ref[...]


def short_conv(p, conv_w, conv_b, row0, nrows, seq_len, rows, cw):
    width = 3 * HY_C
    rb0 = row0 // rows
    sub = rows // CONV_HALO
    last_halo = p.shape[0] // CONV_HALO - 1
    c0 = HY_OFF // cw
    return pl.pallas_call(
        functools.partial(_conv3_kernel, blocks_per_seq=seq_len // rows),
        grid=(nrows // rows, width // cw),
        in_specs=[
            pl.BlockSpec((CONV_HALO, cw), lambda r, c: (jnp.maximum((rb0 + r) * sub - 1, 0), c0 + c)),
            pl.BlockSpec((rows, cw), lambda r, c: (rb0 + r, c0 + c)),
            pl.BlockSpec((CONV_HALO, cw), lambda r, c: (jnp.minimum((rb0 + r + 1) * sub, last_halo), c0 + c)),
            pl.BlockSpec((3, cw), lambda r, c: (0, c)),
            pl.BlockSpec((1, cw), lambda r, c: (0, c)),
        ],
        out_specs=pl.BlockSpec((rows, cw), lambda r, c: (r, c)),
        out_shape=jax.ShapeDtypeStruct((nrows, width), f32),
        compiler_params=_cparams(("arbitrary", "arbitrary"), 32), name="short_conv",
    )(p, p, p, conv_w, conv_b.reshape(1, width))


@functools.lru_cache(maxsize=None)
def _dft_tables(length):
    nb = 128 if length >= 1024 else 16
    n_fft = 2 * length
    na = n_fft // nb
    nah = na // 2
    lo = np.arange(nb, dtype=np.int64)[:, None, None]
    k1 = np.arange(na, dtype=np.int64)[None, :, None]
    hi = np.arange(nah, dtype=np.int64)[None, None, :]
    ang = 2.0 * np.pi * (((nb * hi + lo) * k1) % n_fft) / n_fft
    c, s = np.cos(ang), np.sin(ang)
    a1 = np.concatenate([np.concatenate([c, s], axis=2), np.concatenate([-s, c], axis=2)], axis=1)
    ct, st = np.swapaxes(c, 1, 2) / n_fft, np.swapaxes(s, 1, 2) / n_fft
    a3 = np.concatenate([np.concatenate([ct, -st], axis=2), np.concatenate([st, ct], axis=2)], axis=1)
    kk = np.arange(nb, dtype=np.int64)
    angb = 2.0 * np.pi * ((kk[:, None] * kk[None, :]) % nb) / nb
    cb, sb = np.cos(angb), np.sin(angb)
    mf = np.block([[cb, sb], [-sb, cb]])
    mfc = np.block([[cb, -sb], [sb, cb]])
    return dict(nb=nb, na=na, nah=nah,
                a1=jnp.asarray(a1, dtype=bf16), a1r=jnp.asarray(a1[:, :, :nah], dtype=bf16),
                a3=jnp.asarray(a3, dtype=bf16), mf=jnp.asarray(mf, dtype=bf16), mfc=jnp.asarray(mfc, dtype=bf16))


def _tap_times(j, nb, nah):
    ridx = lax.broadcasted_iota(jnp.int32, (V7X_SUBLANES * nah, 1), 0)
    return nb * (ridx & (nah - 1)) + (j * V7X_SUBLANES + (ridx >> (nah.bit_length() - 1)))


def _filter_mlp_kernel(band_ref, w1_ref, b1_ref, w2_ref, b2_ref, freq_ref, o_ref, *, length, nb, nah):
    hp = lax.Precision.HIGHEST
    t = _tap_times(pl.program_id(0), nb, nah).astype(f32)
    t_unit = t / float(max(length - 1, 1))
    lane = lax.broadcasted_iota(jnp.int32, (t.shape[0], V7X_LANES), 1)
    ang = ((2.0 * math.pi / length) * t) * band_ref[...]
    feats = jnp.where(lane == 0, t_unit,
                      jnp.where(lane <= HY_BANDS, jnp.cos(ang), jnp.where(lane <= 2 * HY_BANDS, -jnp.sin(ang), 0.0)))
    h = jnp.sin(freq_ref[0:1, :] * (jnp.dot(feats, w1_ref[...], precision=hp, preferred_element_type=f32) + b1_ref[...]))
    o_ref[...] = jnp.sin(freq_ref[1:2, :] * (jnp.dot(h, w2_ref[...], precision=hp, preferred_element_type=f32)
                                            + b2_ref[...]))


def filter_mlp(tabs, length, band_row, w1p, b1, w2, b2, freq):
    nb, nah = tabs["nb"], tabs["nah"]
    rows = V7X_SUBLANES * nah
    full = lambda *shape: pl.BlockSpec(shape, lambda j: (0,) * len(shape))
    return pl.pallas_call(
        functools.partial(_filter_mlp_kernel, length=length, nb=nb, nah=nah), grid=(nb // V7X_SUBLANES,),
        in_specs=[full(1, V7X_LANES), full(V7X_LANES, HY_FW), full(1, HY_FW), full(HY_FW, HY_FW), full(1, HY_FW),
                  full(2, HY_FW)],
        out_specs=pl.BlockSpec((rows, HY_FW), lambda j: (j, 0)),
        out_shape=jax.ShapeDtypeStruct((length, HY_FW), f32),
        compiler_params=_cparams(("arbitrary",), 32), name="hy_filter_mlp",
    )(band_row, w1p, b1, w2, b2, freq)


def _filter_stage1_kernel(a1r_ref, h_ref, w3_ref, delta_ref, o_ref, ss_ref, *, length, nb, nah):
    j = pl.program_id(1)
    t_int = _tap_times(j, nb, nah)
    t_unit = t_int.astype(f32) / float(max(length - 1, 1))
    decay = jnp.exp(-t_unit * delta_ref[...])
    hb = h_ref[...].astype(bf16)

    @pl.when(j == 0)
    def _():
        ss_ref[...] = jnp.zeros(ss_ref.shape, f32)

    na = 2 * nah
    for o in range(HY_ORDER):
        fwd = jnp.dot(hb, w3_ref[2 * o].astype(bf16), preferred_element_type=f32) * decay
        bwd = jnp.dot(hb, w3_ref[2 * o + 1].astype(bf16), preferred_element_type=f32) * decay
        bwd = jnp.where(t_int == 0, 0.0, bwd)
        ss_ref[o:o + 1, :] += jnp.sum(fwd * fwd + bwd * bwd, axis=0, keepdims=True)
        sb, db = (fwd + bwd).astype(bf16), (fwd - bwd).astype(bf16)
        for l in range(V7X_SUBLANES):
            a = a1r_ref[l]
            fs = jnp.dot(a, sb[l * nah:(l + 1) * nah], preferred_element_type=f32)
            fd = jnp.dot(a, db[l * nah:(l + 1) * nah], preferred_element_type=f32)
            o_ref[o, 0, l] = _pack_complex(fs[:na], fs[na:])
            o_ref[o, 1, l] = _pack_complex(fd[:na], fd[na:])


def filter_stage1(tabs, length, hmlp, w3r, delta_row):
    nb, na, nah = tabs["nb"], tabs["na"], tabs["nah"]
    ncb = HY_C // V7X_LANES
    return pl.pallas_call(
        functools.partial(_filter_stage1_kernel, length=length, nb=nb, nah=nah),
        grid=(ncb, nb // V7X_SUBLANES),
        in_specs=[pl.BlockSpec((V7X_SUBLANES, 2 * na, nah), lambda c, j: (j, 0, 0)),
                  pl.BlockSpec((V7X_SUBLANES * nah, HY_FW), lambda c, j: (j, 0)),
                  pl.BlockSpec((2 * HY_ORDER, HY_FW, V7X_LANES), lambda c, j: (0, 0, c)),
                  pl.BlockSpec((1, V7X_LANES), lambda c, j: (0, c))],
        out_specs=[pl.BlockSpec((HY_ORDER, 2, V7X_SUBLANES, na, V7X_LANES), lambda c, j: (0, 0, j, 0, c)),
                   pl.BlockSpec((V7X_SUBLANES, V7X_LANES), lambda c, j: (0, c))],
        out_shape=[jax.ShapeDtypeStruct((HY_ORDER, 2, nb, na, HY_C), u32),
                   jax.ShapeDtypeStruct((V7X_SUBLANES, HY_C), f32)],
        compiler_params=_cparams(("arbitrary", "arbitrary"), 48), name="hy_filter_s1",
    )(tabs["a1r"], hmlp, w3r, delta_row)


def _tile_rows(ref, l):
    tiles = math.prod(ref.shape[:-2])
    flat = ref.reshape(tiles * V7X_SUBLANES, ref.shape[-1])
    return flat.at[pl.ds(l, tiles, stride=V7X_SUBLANES), :]


def _pack_complex(re, im):
    def rne(x):
        b = pltpu.bitcast(x, u32)
        return b + (jnp.uint32(0x7FFF) + ((b >> 16) & jnp.uint32(1)))
    return (rne(re) & jnp.uint32(0xFFFF0000)) | (rne(im) >> 16)


def _unpack_complex(w):
    return pltpu.bitcast(w & jnp.uint32(0xFFFF0000), f32), pltpu.bitcast(w << 16, f32)


def _filter_stage2_kernel(x_ref, m_ref, o_ref, *, nb):
    for kl in range(V7X_SUBLANES):
        re, im = _unpack_complex(_tile_rows(x_ref, kl)[...])
        rs = jnp.concatenate([re[:nb], im[:nb]], axis=0).astype(bf16)
        rd = jnp.concatenate([re[nb:], im[nb:]], axis=0).astype(bf16)
        h_re = jnp.dot(m_ref[0:nb, :], rs, preferred_element_type=f32)
        h_im = jnp.dot(m_ref[nb:, :], rd, preferred_element_type=f32)
        o_ref[kl] = _pack_complex(h_re, h_im)


def filter_stage2(tabs, hin):
    nb, na = tabs["nb"], tabs["na"]
    ncb = HY_C // V7X_LANES
    return pl.pallas_call(
        functools.partial(_filter_stage2_kernel, nb=nb), grid=(HY_ORDER, ncb, na // V7X_SUBLANES),
        in_specs=[pl.BlockSpec((None, 2, nb, V7X_SUBLANES, V7X_LANES), lambda o, c, j: (o, 0, 0, j, c)),
                  pl.BlockSpec((2 * nb, 2 * nb), lambda o, c, j: (0, 0))],
        out_specs=pl.BlockSpec((None, V7X_SUBLANES, nb, V7X_LANES), lambda o, c, j: (o, j, 0, c)),
        out_shape=jax.ShapeDtypeStruct((HY_ORDER, na, nb, HY_C), u32),
        compiler_params=_cparams(("arbitrary", "arbitrary", "arbitrary"), 48), name="hy_filter_s2",
    )(hin, tabs["mf"])


def _conv_stage1_kernel(zr_ref, zi_ref, a1_ref, o_ref, *, na):
    for l in range(V7X_SUBLANES):
        rhs = jnp.concatenate([_tile_rows(zr_ref, l)[...], _tile_rows(zi_ref, l)[...]], axis=0).astype(bf16)
        y = jnp.dot(a1_ref[l], rhs, preferred_element_type=f32)
        o_ref[l] = _pack_complex(y[:na], y[na:])


def conv_stage1(tabs, z4, col_off):
    nb, na, nah = tabs["nb"], tabs["na"], tabs["nah"]
    ncb = HY_C // V7X_LANES
    c0 = col_off // V7X_LANES
    return pl.pallas_call(
        functools.partial(_conv_stage1_kernel, na=na), grid=(ncb, nb // V7X_SUBLANES),
        in_specs=[pl.BlockSpec((None, nah, V7X_SUBLANES, V7X_LANES), lambda c, j: (0, 0, j, c0 + c)),
                  pl.BlockSpec((None, nah, V7X_SUBLANES, V7X_LANES), lambda c, j: (1, 0, j, c0 + c)),
                  pl.BlockSpec((V7X_SUBLANES, 2 * na, na), lambda c, j: (j, 0, 0))],
        out_specs=pl.BlockSpec((V7X_SUBLANES, na, V7X_LANES), lambda c, j: (j, 0, c)),
        out_shape=jax.ShapeDtypeStruct((nb, na, HY_C), u32),
        compiler_params=_cparams(("arbitrary", "arbitrary"), 48), name="hy_conv_s1",
    )(z4, z4, tabs["a1"])


def _conv_stage2_kernel(x_ref, h_ref, mf_ref, mfc_ref, o_ref, *, nb):
    for kl in range(V7X_SUBLANES):
        re, im = _unpack_complex(_tile_rows(x_ref, kl)[...])
        rhs = jnp.concatenate([re, im], axis=0).astype(bf16)
        x = jnp.dot(mf_ref[...], rhs, preferred_element_type=f32)
        xr, xi = x[:nb], x[nb:]
        hr, hi = _unpack_complex(h_ref[kl])
        y = jnp.concatenate([xr * hr - xi * hi, xr * hi + xi * hr], axis=0).astype(bf16)
        e = jnp.dot(mfc_ref[...], y, preferred_element_type=f32)
        o_ref[kl] = _pack_complex(e[:nb], e[nb:])


def conv_stage2(tabs, din, hspec, order):
    nb, na = tabs["nb"], tabs["na"]
    ncb = HY_C // V7X_LANES
    return pl.pallas_call(
        functools.partial(_conv_stage2_kernel, nb=nb), grid=(ncb, na // V7X_SUBLANES),
        in_specs=[pl.BlockSpec((nb, V7X_SUBLANES, V7X_LANES), lambda c, j: (0, j, c)),
                  pl.BlockSpec((None, V7X_SUBLANES, nb, V7X_LANES), lambda c, j: (order, j, 0, c)),
                  pl.BlockSpec((2 * nb, 2 * nb), lambda c, j: (0, 0)),
                  pl.BlockSpec((2 * nb, 2 * nb), lambda c, j: (0, 0))],
        out_specs=pl.BlockSpec((V7X_SUBLANES, nb, V7X_LANES), lambda c, j: (j, 0, c)),
        out_shape=jax.ShapeDtypeStruct((na, nb, HY_C), u32),
        compiler_params=_cparams(("arbitrary", "arbitrary"), 48), name="hy_conv_s2",
    )(din, hspec, tabs["mf"], tabs["mfc"])


def _conv_stage3_kernel(e_ref, a3_ref, zr_ref, zi_ref, gr_ref, gi_ref, ss_ref, skip_ref, o_ref, *, order):
    rs = lax.rsqrt(ss_ref[order:order + 1, :] + EPS)
    skip = skip_ref[order:order + 1, :]
    for l in range(V7X_SUBLANES):
        re, im = _unpack_complex(_tile_rows(e_ref, l)[...])
        rhs = jnp.concatenate([re, im], axis=0).astype(bf16)
        y = jnp.dot(a3_ref[l], rhs, preferred_element_type=f32) * rs
        z = jnp.concatenate([_tile_rows(zr_ref, l)[...], _tile_rows(zi_ref, l)[...]], axis=0)
        g = jnp.concatenate([_tile_rows(gr_ref, l)[...], _tile_rows(gi_ref, l)[...]], axis=0)
        _tile_rows(o_ref, l)[...] = g * (y + skip * z)


def conv_stage3(tabs, ein, z4, z_off, g4, g_off, ss, skip, order):
    nb, na, nah = tabs["nb"], tabs["na"], tabs["nah"]
    ncb = HY_C // V7X_LANES
    zc, gc = z_off // V7X_LANES, g_off // V7X_LANES

    def slab(b, c0):
        return pl.BlockSpec((None, nah, V7X_SUBLANES, V7X_LANES), lambda c, j: (b, 0, j, c0 + c))

    return pl.pallas_call(
        functools.partial(_conv_stage3_kernel, order=order), grid=(ncb, nb // V7X_SUBLANES),
        in_specs=[pl.BlockSpec((na, V7X_SUBLANES, V7X_LANES), lambda c, j: (0, j, c)),
                  pl.BlockSpec((V7X_SUBLANES, na, 2 * na), lambda c, j: (j, 0, 0)),
                  slab(0, zc), slab(1, zc), slab(0, gc), slab(1, gc),
                  pl.BlockSpec((V7X_SUBLANES, V7X_LANES), lambda c, j: (0, c)),
                  pl.BlockSpec((HY_ORDER, V7X_LANES), lambda c, j: (0, c))],
        out_specs=pl.BlockSpec((2, nah, V7X_SUBLANES, V7X_LANES), lambda c, j: (0, 0, j, c)),
        out_shape=jax.ShapeDtypeStruct((2, nah, nb, HY_C), f32),
        compiler_params=_cparams(("arbitrary", "arbitrary"), 48), name="hy_conv_s3",
    )(ein, tabs["a3"], z4, z4, g4, g4, ss, skip)


def hyena_mixer(uc, length, hf, skip):
    tabs = _dft_tables(length)
    nb, nah = tabs["nb"], tabs["nah"]
    band_row, w1p, b1, w2, b2, w3r, freq, delta_row = hf
    hmlp = filter_mlp(tabs, length, band_row, w1p, b1, w2, b2, freq)
    hin, ss = filter_stage1(tabs, length, hmlp, w3r, delta_row)
    hspec = filter_stage2(tabs, hin)
    u4 = uc.reshape(2, nah, nb, 3 * HY_C)
    z4, z_off = u4, 0
    for o in range(HY_ORDER):
        din = conv_stage1(tabs, z4, z_off)
        ein = conv_stage2(tabs, din, hspec, o)
        z4 = conv_stage3(tabs, ein, z4, z_off, u4, (o + 1) * HY_C, ss, skip, o)
        z_off = 0
    return z4.reshape(2 * length, HY_C)


def _merge_kernel(gl_ref, al_ref, ac_ref, hl_ref, hc_ref, cl_ref, cc_ref, wg_ref, bg_ref, wb_ref, wo_ref,
                  o_ref, acc_ref, br_ref, *, lat_blocks):
    i, j = pl.program_id(0), pl.program_id(1)

    @pl.when(j == 0)
    def _():
        acc_ref[...] = jnp.zeros(acc_ref.shape, f32)

    @pl.when((j == 0) & (i < lat_blocks))
    def _():
        br_ref[0] = al_ref[...]
        br_ref[1] = hl_ref[...].astype(bf16)
        br_ref[2] = cl_ref[...]

    @pl.when((j == 0) & (i >= lat_blocks))
    def _():
        br_ref[0] = ac_ref[...]
        br_ref[1] = hc_ref[...].astype(bf16)
        br_ref[2] = cc_ref[...]

    gl = gl_ref[...]
    y = None
    for n in range(N_BRANCH):
        z = jnp.dot(gl, wg_ref[n], preferred_element_type=f32) + bg_ref[n]
        g = 1.0 / (1.0 + jnp.exp(-z))
        term = g * jnp.dot(br_ref[n], wb_ref[n], preferred_element_type=f32)
        y = term if y is None else y + term
    acc_ref[...] += jnp.dot(y.astype(bf16), wo_ref[...], preferred_element_type=f32)

    @pl.when(j == pl.num_programs(1) - 1)
    def _():
        o_ref[...] = acc_ref[...].astype(o_ref.dtype)


def merge_out(p, a, hy, c, wg, bg, wb, wo, layer, tm, tj):
    t = p.shape[0]
    d = wo.shape[2]
    nl = a[0].shape[0] // tm
    assert a[1].shape[0] % tm == 0 and t == a[0].shape[0] + a[1].shape[0]
    lat = pl.BlockSpec((tm, BRANCH_W), lambda i, j: (jnp.minimum(i, nl - 1), 0))
    ctx = pl.BlockSpec((tm, BRANCH_W), lambda i, j: (jnp.maximum(i - nl, 0), 0))
    return pl.pallas_call(
        functools.partial(_merge_kernel, lat_blocks=nl), grid=(t // tm, d // tj),
        in_specs=[pl.BlockSpec((tm, GATE_RANK), lambda i, j: (i, GATE_OFF // GATE_RANK)),
                  lat, ctx, lat, ctx, lat, ctx,
                  pl.BlockSpec((None, N_BRANCH, GATE_RANK, tj), lambda i, j: (layer, 0, 0, j)),
                  pl.BlockSpec((None, N_BRANCH, 1, tj), lambda i, j: (layer, 0, 0, j)),
                  pl.BlockSpec((None, N_BRANCH, BRANCH_W, tj), lambda i, j: (layer, 0, 0, j)),
                  pl.BlockSpec((None, tj, d), lambda i, j: (layer, j, 0))],
        out_specs=pl.BlockSpec((tm, d), lambda i, j: (i, 0)),
        out_shape=jax.ShapeDtypeStruct((t, d), bf16),
        scratch_shapes=[pltpu.VMEM((tm, d), f32), pltpu.VMEM((N_BRANCH, tm, BRANCH_W), bf16)],
        compiler_params=_cparams(("arbitrary", "arbitrary"), 58), name="merge_out",
    )(p, a[0], a[1], hy[0], hy[1], c[0], c[1], wg, bg, wb, wo)


def _ffn_kernel(h_ref, wg_ref, wu_ref, wd_ref, o_ref, acc_ref):
    j = pl.program_id(1)

    @pl.when(j == 0)
    def _():
        acc_ref[...] = jnp.zeros(acc_ref.shape, f32)

    h = h_ref[...]
    g = jnp.dot(h, wg_ref[...], preferred_element_type=f32)
    u = jnp.dot(h, wu_ref[...], preferred_element_type=f32)
    a = (g * (1.0 / (1.0 + jnp.exp(-g))) * u).astype(bf16)
    acc_ref[...] += jnp.dot(a, wd_ref[...], preferred_element_type=f32)

    @pl.when(j == pl.num_programs(1) - 1)
    def _():
        o_ref[...] = acc_ref[...].astype(o_ref.dtype)


def ffn(h, w_gu, w_down, layer, tm, th):
    t, d = h.shape
    hidden = w_down.shape[1]
    nh = hidden // th
    return pl.pallas_call(
        _ffn_kernel, grid=(t // tm, nh),
        in_specs=[pl.BlockSpec((tm, d), lambda i, j: (i, 0)),
                  pl.BlockSpec((None, d, th), lambda i, j: (layer, 0, j)),
                  pl.BlockSpec((None, d, th), lambda i, j: (layer, 0, nh + j)),
                  pl.BlockSpec((None, th, d), lambda i, j: (layer, j, 0))],
        out_specs=pl.BlockSpec((tm, d), lambda i, j: (i, 0)),
        out_shape=jax.ShapeDtypeStruct((t, d), bf16),
        scratch_shapes=[pltpu.VMEM((tm, d), f32)],
        compiler_params=_cparams(("arbitrary", "arbitrary"), 56), name="ffn",
    )(h, w_gu, w_gu, w_down)


def _pack_w_in(w_in):
    depth, d, _ = w_in.shape
    o_qa, o_ka, o_hy = 0, GQA_HEADS * HEAD_DIM, (GQA_HEADS + 2 * GQA_KV_HEADS) * HEAD_DIM
    o_qc = o_hy + 3 * HY_C
    o_ckv = o_qc + MLA_HEADS * MLA_QK
    o_kr = o_ckv + MLA_RANK
    o_gate = o_kr + MLA_ROPE
    qc = w_in[:, :, o_qc:o_ckv].reshape(depth, d, MLA_HEADS, MLA_QK)
    qc = jnp.pad(qc, ((0, 0), (0, 0), (0, 0), (0, MLA_QPAD - MLA_QK))).reshape(depth, d, MLA_HEADS * MLA_QPAD)
    tail = NP_COLS - KR_OFF - MLA_ROPE
    parts = [qc, w_in[:, :, o_qa:o_ka], w_in[:, :, o_ka:o_hy], w_in[:, :, o_hy:o_qc], w_in[:, :, o_ckv:o_kr],
             w_in[:, :, o_gate:o_gate + GATE_RANK], w_in[:, :, o_kr:o_gate], jnp.zeros((depth, d, tail), w_in.dtype)]
    return jnp.concatenate(parts, axis=2).astype(bf16)


def _pack_w_kvb(w):
    depth = w.shape[0]
    w = w.reshape(depth, MLA_RANK, MLA_HEADS, 2, MLA_NOPE)
    return jnp.swapaxes(w, 2, 3).reshape(depth, MLA_RANK, 2 * MLA_HEADS * MLA_NOPE).astype(bf16)


def kernel(x, c, ctx, c_ctx, norm1_g, norm2_g, ada_down, ada_up, ada_b, w_in, gqa_q_norm, gqa_k_norm, hy_conv_w,
           hy_conv_b, hf_w1, hf_b1, hf_w2, hf_b2, hf_w3, hf_freq, hy_skip, mla_kv_norm, mla_w_kvb, mla_q_norm,
           mla_k_norm, w_gate_up, b_gate, w_branch, w_out, ffn_w_gu, ffn_w_down):
    batch, seq, d = x.shape
    ctx_len = ctx.shape[1]
    depth = w_in.shape[0]
    hidden = ffn_w_down.shape[1]
    assert batch == 2, "the Hyena long convolution packs the two batches as one complex sequence"
    assert w_in.shape[2] == IN_COLS and seq % GRID_W == 0
    n_lat, n_ctx = batch * seq, batch * ctx_len
    t_all = n_lat + n_ctx

    tm = _row_tile(seq, n_ctx, (512, 256, 128))
    tm_ew = _row_tile(seq, n_ctx, (256, 128))
    tm_mm = 768 if t_all % 768 == 0 else tm
    tm_in, tn_in = (1536, 512) if t_all % 1536 == 0 else (tm_mm, 1536)
    tj = 512
    th = 256
    tq_a = min(256, ctx_len)
    tq_c = min(1024, seq)
    tk = min(512, seq)

    def grp(tile):
        per = seq // tile
        return lambda i: jnp.minimum(i // per, batch)

    w_in_p = _pack_w_in(w_in)
    w_kvb_p = _pack_w_kvb(mla_w_kvb)
    wg_b, wb_b, wo_b = w_gate_up.astype(bf16), w_branch.astype(bf16), w_out.astype(bf16)
    wgu_b, wd_b = ffn_w_gu.astype(bf16), ffn_w_down.astype(bf16)
    bg = b_gate.reshape(depth, N_BRANCH, 1, d)
    gq_pad = jnp.pad(mla_q_norm, ((0, 0), (0, MLA_QPAD - MLA_QK)))
    gk_nope = mla_k_norm[:, :MLA_NOPE]
    gk_rope = jnp.pad(mla_k_norm[:, MLA_NOPE:], ((0, 0), (0, V7X_LANES - MLA_ROPE)))
    w1p = jnp.pad(hf_w1, ((0, 0), (0, V7X_LANES - HY_EMB), (0, 0)))
    w3r = jnp.swapaxes(hf_w3.reshape(depth, HY_FW, 2 * HY_ORDER, HY_C), 1, 2)
    bands = jnp.linspace(1e-4, HY_BANDS - 1, HY_BANDS, dtype=f32)
    band_row = jnp.concatenate([jnp.zeros((1,), f32), bands, bands,
                                jnp.zeros((V7X_LANES - HY_EMB,), f32)]).reshape(1, V7X_LANES)
    delta_row = jnp.abs(jnp.linspace(math.log(HY_DECAY_TARGET) / HY_DECAY_PCT_MIN,
                                     math.log(HY_DECAY_TARGET) / HY_DECAY_PCT_MAX, HY_C, dtype=f32)).reshape(1, HY_C)

    xs = jnp.concatenate([x.reshape(n_lat, d), ctx.reshape(n_ctx, d)], axis=0)
    tpos = jnp.arange(seq, dtype=jnp.int32)
    zpad = jnp.zeros((n_ctx,), f32)
    pos_row = jnp.concatenate([jnp.tile((tpos // GRID_W).astype(f32), batch), zpad]).reshape(t_all, 1)
    pos_col = jnp.concatenate([jnp.tile((tpos % GRID_W).astype(f32), batch), zpad]).reshape(t_all, 1)
    cond8 = jnp.concatenate([c, c_ctx[None, :], jnp.zeros((8 - batch - 1, d), f32)], axis=0)

    mod = ada_modulation(cond8, ada_down, ada_up, ada_b)
    mod5 = mod.reshape(depth, 8, N_MOD, 1, d)
    cos_a, sin_a, cos_c, sin_c = rope_tables(pos_row, pos_col, tm_ew)

    delta = None
    for l in range(depth):
        if l == 0:
            (h,) = prenorm(xs, mod5, l, grp(tm_ew), tm_ew, norm_g=norm1_g[l], shift_idx=0, scale_idx=1)
        else:
            xs, h = prenorm(xs, mod5, l, grp(tm_ew), tm_ew, delta=delta, gate_idx=5, gate_layer=l - 1,
                            norm_g=norm1_g[l], shift_idx=0, scale_idx=1)
        p = matmul(h, w_in_p, l, tm_in, tn_in)

        qa, ka = gqa_prep(p, gqa_q_norm[l], gqa_k_norm[l], cos_a, sin_a, tm_ew)
        fa = dict(batch=batch, seq=seq, ctx_len=ctx_len, kv_heads=GQA_KV_HEADS, group=GQA_GROUP,
                  dq=HEAD_DIM, dv=HEAD_DIM, tk=tk)
        att_a = (flash_attention(qa, ka, p, VA_OFF, tq=tq_a, lat=True, **fa),
                 flash_attention(qa, ka, p, VA_OFF, tq=ctx_len, lat=False, **fa))

        qc, kc, vc = mla_prep(p, mla_kv_norm[l], w_kvb_p[l], gq_pad[l], gk_nope[l], gk_rope[l], cos_c, sin_c, tm_ew)
        fc = dict(batch=batch, seq=seq, ctx_len=ctx_len, kv_heads=MLA_HEADS, group=1, dq=MLA_QPAD, dv=MLA_V, tk=tk)
        att_c = (flash_attention(qc, kc, vc, 0, tq=tq_c, lat=True, **fc),
                 flash_attention(qc, kc, vc, 0, tq=ctx_len, lat=False, **fc))

        hf = (band_row, w1p[l], hf_b1[l].reshape(1, -1), hf_w2[l], hf_b2[l].reshape(1, -1), w3r[l], hf_freq[l],
              delta_row)
        uc_lat = short_conv(p, hy_conv_w[l], hy_conv_b[l], 0, n_lat, seq, min(512, seq), 512)
        uc_ctx = short_conv(p, hy_conv_w[l], hy_conv_b[l], n_lat, n_ctx, ctx_len, min(256, ctx_len), 512)
        hy = (hyena_mixer(uc_lat, seq, hf, hy_skip[l]), hyena_mixer(uc_ctx, ctx_len, hf, hy_skip[l]))

        delta = merge_out(p, att_a, hy, att_c, wg_b, bg, wb_b, wo_b, l, tm, tj)
        xs, h2 = prenorm(xs, mod5, l, grp(tm_ew), tm_ew, delta=delta, gate_idx=2, gate_layer=l,
                         norm_g=norm2_g[l], shift_idx=3, scale_idx=4)
        delta = ffn(h2, wgu_b, wd_b, l, tm_mm, th)

    (out,) = prenorm(xs, mod5, depth - 1, grp(tm_ew), tm_ew, delta=delta, gate_idx=5, gate_layer=depth - 1,
                     rows=n_lat)
    return out.reshape(batch, seq, d)
```

```python
import functools
import math

import jax
import jax.numpy as jnp
import numpy as np
from jax import lax
from jax.experimental import pallas as pl
from jax.experimental.pallas import tpu as pltpu

f32 = jnp.float32
bf16 = jnp.bfloat16
u32 = jnp.uint32

GRID_W = 64
ROPE_THETA = 10000.0
EPS = 1e-6
HEAD_DIM = 128
GQA_HEADS = 8
GQA_KV_HEADS = 2
GQA_GROUP = GQA_HEADS // GQA_KV_HEADS
HY_C = 1024
HY_ORDER = 2
HY_EMB = 33
HY_BANDS = (HY_EMB - 1) // 2
HY_FW = 64
HY_DECAY_TARGET = 1e-2
HY_DECAY_PCT_MIN = 0.3
HY_DECAY_PCT_MAX = 1.5
MLA_HEADS = 8
MLA_NOPE = 128
MLA_ROPE = 64
MLA_QK = MLA_NOPE + MLA_ROPE
MLA_V = 128
MLA_RANK = 512
N_BRANCH = 3
BRANCH_W = 1024
GATE_RANK = 256
ADA_RANK = 256
N_MOD = 6
IN_COLS = 6976

V7X_LANES = 128
V7X_SUBLANES = 8
V7X_VMEM_BYTES = 64 * 1024 * 1024
MIB = 1024 * 1024

MLA_QPAD = 2 * V7X_LANES
QC_OFF = 0
QA_OFF = QC_OFF + MLA_HEADS * MLA_QPAD
KA_OFF = QA_OFF + GQA_HEADS * HEAD_DIM
VA_OFF = KA_OFF + GQA_KV_HEADS * HEAD_DIM
HY_OFF = VA_OFF + GQA_KV_HEADS * HEAD_DIM
CKV_OFF = HY_OFF + 3 * HY_C
GATE_OFF = CKV_OFF + MLA_RANK
KR_OFF = GATE_OFF + GATE_RANK
NP_COLS = 7680

LOG2E = math.log2(math.e)


def _cparams(sem, vmem_mib):
    return pltpu.CompilerParams(dimension_semantics=sem, vmem_limit_bytes=int(vmem_mib * MIB))


def _row_tile(s, nctx, cands):
    for t in cands:
        if s % t == 0 and nctx % t == 0:
            return t
    raise ValueError("no row tile fits")


def _ada_kernel(cond_ref, down_ref, up_ref, b_ref, o_ref, t_ref):
    @pl.when(pl.program_id(1) == 0)
    def _():
        c = cond_ref[...]
        c = c * (1.0 / (1.0 + jnp.exp(-c)))
        t_ref[...] = jnp.dot(c, down_ref[...], precision=lax.Precision.HIGHEST, preferred_element_type=f32)

    o_ref[...] = jnp.dot(t_ref[...], up_ref[...], precision=lax.Precision.HIGHEST,
                         preferred_element_type=f32) + b_ref[...]


def ada_modulation(cond8, ada_down, ada_up, ada_b):
    depth, d, _ = ada_down.shape
    n = ada_up.shape[2]
    tn = 2048 if n % 2048 == 0 else 512
    return pl.pallas_call(
        _ada_kernel,
        grid=(depth, n // tn),
        in_specs=[
            pl.BlockSpec((8, d), lambda l, j: (0, 0)),
            pl.BlockSpec((None, d, ADA_RANK), lambda l, j: (l, 0, 0)),
            pl.BlockSpec((None, ADA_RANK, tn), lambda l, j: (l, 0, j)),
            pl.BlockSpec((None, 1, tn), lambda l, j: (l, 0, j)),
        ],
        out_specs=pl.BlockSpec((None, 8, tn), lambda l, j: (l, 0, j)),
        out_shape=jax.ShapeDtypeStruct((depth, 8, n), f32),
        scratch_shapes=[pltpu.VMEM((8, ADA_RANK), f32)],
        compiler_params=_cparams(("arbitrary", "arbitrary"), 40),
        name="ada_modulation",
    )(cond8, ada_down, ada_up, ada_b.reshape(depth, 1, n))


def _rope_table_kernel(pr_ref, pc_ref, ca_ref, sa_ref, cc_ref, sc_ref):
    shape = ca_ref.shape
    lane = lax.broadcasted_iota(jnp.int32, shape, 1)
    pr = jnp.broadcast_to(pr_ref[...], shape)
    pc = jnp.broadcast_to(pc_ref[...], shape)
    log_theta = math.log(ROPE_THETA)
    fa = jnp.exp((lane & 31).astype(f32) * (-log_theta / 32.0))
    ang = jnp.where(lane < 64, pr, pc) * fa
    ca_ref[...] = jnp.cos(ang)
    sa_ref[...] = jnp.where((lane & 63) < 32, -1.0, 1.0) * jnp.sin(ang)
    fc = jnp.exp((lane & 15).astype(f32) * (-log_theta / 16.0))
    angc = jnp.where(lane < 64, jnp.where(lane < 32, pr, pc) * fc, 0.0)
    cc_ref[...] = jnp.cos(angc)
    sc_ref[...] = jnp.where((lane & 31) < 16, -1.0, 1.0) * jnp.sin(angc)


def rope_tables(pos_row, pos_col, tm):
    t = pos_row.shape[0]
    spec1 = pl.BlockSpec((tm, 1), lambda i: (i, 0))
    spec = pl.BlockSpec((tm, V7X_LANES), lambda i: (i, 0))
    sh = jax.ShapeDtypeStruct((t, V7X_LANES), f32)
    return pl.pallas_call(
        _rope_table_kernel, grid=(t // tm,), in_specs=[spec1, spec1], out_specs=[spec] * 4,
        out_shape=[sh] * 4, compiler_params=_cparams(("arbitrary",), 32), name="rope_tables",
    )(pos_row, pos_col)


def _prenorm_kernel(*refs, has_delta, want_h):
    if has_delta:
        x_ref, d_ref, gate_ref = refs[:3]
        rest = refs[3:]
    else:
        x_ref = refs[0]
        rest = refs[1:]
    x = x_ref[...]
    if has_delta:
        x = x + gate_ref[...] * d_ref[...].astype(f32)
    if want_h:
        g_ref, shift_ref, scale_ref = rest[:3]
        outs = rest[3:]
    else:
        outs = rest
    k = 0
    if has_delta:
        outs[k][...] = x
        k += 1
    if want_h:
        y = x * lax.rsqrt(jnp.mean(x * x, axis=-1, keepdims=True) + EPS)
        y = y * g_ref[...]
        outs[k][...] = (y * (1.0 + scale_ref[...]) + shift_ref[...]).astype(bf16)


def prenorm(x, mod5, layer, grp_of_block, tm, *, delta=None, gate_idx=None, gate_layer=None,
            norm_g=None, shift_idx=None, scale_idx=None, rows=None):
    t, d = x.shape
    rows = t if rows is None else rows
    has_delta = delta is not None
    want_h = norm_g is not None
    row_spec = pl.BlockSpec((tm, d), lambda i: (i, 0))

    def mod_spec(lyr, which):
        return pl.BlockSpec((None, None, None, 1, d), lambda i: (lyr, grp_of_block(i), which, 0, 0))

    in_specs, args = [row_spec], [x]
    if has_delta:
        in_specs += [row_spec, mod_spec(gate_layer, gate_idx)]
        args += [delta, mod5]
    if want_h:
        in_specs += [pl.BlockSpec((1, d), lambda i: (0, 0)), mod_spec(layer, shift_idx), mod_spec(layer, scale_idx)]
        args += [norm_g.reshape(1, d), mod5, mod5]
    out_specs, out_shape = [], []
    if has_delta:
        out_specs.append(row_spec)
        out_shape.append(jax.ShapeDtypeStruct((rows, d), f32))
    if want_h:
        out_specs.append(row_spec)
        out_shape.append(jax.ShapeDtypeStruct((rows, d), bf16))
    return pl.pallas_call(
        functools.partial(_prenorm_kernel, has_delta=has_delta, want_h=want_h),
        grid=(rows // tm,), in_specs=in_specs, out_specs=out_specs, out_shape=out_shape,
        compiler_params=_cparams(("arbitrary",), 48), name="prenorm",
    )(*args)


def _matmul_kernel(a_ref, w_ref, o_ref):
    o_ref[...] = jnp.dot(a_ref[...], w_ref[...], preferred_element_type=f32).astype(o_ref.dtype)


def matmul(a, w, layer, tm, tn):
    t, k = a.shape
    n = w.shape[2]
    return pl.pallas_call(
        _matmul_kernel, grid=(t // tm, n // tn),
        in_specs=[pl.BlockSpec((tm, k), lambda i, j: (i, 0)),
                  pl.BlockSpec((None, k, tn), lambda i, j: (layer, 0, j))],
        out_specs=pl.BlockSpec((tm, tn), lambda i, j: (i, j)),
        out_shape=jax.ShapeDtypeStruct((t, n), bf16),
        compiler_params=_cparams(("arbitrary", "arbitrary"), 48), name="in_proj",
    )(a, w)


def _rope128(x, cos, sin_signed, half):
    lane = lax.broadcasted_iota(jnp.int32, x.shape, 1)
    first = (lane & (2 * half - 1)) < half
    swapped = jnp.where(first, pltpu.roll(x, V7X_LANES - half, 1), pltpu.roll(x, half, 1))
    return x * cos + swapped * sin_signed


def _gqa_prep_kernel(q_ref, kv_ref, gq_ref, gk_ref, cos_ref, sin_ref, qo_ref, ko_ref):
    cos, sin = cos_ref[...], sin_ref[...]
    qscale = HEAD_DIM ** -0.5 * LOG2E
    for h in range(GQA_HEADS):
        sl = slice(h * HEAD_DIM, (h + 1) * HEAD_DIM)
        q = q_ref[:, sl].astype(f32)
        q = q * lax.rsqrt(jnp.mean(q * q, axis=-1, keepdims=True) + EPS) * gq_ref[...]
        qo_ref[:, sl] = (_rope128(q, cos, sin, 32) * qscale).astype(bf16)
    for h in range(GQA_KV_HEADS):
        sl = slice(h * HEAD_DIM, (h + 1) * HEAD_DIM)
        k = kv_ref[:, sl].astype(f32)
        k = k * lax.rsqrt(jnp.mean(k * k, axis=-1, keepdims=True) + EPS) * gk_ref[...]
        ko_ref[:, sl] = _rope128(k, cos, sin, 32).astype(bf16)


def gqa_prep(p, gq, gk, cos_a, sin_a, tm):
    t = p.shape[0]
    qw, kw = GQA_HEADS * HEAD_DIM, GQA_KV_HEADS * HEAD_DIM
    tab = pl.BlockSpec((tm, V7X_LANES), lambda i: (i, 0))
    vec = pl.BlockSpec((1, HEAD_DIM), lambda i: (0, 0))
    return pl.pallas_call(
        _gqa_prep_kernel, grid=(t // tm,),
        in_specs=[pl.BlockSpec((tm, qw), lambda i: (i, QA_OFF // qw)),
                  pl.BlockSpec((tm, 2 * kw), lambda i: (i, KA_OFF // (2 * kw))), vec, vec, tab, tab],
        out_specs=[pl.BlockSpec((tm, qw), lambda i: (i, 0)), pl.BlockSpec((tm, kw), lambda i: (i, 0))],
        out_shape=[jax.ShapeDtypeStruct((t, qw), bf16), jax.ShapeDtypeStruct((t, kw), bf16)],
        compiler_params=_cparams(("arbitrary",), 32), name="gqa_prep",
    )(p, p, gq.reshape(1, -1), gk.reshape(1, -1), cos_a, sin_a)


def _mla_prep_kernel(q_ref, ckv_ref, kr_ref, kvg_ref, wkvb_ref, gq_ref, gkn_ref, gkr_ref, cos_ref, sin_ref,
                     qo_ref, ko_ref, vo_ref):
    cos, sin = cos_ref[...], sin_ref[...]
    qscale = MLA_QK ** -0.5 * LOG2E
    inv_qk = 1.0 / MLA_QK
    for h in range(MLA_HEADS):
        lo = h * MLA_QPAD
        qn = q_ref[:, lo:lo + V7X_LANES].astype(f32)
        qr = q_ref[:, lo + V7X_LANES:lo + MLA_QPAD].astype(f32)
        ss = jnp.sum(qn * qn, axis=-1, keepdims=True) + jnp.sum(qr * qr, axis=-1, keepdims=True)
        r = lax.rsqrt(ss * inv_qk + EPS) * qscale
        qo_ref[:, lo:lo + V7X_LANES] = (qn * r * gq_ref[:, :V7X_LANES]).astype(bf16)
        qo_ref[:, lo + V7X_LANES:lo + MLA_QPAD] = (_rope128(qr * gq_ref[:, V7X_LANES:], cos, sin, 16) * r).astype(bf16)
    c = ckv_ref[...].astype(f32)
    cn = c * lax.rsqrt(jnp.mean(c * c, axis=-1, keepdims=True) + EPS) * kvg_ref[...]
    kv = jnp.dot(cn.astype(bf16), wkvb_ref[...], preferred_element_type=f32)
    kr = kr_ref[...].astype(f32)
    ss_r = jnp.sum(kr * kr, axis=-1, keepdims=True)
    kr_rot = _rope128(kr * gkr_ref[...], cos, sin, 16)
    nv = MLA_HEADS * MLA_NOPE
    for h in range(MLA_HEADS):
        kn = kv[:, h * MLA_NOPE:(h + 1) * MLA_NOPE]
        r = lax.rsqrt((jnp.sum(kn * kn, axis=-1, keepdims=True) + ss_r) * inv_qk + EPS)
        lo = h * MLA_QPAD
        ko_ref[:, lo:lo + V7X_LANES] = (kn * r * gkn_ref[...]).astype(bf16)
        ko_ref[:, lo + V7X_LANES:lo + MLA_QPAD] = (kr_rot * r).astype(bf16)
    vo_ref[...] = kv[:, nv:].astype(bf16)


def mla_prep(p, kv_g, wkvb_p, gq_pad, gk_nope, gk_rope, cos_c, sin_c, tm):
    t = p.shape[0]
    qw = MLA_HEADS * MLA_QPAD
    vw = MLA_HEADS * MLA_V
    tab = pl.BlockSpec((tm, V7X_LANES), lambda i: (i, 0))

    def vec(n):
        return pl.BlockSpec((1, n), lambda i: (0, 0))

    return pl.pallas_call(
        _mla_prep_kernel, grid=(t // tm,),
        in_specs=[pl.BlockSpec((tm, qw), lambda i: (i, QC_OFF // qw)),
                  pl.BlockSpec((tm, MLA_RANK), lambda i: (i, CKV_OFF // MLA_RANK)),
                  pl.BlockSpec((tm, V7X_LANES), lambda i: (i, KR_OFF // V7X_LANES)),
                  vec(MLA_RANK), pl.BlockSpec((MLA_RANK, 2 * vw), lambda i: (0, 0)),
                  vec(MLA_QPAD), vec(V7X_LANES), vec(V7X_LANES), tab, tab],
        out_specs=[pl.BlockSpec((tm, qw), lambda i: (i, 0)), pl.BlockSpec((tm, qw), lambda i: (i, 0)),
                   pl.BlockSpec((tm, vw), lambda i: (i, 0))],
        out_shape=[jax.ShapeDtypeStruct((t, qw), bf16), jax.ShapeDtypeStruct((t, qw), bf16),
                   jax.ShapeDtypeStruct((t, vw), bf16)],
        compiler_params=_cparams(("arbitrary",), 48), name="mla_prep",
    )(p, p, p, kv_g.reshape(1, -1), wkvb_p, gq_pad.reshape(1, -1), gk_nope.reshape(1, -1),
      gk_rope.reshape(1, -1), cos_c, sin_c)


FLASH_COLS = 256


def _scores(k, qt_ref, s_ref):
    s_ref[0:k.shape[0], :] = jnp.dot(k, qt_ref[...], preferred_element_type=f32)


def _softmax_pv(s_ref, tk, v, m_ref, l_ref, acc_ref):
    vt = v.T
    step = min(FLASH_COLS, s_ref.shape[1])
    for c0 in range(0, s_ref.shape[1], step):
        cols = slice(c0, c0 + step)
        s = s_ref[0:tk, cols]
        m_prev = m_ref[:, cols]
        m_new = jnp.maximum(m_prev, jnp.max(s, axis=0, keepdims=True))
        alpha = jnp.exp2(m_prev - m_new)
        pr = jnp.exp2(s - m_new)
        l_ref[:, cols] = alpha * l_ref[:, cols] + jnp.sum(pr, axis=0, keepdims=True)
        acc_ref[:, cols] = alpha * acc_ref[:, cols] + jnp.dot(vt, pr.astype(bf16), preferred_element_type=f32)
        m_ref[:, cols] = m_new


def _flash_kernel(*refs, group, dq, dv, tq, tk, n_lat):
    if n_lat:
        q_ref, kl_ref, vl_ref, kc_ref, vc_ref, o_ref, qt_ref, m_ref, l_ref, acc_ref, s0_ref, s1_ref = refs
    else:
        q_ref, kc_ref, vc_ref, o_ref, qt_ref, m_ref, l_ref, acc_ref, s0_ref, s1_ref = refs
    for g in range(group):
        qt_ref[:, g * tq:(g + 1) * tq] = q_ref[:, g * dq:(g + 1) * dq].T
    m_ref[...] = jnp.full(m_ref.shape, -1e30, f32)
    l_ref[...] = jnp.zeros(l_ref.shape, f32)
    acc_ref[...] = jnp.zeros(acc_ref.shape, f32)
    n_ctx = kc_ref.shape[0]
    _scores(kc_ref[...], qt_ref, s1_ref)
    if n_lat:
        def kchunk(j):
            return kl_ref[j * tk:(j + 1) * tk, :]

        def vchunk(j):
            return vl_ref[j * tk:(j + 1) * tk, :]

        assert n_lat % 2 == 0
        _scores(kchunk(0), qt_ref, s0_ref)
        _softmax_pv(s1_ref, n_ctx, vc_ref[...], m_ref, l_ref, acc_ref)
        for i in range(n_lat // 2 - 1):
            _scores(kchunk(2 * i + 1), qt_ref, s1_ref)
            _softmax_pv(s0_ref, tk, vchunk(2 * i), m_ref, l_ref, acc_ref)
            _scores(kchunk(2 * i + 2), qt_ref, s0_ref)
            _softmax_pv(s1_ref, tk, vchunk(2 * i + 1), m_ref, l_ref, acc_ref)
        _scores(kchunk(n_lat - 1), qt_ref, s1_ref)
        _softmax_pv(s0_ref, tk, vchunk(n_lat - 2), m_ref, l_ref, acc_ref)
        _softmax_pv(s1_ref, tk, vchunk(n_lat - 1), m_ref, l_ref, acc_ref)
    else:
        _softmax_pv(s1_ref, n_ctx, vc_ref[...], m_ref, l_ref, acc_ref)
    for g in range(group):
        cols = slice(g * tq, (g + 1) * tq)
        o_ref[:, g * dv:(g + 1) * dv] = (acc_ref[:, cols] / l_ref[:, cols]).T.astype(o_ref.dtype)


def flash_attention(q, k, v, v_col_off, *, batch, seq, ctx_len, kv_heads, group, dq, dv, tq, tk, lat):
    ctx_blk0 = batch * seq // ctx_len
    voff = v_col_off // dv
    rows = group * tq
    kc_spec = pl.BlockSpec((ctx_len, dq), lambda b, h, i: (ctx_blk0 + b, h))
    vc_spec = pl.BlockSpec((ctx_len, dv), lambda b, h, i: (ctx_blk0 + b, voff + h))
    if lat:
        nq = seq // tq
        in_specs = [pl.BlockSpec((tq, group * dq), lambda b, h, i: (b * nq + i, h)),
                    pl.BlockSpec((seq, dq), lambda b, h, i: (b, h)),
                    pl.BlockSpec((seq, dv), lambda b, h, i: (b, voff + h)), kc_spec, vc_spec]
        args = [q, k, v, k, v]
        out_spec = pl.BlockSpec((tq, group * dv), lambda b, h, i: (b * nq + i, h))
        out_rows = batch * seq
        n_lat = seq // tk
    else:
        nq = 1
        assert tq == ctx_len
        in_specs = [pl.BlockSpec((tq, group * dq), lambda b, h, i: (ctx_blk0 + b, h)), kc_spec, vc_spec]
        args = [q, k, v]
        out_spec = pl.BlockSpec((tq, group * dv), lambda b, h, i: (b, h))
        out_rows = batch * ctx_len
        n_lat = 0
    return pl.pallas_call(
        functools.partial(_flash_kernel, group=group, dq=dq, dv=dv, tq=tq, tk=tk, n_lat=n_lat),
        grid=(batch, kv_heads, nq), in_specs=in_specs, out_specs=out_spec,
        out_shape=jax.ShapeDtypeStruct((out_rows, kv_heads * group * dv), bf16),
        scratch_shapes=[pltpu.VMEM((dq, rows), bf16), pltpu.VMEM((1, rows), f32), pltpu.VMEM((1, rows), f32),
                        pltpu.VMEM((dv, rows), f32)] + [pltpu.VMEM((max(tk, ctx_len) if lat else ctx_len, rows), f32)] * 2,
        compiler_params=_cparams(("arbitrary", "arbitrary", "arbitrary"), 48),
        name="flash_lat" if lat else "flash_ctx",
    )(*args)


CONV_HALO = 16


def _conv3_kernel(prev_ref, cur_ref, next_ref, w_ref, b_ref, o_ref, *, blocks_per_seq):
    rows = cur_ref.shape[0]
    pos = pl.program_id(0) % blocks_per_seq
    x = cur_ref[...].astype(f32)
    row = lax.broadcasted_iota(jnp.int32, x.shape, 0)
    prev_row = jnp.where(pos == 0, 0.0, prev_ref[...].astype(f32)[CONV_HALO - 1:CONV_HALO, :])
    next_row = jnp.where(pos == blocks_per_seq - 1, 0.0, next_ref[...].astype(f32)[0:1, :])
    xm = jnp.where(row == 0, prev_row, pltpu.roll(x, 1, 0))
    xp = jnp.where(row == rows - 1, next_row, pltpu.roll(x, rows - 1, 0))
    o_ref[...] = xm * w_ref[0:1, :] + x * w_ref[1:2, :] + xp * w_ref[2:3, :] + b_ref[...]


def short_conv(p, conv_w, conv_b, row0, nrows, seq_len, rows, cw):
    width = 3 * HY_C
    rb0 = row0 // rows
    sub = rows // CONV_HALO
    last_halo = p.shape[0] // CONV_HALO - 1
    c0 = HY_OFF // cw
    return pl.pallas_call(
        functools.partial(_conv3_kernel, blocks_per_seq=seq_len // rows),
        grid=(nrows // rows, width // cw),
        in_specs=[
            pl.BlockSpec((CONV_HALO, cw), lambda r, c: (jnp.maximum((rb0 + r) * sub - 1, 0), c0 + c)),
            pl.BlockSpec((rows, cw), lambda r, c: (rb0 + r, c0 + c)),
            pl.BlockSpec((CONV_HALO, cw), lambda r, c: (jnp.minimum((rb0 + r + 1) * sub, last_halo), c0 + c)),
            pl.BlockSpec((3, cw), lambda r, c: (0, c)),
            pl.BlockSpec((1, cw), lambda r, c: (0, c)),
        ],
        out_specs=pl.BlockSpec((rows, cw), lambda r, c: (r, c)),
        out_shape=jax.ShapeDtypeStruct((nrows, width), f32),
        compiler_params=_cparams(("arbitrary", "arbitrary"), 32), name="short_conv",
    )(p, p, p, conv_w, conv_b.reshape(1, width))


@functools.lru_cache(maxsize=None)
def _dft_tables(length):
    nb = 128 if length >= 1024 else 16
    n_fft = 2 * length
    na = n_fft // nb
    nah = na // 2
    lo = np.arange(nb, dtype=np.int64)[:, None, None]
    k1 = np.arange(na, dtype=np.int64)[None, :, None]
    hi = np.arange(nah, dtype=np.int64)[None, None, :]
    ang = 2.0 * np.pi * (((nb * hi + lo) * k1) % n_fft) / n_fft
    c, s = np.cos(ang), np.sin(ang)
    a1 = np.concatenate([np.concatenate([c, s], axis=2), np.concatenate([-s, c], axis=2)], axis=1)
    ct, st = np.swapaxes(c, 1, 2) / n_fft, np.swapaxes(s, 1, 2) / n_fft
    a3 = np.concatenate([np.concatenate([ct, -st], axis=2), np.concatenate([st, ct], axis=2)], axis=1)
    kk = np.arange(nb, dtype=np.int64)
    angb = 2.0 * np.pi * ((kk[:, None] * kk[None, :]) % nb) / nb
    cb, sb = np.cos(angb), np.sin(angb)
    mf = np.block([[cb, sb], [-sb, cb]])
    mfc = np.block([[cb, -sb], [sb, cb]])
    return dict(nb=nb, na=na, nah=nah,
                a1=jnp.asarray(a1, dtype=bf16), a1r=jnp.asarray(a1[:, :, :nah], dtype=bf16),
                a3=jnp.asarray(a3, dtype=bf16), mf=jnp.asarray(mf, dtype=bf16), mfc=jnp.asarray(mfc, dtype=bf16))


def _tap_times(j, nb, nah):
    ridx = lax.broadcasted_iota(jnp.int32, (V7X_SUBLANES * nah, 1), 0)
    return nb * (ridx & (nah - 1)) + (j * V7X_SUBLANES + (ridx >> (nah.bit_length() - 1)))


def _filter_mlp_kernel(band_ref, w1_ref, b1_ref, w2_ref, b2_ref, freq_ref, o_ref, *, length, nb, nah):
    hp = lax.Precision.HIGHEST
    t = _tap_times(pl.program_id(0), nb, nah).astype(f32)
    t_unit = t / float(max(length - 1, 1))
    lane = lax.broadcasted_iota(jnp.int32, (t.shape[0], V7X_LANES), 1)
    ang = ((2.0 * math.pi / length) * t) * band_ref[...]
    feats = jnp.where(lane == 0, t_unit,
                      jnp.where(lane <= HY_BANDS, jnp.cos(ang), jnp.where(lane <= 2 * HY_BANDS, -jnp.sin(ang), 0.0)))
    h = jnp.sin(freq_ref[0:1, :] * (jnp.dot(feats, w1_ref[...], precision=hp, preferred_element_type=f32) + b1_ref[...]))
    o_ref[...] = jnp.sin(freq_ref[1:2, :] * (jnp.dot(h, w2_ref[...], precision=hp, preferred_element_type=f32)
                                            + b2_ref[...]))


def filter_mlp(tabs, length, band_row, w1p, b1, w2, b2, freq):
    nb, nah = tabs["nb"], tabs["nah"]
    rows = V7X_SUBLANES * nah
    full = lambda *shape: pl.BlockSpec(shape, lambda j: (0,) * len(shape))
    return pl.pallas_call(
        functools.partial(_filter_mlp_kernel, length=length, nb=nb, nah=nah), grid=(nb // V7X_SUBLANES,),
        in_specs=[full(1, V7X_LANES), full(V7X_LANES, HY_FW), full(1, HY_FW), full(HY_FW, HY_FW), full(1, HY_FW),
                  full(2, HY_FW)],
        out_specs=pl.BlockSpec((rows, HY_FW), lambda j: (j, 0)),
        out_shape=jax.ShapeDtypeStruct((length, HY_FW), f32),
        compiler_params=_cparams(("arbitrary",), 32), name="hy_filter_mlp",
    )(band_row, w1p, b1, w2, b2, freq)


def _filter_stage1_kernel(a1r_ref, h_ref, w3_ref, delta_ref, o_ref, ss_ref, *, length, nb, nah):
    j = pl.program_id(1)
    t_int = _tap_times(j, nb, nah)
    t_unit = t_int.astype(f32) / float(max(length - 1, 1))
    decay = jnp.exp(-t_unit * delta_ref[...])
    hb = h_ref[...].astype(bf16)

    @pl.when(j == 0)
    def _():
        ss_ref[...] = jnp.zeros(ss_ref.shape, f32)

    na = 2 * nah
    for o in range(HY_ORDER):
        fwd = jnp.dot(hb, w3_ref[2 * o].astype(bf16), preferred_element_type=f32) * decay
        bwd = jnp.dot(hb, w3_ref[2 * o + 1].astype(bf16), preferred_element_type=f32) * decay
        bwd = jnp.where(t_int == 0, 0.0, bwd)
        ss_ref[o:o + 1, :] += jnp.sum(fwd * fwd + bwd * bwd, axis=0, keepdims=True)
        sb, db = (fwd + bwd).astype(bf16), (fwd - bwd).astype(bf16)
        for l in range(V7X_SUBLANES):
            a = a1r_ref[l]
            fs = jnp.dot(a, sb[l * nah:(l + 1) * nah], preferred_element_type=f32)
            fd = jnp.dot(a, db[l * nah:(l + 1) * nah], preferred_element_type=f32)
            o_ref[o, 0, l] = _pack_complex(fs[:na], fs[na:])
            o_ref[o, 1, l] = _pack_complex(fd[:na], fd[na:])


def filter_stage1(tabs, length, hmlp, w3r, delta_row):
    nb, na, nah = tabs["nb"], tabs["na"], tabs["nah"]
    ncb = HY_C // V7X_LANES
    return pl.pallas_call(
        functools.partial(_filter_stage1_kernel, length=length, nb=nb, nah=nah),
        grid=(ncb, nb // V7X_SUBLANES),
        in_specs=[pl.BlockSpec((V7X_SUBLANES, 2 * na, nah), lambda c, j: (j, 0, 0)),
                  pl.BlockSpec((V7X_SUBLANES * nah, HY_FW), lambda c, j: (j, 0)),
                  pl.BlockSpec((2 * HY_ORDER, HY_FW, V7X_LANES), lambda c, j: (0, 0, c)),
                  pl.BlockSpec((1, V7X_LANES), lambda c, j: (0, c))],
        out_specs=[pl.BlockSpec((HY_ORDER, 2, V7X_SUBLANES, na, V7X_LANES), lambda c, j: (0, 0, j, 0, c)),
                   pl.BlockSpec((V7X_SUBLANES, V7X_LANES), lambda c, j: (0, c))],
        out_shape=[jax.ShapeDtypeStruct((HY_ORDER, 2, nb, na, HY_C), u32),
                   jax.ShapeDtypeStruct((V7X_SUBLANES, HY_C), f32)],
        compiler_params=_cparams(("arbitrary", "arbitrary"), 48), name="hy_filter_s1",
    )(tabs["a1r"], hmlp, w3r, delta_row)


def _tile_rows(ref, l):
    tiles = math.prod(ref.shape[:-2])
    flat = ref.reshape(tiles * V7X_SUBLANES, ref.shape[-1])
    return flat.at[pl.ds(l, tiles, stride=V7X_SUBLANES), :]


def _pack_complex(re, im):
    def rne(x):
        b = pltpu.bitcast(x, u32)
        return b + (jnp.uint32(0x7FFF) + ((b >> 16) & jnp.uint32(1)))
    return (rne(re) & jnp.uint32(0xFFFF0000)) | (rne(im) >> 16)


def _unpack_complex(w):
    return pltpu.bitcast(w & jnp.uint32(0xFFFF0000), f32), pltpu.bitcast(w << 16, f32)


def _filter_stage2_kernel(x_ref, m_ref, o_ref, *, nb):
    for kl in range(V7X_SUBLANES):
        re, im = _unpack_complex(_tile_rows(x_ref, kl)[...])
        rs = jnp.concatenate([re[:nb], im[:nb]], axis=0).astype(bf16)
        rd = jnp.concatenate([re[nb:], im[nb:]], axis=0).astype(bf16)
        h_re = jnp.dot(m_ref[0:nb, :], rs, preferred_element_type=f32)
        h_im = jnp.dot(m_ref[nb:, :], rd, preferred_element_type=f32)
        o_ref[kl] = _pack_complex(h_re, h_im)


def filter_stage2(tabs, hin):
    nb, na = tabs["nb"], tabs["na"]
    ncb = HY_C // V7X_LANES
    return pl.pallas_call(
        functools.partial(_filter_stage2_kernel, nb=nb), grid=(HY_ORDER, ncb, na // V7X_SUBLANES),
        in_specs=[pl.BlockSpec((None, 2, nb, V7X_SUBLANES, V7X_LANES), lambda o, c, j: (o, 0, 0, j, c)),
                  pl.BlockSpec((2 * nb, 2 * nb), lambda o, c, j: (0, 0))],
        out_specs=pl.BlockSpec((None, V7X_SUBLANES, nb, V7X_LANES), lambda o, c, j: (o, j, 0, c)),
        out_shape=jax.ShapeDtypeStruct((HY_ORDER, na, nb, HY_C), u32),
        compiler_params=_cparams(("arbitrary", "arbitrary", "arbitrary"), 48), name="hy_filter_s2",
    )(hin, tabs["mf"])


def _conv_stage1_kernel(zr_ref, zi_ref, a1_ref, o_ref, *, na):
    for l in range(V7X_SUBLANES):
        rhs = jnp.concatenate([_tile_rows(zr_ref, l)[...], _tile_rows(zi_ref, l)[...]], axis=0).astype(bf16)
        y = jnp.dot(a1_ref[l], rhs, preferred_element_type=f32)
        o_ref[l] = _pack_complex(y[:na], y[na:])


def conv_stage1(tabs, z4, col_off):
    nb, na, nah = tabs["nb"], tabs["na"], tabs["nah"]
    ncb = HY_C // V7X_LANES
    c0 = col_off // V7X_LANES
    return pl.pallas_call(
        functools.partial(_conv_stage1_kernel, na=na), grid=(ncb, nb // V7X_SUBLANES),
        in_specs=[pl.BlockSpec((None, nah, V7X_SUBLANES, V7X_LANES), lambda c, j: (0, 0, j, c0 + c)),
                  pl.BlockSpec((None, nah, V7X_SUBLANES, V7X_LANES), lambda c, j: (1, 0, j, c0 + c)),
                  pl.BlockSpec((V7X_SUBLANES, 2 * na, na), lambda c, j: (j, 0, 0))],
        out_specs=pl.BlockSpec((V7X_SUBLANES, na, V7X_LANES), lambda c, j: (j, 0, c)),
        out_shape=jax.ShapeDtypeStruct((nb, na, HY_C), u32),
        compiler_params=_cparams(("arbitrary", "arbitrary"), 48), name="hy_conv_s1",
    )(z4, z4, tabs["a1"])


def _conv_stage2_kernel(x_ref, h_ref, mf_ref, mfc_ref, o_ref, *, nb):
    for kl in range(V7X_SUBLANES):
        re, im = _unpack_complex(_tile_rows(x_ref, kl)[...])
        rhs = jnp.concatenate([re, im], axis=0).astype(bf16)
        x = jnp.dot(mf_ref[...], rhs, preferred_element_type=f32)
        xr, xi = x[:nb], x[nb:]
        hr, hi = _unpack_complex(h_ref[kl])
        y = jnp.concatenate([xr * hr - xi * hi, xr * hi + xi * hr], axis=0).astype(bf16)
        e = jnp.dot(mfc_ref[...], y, preferred_element_type=f32)
        o_ref[kl] = _pack_complex(e[:nb], e[nb:])


def conv_stage2(tabs, din, hspec, order):
    nb, na = tabs["nb"], tabs["na"]
    ncb = HY_C // V7X_LANES
    return pl.pallas_call(
        functools.partial(_conv_stage2_kernel, nb=nb), grid=(ncb, na // V7X_SUBLANES),
        in_specs=[pl.BlockSpec((nb, V7X_SUBLANES, V7X_LANES), lambda c, j: (0, j, c)),
                  pl.BlockSpec((None, V7X_SUBLANES, nb, V7X_LANES), lambda c, j: (order, j, 0, c)),
                  pl.BlockSpec((2 * nb, 2 * nb), lambda c, j: (0, 0)),
                  pl.BlockSpec((2 * nb, 2 * nb), lambda c, j: (0, 0))],
        out_specs=pl.BlockSpec((V7X_SUBLANES, nb, V7X_LANES), lambda c, j: (j, 0, c)),
        out_shape=jax.ShapeDtypeStruct((na, nb, HY_C), u32),
        compiler_params=_cparams(("arbitrary", "arbitrary"), 48), name="hy_conv_s2",
    )(din, hspec, tabs["mf"], tabs["mfc"])


def _conv_stage3_kernel(e_ref, a3_ref, zr_ref, zi_ref, gr_ref, gi_ref, ss_ref, skip_ref, o_ref, *, order):
    rs = lax.rsqrt(ss_ref[order:order + 1, :] + EPS)
    skip = skip_ref[order:order + 1, :]
    for l in range(V7X_SUBLANES):
        re, im = _unpack_complex(_tile_rows(e_ref, l)[...])
        rhs = jnp.concatenate([re, im], axis=0).astype(bf16)
        y = jnp.dot(a3_ref[l], rhs, preferred_element_type=f32) * rs
        z = jnp.concatenate([_tile_rows(zr_ref, l)[...], _tile_rows(zi_ref, l)[...]], axis=0)
        g = jnp.concatenate([_tile_rows(gr_ref, l)[...], _tile_rows(gi_ref, l)[...]], axis=0)
        _tile_rows(o_ref, l)[...] = g * (y + skip * z)


def conv_stage3(tabs, ein, z4, z_off, g4, g_off, ss, skip, order):
    nb, na, nah = tabs["nb"], tabs["na"], tabs["nah"]
    ncb = HY_C // V7X_LANES
    zc, gc = z_off // V7X_LANES, g_off // V7X_LANES

    def slab(b, c0):
        return pl.BlockSpec((None, nah, V7X_SUBLANES, V7X_LANES), lambda c, j: (b, 0, j, c0 + c))

    return pl.pallas_call(
        functools.partial(_conv_stage3_kernel, order=order), grid=(ncb, nb // V7X_SUBLANES),
        in_specs=[pl.BlockSpec((na, V7X_SUBLANES, V7X_LANES), lambda c, j: (0, j, c)),
                  pl.BlockSpec((V7X_SUBLANES, na, 2 * na), lambda c, j: (j, 0, 0)),
                  slab(0, zc), slab(1, zc), slab(0, gc), slab(1, gc),
                  pl.BlockSpec((V7X_SUBLANES, V7X_LANES), lambda c, j: (0, c)),
                  pl.BlockSpec((HY_ORDER, V7X_LANES), lambda c, j: (0, c))],
        out_specs=pl.BlockSpec((2, nah, V7X_SUBLANES, V7X_LANES), lambda c, j: (0, 0, j, c)),
        out_shape=jax.ShapeDtypeStruct((2, nah, nb, HY_C), f32),
        compiler_params=_cparams(("arbitrary", "arbitrary"), 48), name="hy_conv_s3",
    )(ein, tabs["a3"], z4, z4, g4, g4, ss, skip)


def hyena_mixer(uc, length, hf, skip):
    tabs = _dft_tables(length)
    nb, nah = tabs["nb"], tabs["nah"]
    band_row, w1p, b1, w2, b2, w3r, freq, delta_row = hf
    hmlp = filter_mlp(tabs, length, band_row, w1p, b1, w2, b2, freq)
    hin, ss = filter_stage1(tabs, length, hmlp, w3r, delta_row)
    hspec = filter_stage2(tabs, hin)
    u4 = uc.reshape(2, nah, nb, 3 * HY_C)
    z4, z_off = u4, 0
    for o in range(HY_ORDER):
        din = conv_stage1(tabs, z4, z_off)
        ein = conv_stage2(tabs, din, hspec, o)
        z4 = conv_stage3(tabs, ein, z4, z_off, u4, (o + 1) * HY_C, ss, skip, o)
        z_off = 0
    return z4.reshape(2 * length, HY_C)


def _merge_kernel(gl_ref, al_ref, ac_ref, hl_ref, hc_ref, cl_ref, cc_ref, wg_ref, bg_ref, wb_ref, wo_ref,
                  o_ref, acc_ref, br_ref, *, lat_blocks):
    i, j = pl.program_id(0), pl.program_id(1)

    @pl.when(j == 0)
    def _():
        acc_ref[...] = jnp.zeros(acc_ref.shape, f32)

    @pl.when((j == 0) & (i < lat_blocks))
    def _():
        br_ref[0] = al_ref[...]
        br_ref[1] = hl_ref[...].astype(bf16)
        br_ref[2] = cl_ref[...]

    @pl.when((j == 0) & (i >= lat_blocks))
    def _():
        br_ref[0] = ac_ref[...]
        br_ref[1] = hc_ref[...].astype(bf16)
        br_ref[2] = cc_ref[...]

    gl = gl_ref[...]
    y = None
    for n in range(N_BRANCH):
        z = jnp.dot(gl, wg_ref[n], preferred_element_type=f32) + bg_ref[n]
        g = 1.0 / (1.0 + jnp.exp(-z))
        term = g * jnp.dot(br_ref[n], wb_ref[n], preferred_element_type=f32)
        y = term if y is None else y + term
    acc_ref[...] += jnp.dot(y.astype(bf16), wo_ref[...], preferred_element_type=f32)

    @pl.when(j == pl.num_programs(1) - 1)
    def _():
        o_ref[...] = acc_ref[...].astype(o_ref.dtype)


def merge_out(p, a, hy, c, wg, bg, wb, wo, layer, tm, tj):
    t = p.shape[0]
    d = wo.shape[2]
    nl = a[0].shape[0] // tm
    assert a[1].shape[0] % tm == 0 and t == a[0].shape[0] + a[1].shape[0]
    lat = pl.BlockSpec((tm, BRANCH_W), lambda i, j: (jnp.minimum(i, nl - 1), 0))
    ctx = pl.BlockSpec((tm, BRANCH_W), lambda i, j: (jnp.maximum(i - nl, 0), 0))
    return pl.pallas_call(
        functools.partial(_merge_kernel, lat_blocks=nl), grid=(t // tm, d // tj),
        in_specs=[pl.BlockSpec((tm, GATE_RANK), lambda i, j: (i, GATE_OFF // GATE_RANK)),
                  lat, ctx, lat, ctx, lat, ctx,
                  pl.BlockSpec((None, N_BRANCH, GATE_RANK, tj), lambda i, j: (layer, 0, 0, j)),
                  pl.BlockSpec((None, N_BRANCH, 1, tj), lambda i, j: (layer, 0, 0, j)),
                  pl.BlockSpec((None, N_BRANCH, BRANCH_W, tj), lambda i, j: (layer, 0, 0, j)),
                  pl.BlockSpec((None, tj, d), lambda i, j: (layer, j, 0))],
        out_specs=pl.BlockSpec((tm, d), lambda i, j: (i, 0)),
        out_shape=jax.ShapeDtypeStruct((t, d), bf16),
        scratch_shapes=[pltpu.VMEM((tm, d), f32), pltpu.VMEM((N_BRANCH, tm, BRANCH_W), bf16)],
        compiler_params=_cparams(("arbitrary", "arbitrary"), 58), name="merge_out",
    )(p, a[0], a[1], hy[0], hy[1], c[0], c[1], wg, bg, wb, wo)


def _ffn_kernel(h_ref, wg_ref, wu_ref, wd_ref, o_ref, acc_ref):
    j = pl.program_id(1)

    @pl.when(j == 0)
    def _():
        acc_ref[...] = jnp.zeros(acc_ref.shape, f32)

    h = h_ref[...]
    g = jnp.dot(h, wg_ref[...], preferred_element_type=f32)
    u = jnp.dot(h, wu_ref[...], preferred_element_type=f32)
    a = (g * (1.0 / (1.0 + jnp.exp(-g))) * u).astype(bf16)
    acc_ref[...] += jnp.dot(a, wd_ref[...], preferred_element_type=f32)

    @pl.when(j == pl.num_programs(1) - 1)
    def _():
        o_ref[...] = acc_ref[...].astype(o_ref.dtype)


def ffn(h, w_gu, w_down, layer, tm, th):
    t, d = h.shape
    hidden = w_down.shape[1]
    nh = hidden // th
    return pl.pallas_call(
        _ffn_kernel, grid=(t // tm, nh),
        in_specs=[pl.BlockSpec((tm, d), lambda i, j: (i, 0)),
                  pl.BlockSpec((None, d, th), lambda i, j: (layer, 0, j)),
                  pl.BlockSpec((None, d, th), lambda i, j: (layer, 0, nh + j)),
                  pl.BlockSpec((None, th, d), lambda i, j: (layer, j, 0))],
        out_specs=pl.BlockSpec((tm, d), lambda i, j: (i, 0)),
        out_shape=jax.ShapeDtypeStruct((t, d), bf16),
        scratch_shapes=[pltpu.VMEM((tm, d), f32)],
        compiler_params=_cparams(("arbitrary", "arbitrary"), 56), name="ffn",
    )(h, w_gu, w_gu, w_down)


def _pack_w_in(w_in):
    depth, d, _ = w_in.shape
    o_qa, o_ka, o_hy = 0, GQA_HEADS * HEAD_DIM, (GQA_HEADS + 2 * GQA_KV_HEADS) * HEAD_DIM
    o_qc = o_hy + 3 * HY_C
    o_ckv = o_qc + MLA_HEADS * MLA_QK
    o_kr = o_ckv + MLA_RANK
    o_gate = o_kr + MLA_ROPE
    qc = w_in[:, :, o_qc:o_ckv].reshape(depth, d, MLA_HEADS, MLA_QK)
    qc = jnp.pad(qc, ((0, 0), (0, 0), (0, 0), (0, MLA_QPAD - MLA_QK))).reshape(depth, d, MLA_HEADS * MLA_QPAD)
    tail = NP_COLS - KR_OFF - MLA_ROPE
    parts = [qc, w_in[:, :, o_qa:o_ka], w_in[:, :, o_ka:o_hy], w_in[:, :, o_hy:o_qc], w_in[:, :, o_ckv:o_kr],
             w_in[:, :, o_gate:o_gate + GATE_RANK], w_in[:, :, o_kr:o_gate], jnp.zeros((depth, d, tail), w_in.dtype)]
    return jnp.concatenate(parts, axis=2).astype(bf16)


def _pack_w_kvb(w):
    depth = w.shape[0]
    w = w.reshape(depth, MLA_RANK, MLA_HEADS, 2, MLA_NOPE)
    return jnp.swapaxes(w, 2, 3).reshape(depth, MLA_RANK, 2 * MLA_HEADS * MLA_NOPE).astype(bf16)


def kernel(x, c, ctx, c_ctx, norm1_g, norm2_g, ada_down, ada_up, ada_b, w_in, gqa_q_norm, gqa_k_norm, hy_conv_w,
           hy_conv_b, hf_w1, hf_b1, hf_w2, hf_b2, hf_w3, hf_freq, hy_skip, mla_kv_norm, mla_w_kvb, mla_q_norm,
           mla_k_norm, w_gate_up, b_gate, w_branch, w_out, ffn_w_gu, ffn_w_down):
    batch, seq, d = x.shape
    ctx_len = ctx.shape[1]
    depth = w_in.shape[0]
    hidden = ffn_w_down.shape[1]
    assert batch == 2, "the Hyena long convolution packs the two batches as one complex sequence"
    assert w_in.shape[2] == IN_COLS and seq % GRID_W == 0
    n_lat, n_ctx = batch * seq, batch * ctx_len
    t_all = n_lat + n_ctx

    tm = _row_tile(seq, n_ctx, (512, 256, 128))
    tm_ew = _row_tile(seq, n_ctx, (256, 128))
    tm_mm = 768 if t_all % 768 == 0 else tm
    tm_in, tn_in = tm_mm, 1536
    tj = 512
    th = 512
    tq_a = min(256, ctx_len)
    tq_c = min(1024, seq)
    tk = min(512, seq)

    def grp(tile):
        per = seq // tile
        return lambda i: jnp.minimum(i // per, batch)

    w_in_p = _pack_w_in(w_in)
    w_kvb_p = _pack_w_kvb(mla_w_kvb)
    wg_b, wb_b, wo_b = w_gate_up.astype(bf16), w_branch.astype(bf16), w_out.astype(bf16)
    fpad = (-hidden) % th
    wgu_b = jnp.concatenate([jnp.pad(ffn_w_gu[:, :, :hidden], ((0, 0), (0, 0), (0, fpad))),
                             jnp.pad(ffn_w_gu[:, :, hidden:], ((0, 0), (0, 0), (0, fpad)))], axis=2).astype(bf16)
    wd_b = jnp.pad(ffn_w_down, ((0, 0), (0, fpad), (0, 0))).astype(bf16)
    bg = b_gate.reshape(depth, N_BRANCH, 1, d)
    gq_pad = jnp.pad(mla_q_norm, ((0, 0), (0, MLA_QPAD - MLA_QK)))
    gk_nope = mla_k_norm[:, :MLA_NOPE]
    gk_rope = jnp.pad(mla_k_norm[:, MLA_NOPE:], ((0, 0), (0, V7X_LANES - MLA_ROPE)))
    w1p = jnp.pad(hf_w1, ((0, 0), (0, V7X_LANES - HY_EMB), (0, 0)))
    w3r = jnp.swapaxes(hf_w3.reshape(depth, HY_FW, 2 * HY_ORDER, HY_C), 1, 2)
    bands = jnp.linspace(1e-4, HY_BANDS - 1, HY_BANDS, dtype=f32)
    band_row = jnp.concatenate([jnp.zeros((1,), f32), bands, bands,
                                jnp.zeros((V7X_LANES - HY_EMB,), f32)]).reshape(1, V7X_LANES)
    delta_row = jnp.abs(jnp.linspace(math.log(HY_DECAY_TARGET) / HY_DECAY_PCT_MIN,
                                     math.log(HY_DECAY_TARGET) / HY_DECAY_PCT_MAX, HY_C, dtype=f32)).reshape(1, HY_C)

    xs = jnp.concatenate([x.reshape(n_lat, d), ctx.reshape(n_ctx, d)], axis=0)
    tpos = jnp.arange(seq, dtype=jnp.int32)
    zpad = jnp.zeros((n_ctx,), f32)
    pos_row = jnp.concatenate([jnp.tile((tpos // GRID_W).astype(f32), batch), zpad]).reshape(t_all, 1)
    pos_col = jnp.concatenate([jnp.tile((tpos % GRID_W).astype(f32), batch), zpad]).reshape(t_all, 1)
    cond8 = jnp.concatenate([c, c_ctx[None, :], jnp.zeros((8 - batch - 1, d), f32)], axis=0)

    mod = ada_modulation(cond8, ada_down, ada_up, ada_b)
    mod5 = mod.reshape(depth, 8, N_MOD, 1, d)
    cos_a, sin_a, cos_c, sin_c = rope_tables(pos_row, pos_col, tm_ew)

    delta = None
    for l in range(depth):
        if l == 0:
            (h,) = prenorm(xs, mod5, l, grp(tm_ew), tm_ew, norm_g=norm1_g[l], shift_idx=0, scale_idx=1)
        else:
            xs, h = prenorm(xs, mod5, l, grp(tm_ew), tm_ew, delta=delta, gate_idx=5, gate_layer=l - 1,
                            norm_g=norm1_g[l], shift_idx=0, scale_idx=1)
        p = matmul(h, w_in_p, l, tm_in, tn_in)

        qa, ka = gqa_prep(p, gqa_q_norm[l], gqa_k_norm[l], cos_a, sin_a, tm_ew)
        fa = dict(batch=batch, seq=seq, ctx_len=ctx_len, kv_heads=GQA_KV_HEADS, group=GQA_GROUP,
                  dq=HEAD_DIM, dv=HEAD_DIM, tk=tk)
        att_a = (flash_attention(qa, ka, p, VA_OFF, tq=tq_a, lat=True, **fa),
                 flash_attention(qa, ka, p, VA_OFF, tq=ctx_len, lat=False, **fa))

        qc, kc, vc = mla_prep(p, mla_kv_norm[l], w_kvb_p[l], gq_pad[l], gk_nope[l], gk_rope[l], cos_c, sin_c, tm_ew)
        fc = dict(batch=batch, seq=seq, ctx_len=ctx_len, kv_heads=MLA_HEADS, group=1, dq=MLA_QPAD, dv=MLA_V, tk=tk)
        att_c = (flash_attention(qc, kc, vc, 0, tq=tq_c, lat=True, **fc),
                 flash_attention(qc, kc, vc, 0, tq=ctx_len, lat=False, **fc))

        hf = (band_row, w1p[l], hf_b1[l].reshape(1, -1), hf_w2[l], hf_b2[l].reshape(1, -1), w3r[l], hf_freq[l],
              delta_row)
        uc_lat = short_conv(p, hy_conv_w[l], hy_conv_b[l], 0, n_lat, seq, min(512, seq), 512)
        uc_ctx = short_conv(p, hy_conv_w[l], hy_conv_b[l], n_lat, n_ctx, ctx_len, min(256, ctx_len), 512)
        hy = (hyena_mixer(uc_lat, seq, hf, hy_skip[l]), hyena_mixer(uc_ctx, ctx_len, hf, hy_skip[l]))

        delta = merge_out(p, att_a, hy, att_c, wg_b, bg, wb_b, wo_b, l, tm, tj)
        xs, h2 = prenorm(xs, mod5, l, grp(tm_ew), tm_ew, delta=delta, gate_idx=2, gate_layer=l,
                         norm_g=norm2_g[l], shift_idx=3, scale_idx=4)
        delta = ffn(h2, wgu_b, wd_b, l, tm, th)

    (out,) = prenorm(xs, mod5, depth - 1, grp(tm_ew), tm_ew, delta=delta, gate_idx=5, gate_layer=depth - 1,
                     rows=n_lat)
    return out.reshape(batch, seq, d)
```

```python
import functools
import math

import jax
import jax.numpy as jnp
import numpy as np
from jax import lax
from jax.experimental import pallas as pl
from jax.experimental.pallas import tpu as pltpu

f32 = jnp.float32
bf16 = jnp.bfloat16
u32 = jnp.uint32

GRID_W = 64
ROPE_THETA = 10000.0
EPS = 1e-6
HEAD_DIM = 128
GQA_HEADS = 8
GQA_KV_HEADS = 2
GQA_GROUP = GQA_HEADS // GQA_KV_HEADS
HY_C = 1024
HY_ORDER = 2
HY_EMB = 33
HY_BANDS = (HY_EMB - 1) // 2
HY_FW = 64
HY_DECAY_TARGET = 1e-2
HY_DECAY_PCT_MIN = 0.3
HY_DECAY_PCT_MAX = 1.5
MLA_HEADS = 8
MLA_NOPE = 128
MLA_ROPE = 64
MLA_QK = MLA_NOPE + MLA_ROPE
MLA_V = 128
MLA_RANK = 512
N_BRANCH = 3
BRANCH_W = 1024
GATE_RANK = 256
ADA_RANK = 256
N_MOD = 6
IN_COLS = 6976

V7X_LANES = 128
V7X_SUBLANES = 8
V7X_VMEM_BYTES = 64 * 1024 * 1024
MIB = 1024 * 1024

MLA_QPAD = 2 * V7X_LANES
QC_OFF = 0
QA_OFF = QC_OFF + MLA_HEADS * MLA_QPAD
KA_OFF = QA_OFF + GQA_HEADS * HEAD_DIM
VA_OFF = KA_OFF + GQA_KV_HEADS * HEAD_DIM
HY_OFF = VA_OFF + GQA_KV_HEADS * HEAD_DIM
CKV_OFF = HY_OFF + 3 * HY_C
GATE_OFF = CKV_OFF + MLA_RANK
KR_OFF = GATE_OFF + GATE_RANK
NP_COLS = 7680

LOG2E = math.log2(math.e)


def _cparams(sem, vmem_mib):
    return pltpu.CompilerParams(dimension_semantics=sem, vmem_limit_bytes=int(vmem_mib * MIB))


def _row_tile(s, nctx, cands):
    for t in cands:
        if s % t == 0 and nctx % t == 0:
            return t
    raise ValueError("no row tile fits")


def _ada_kernel(cond_ref, down_ref, up_ref, b_ref, o_ref, t_ref):
    @pl.when(pl.program_id(1) == 0)
    def _():
        c = cond_ref[...]
        c = c * (1.0 / (1.0 + jnp.exp(-c)))
        t_ref[...] = jnp.dot(c, down_ref[...], precision=lax.Precision.HIGHEST, preferred_element_type=f32)

    o_ref[...] = jnp.dot(t_ref[...], up_ref[...], precision=lax.Precision.HIGHEST,
                         preferred_element_type=f32) + b_ref[...]


def ada_modulation(cond8, ada_down, ada_up, ada_b):
    depth, d, _ = ada_down.shape
    n = ada_up.shape[2]
    tn = 2048 if n % 2048 == 0 else 512
    return pl.pallas_call(
        _ada_kernel,
        grid=(depth, n // tn),
        in_specs=[
            pl.BlockSpec((8, d), lambda l, j: (0, 0)),
            pl.BlockSpec((None, d, ADA_RANK), lambda l, j: (l, 0, 0)),
            pl.BlockSpec((None, ADA_RANK, tn), lambda l, j: (l, 0, j)),
            pl.BlockSpec((None, 1, tn), lambda l, j: (l, 0, j)),
        ],
        out_specs=pl.BlockSpec((None, 8, tn), lambda l, j: (l, 0, j)),
        out_shape=jax.ShapeDtypeStruct((depth, 8, n), f32),
        scratch_shapes=[pltpu.VMEM((8, ADA_RANK), f32)],
        compiler_params=_cparams(("arbitrary", "arbitrary"), 40),
        name="ada_modulation",
    )(cond8, ada_down, ada_up, ada_b.reshape(depth, 1, n))


def _rope_table_kernel(pr_ref, pc_ref, ca_ref, sa_ref, cc_ref, sc_ref):
    shape = ca_ref.shape
    lane = lax.broadcasted_iota(jnp.int32, shape, 1)
    pr = jnp.broadcast_to(pr_ref[...], shape)
    pc = jnp.broadcast_to(pc_ref[...], shape)
    log_theta = math.log(ROPE_THETA)
    fa = jnp.exp((lane & 31).astype(f32) * (-log_theta / 32.0))
    ang = jnp.where(lane < 64, pr, pc) * fa
    ca_ref[...] = jnp.cos(ang)
    sa_ref[...] = jnp.where((lane & 63) < 32, -1.0, 1.0) * jnp.sin(ang)
    fc = jnp.exp((lane & 15).astype(f32) * (-log_theta / 16.0))
    angc = jnp.where(lane < 64, jnp.where(lane < 32, pr, pc) * fc, 0.0)
    cc_ref[...] = jnp.cos(angc)
    sc_ref[...] = jnp.where((lane & 31) < 16, -1.0, 1.0) * jnp.sin(angc)


def rope_tables(pos_row, pos_col, tm):
    t = pos_row.shape[0]
    spec1 = pl.BlockSpec((tm, 1), lambda i: (i, 0))
    spec = pl.BlockSpec((tm, V7X_LANES), lambda i: (i, 0))
    sh = jax.ShapeDtypeStruct((t, V7X_LANES), f32)
    return pl.pallas_call(
        _rope_table_kernel, grid=(t // tm,), in_specs=[spec1, spec1], out_specs=[spec] * 4,
        out_shape=[sh] * 4, compiler_params=_cparams(("arbitrary",), 32), name="rope_tables",
    )(pos_row, pos_col)


def _prenorm_kernel(*refs, has_delta, want_h):
    if has_delta:
        x_ref, d_ref, gate_ref = refs[:3]
        rest = refs[3:]
    else:
        x_ref = refs[0]
        rest = refs[1:]
    x = x_ref[...]
    if has_delta:
        x = x + gate_ref[...] * d_ref[...].astype(f32)
    if want_h:
        g_ref, shift_ref, scale_ref = rest[:3]
        outs = rest[3:]
    else:
        outs = rest
    k = 0
    if has_delta:
        outs[k][...] = x
        k += 1
    if want_h:
        y = x * lax.rsqrt(jnp.mean(x * x, axis=-1, keepdims=True) + EPS)
        y = y * g_ref[...]
        outs[k][...] = (y * (1.0 + scale_ref[...]) + shift_ref[...]).astype(bf16)


def prenorm(x, mod5, layer, grp_of_block, tm, *, delta=None, gate_idx=None, gate_layer=None,
            norm_g=None, shift_idx=None, scale_idx=None, rows=None):
    t, d = x.shape
    rows = t if rows is None else rows
    has_delta = delta is not None
    want_h = norm_g is not None
    row_spec = pl.BlockSpec((tm, d), lambda i: (i, 0))

    def mod_spec(lyr, which):
        return pl.BlockSpec((None, None, None, 1, d), lambda i: (lyr, grp_of_block(i), which, 0, 0))

    in_specs, args = [row_spec], [x]
    if has_delta:
        in_specs += [row_spec, mod_spec(gate_layer, gate_idx)]
        args += [delta, mod5]
    if want_h:
        in_specs += [pl.BlockSpec((1, d), lambda i: (0, 0)), mod_spec(layer, shift_idx), mod_spec(layer, scale_idx)]
        args += [norm_g.reshape(1, d), mod5, mod5]
    out_specs, out_shape = [], []
    if has_delta:
        out_specs.append(row_spec)
        out_shape.append(jax.ShapeDtypeStruct((rows, d), f32))
    if want_h:
        out_specs.append(row_spec)
        out_shape.append(jax.ShapeDtypeStruct((rows, d), bf16))
    return pl.pallas_call(
        functools.partial(_prenorm_kernel, has_delta=has_delta, want_h=want_h),
        grid=(rows // tm,), in_specs=in_specs, out_specs=out_specs, out_shape=out_shape,
        compiler_params=_cparams(("arbitrary",), 48), name="prenorm",
    )(*args)


def _matmul_kernel(a_ref, w_ref, o_ref):
    o_ref[...] = jnp.dot(a_ref[...], w_ref[...], preferred_element_type=f32).astype(o_ref.dtype)


def matmul(a, w, layer, tm, tn):
    t, k = a.shape
    n = w.shape[2]
    return pl.pallas_call(
        _matmul_kernel, grid=(t // tm, n // tn),
        in_specs=[pl.BlockSpec((tm, k), lambda i, j: (i, 0)),
                  pl.BlockSpec((None, k, tn), lambda i, j: (layer, 0, j))],
        out_specs=pl.BlockSpec((tm, tn), lambda i, j: (i, j)),
        out_shape=jax.ShapeDtypeStruct((t, n), bf16),
        compiler_params=_cparams(("arbitrary", "arbitrary"), 48), name="in_proj",
    )(a, w)


def _rope128(x, cos, sin_signed, half):
    lane = lax.broadcasted_iota(jnp.int32, x.shape, 1)
    first = (lane & (2 * half - 1)) < half
    swapped = jnp.where(first, pltpu.roll(x, V7X_LANES - half, 1), pltpu.roll(x, half, 1))
    return x * cos + swapped * sin_signed


def _gqa_prep_kernel(q_ref, kv_ref, gq_ref, gk_ref, cos_ref, sin_ref, qo_ref, ko_ref):
    cos, sin = cos_ref[...], sin_ref[...]
    qscale = HEAD_DIM ** -0.5 * LOG2E
    for h in range(GQA_HEADS):
        sl = slice(h * HEAD_DIM, (h + 1) * HEAD_DIM)
        q = q_ref[:, sl].astype(f32)
        q = q * lax.rsqrt(jnp.mean(q * q, axis=-1, keepdims=True) + EPS) * gq_ref[...]
        qo_ref[:, sl] = (_rope128(q, cos, sin, 32) * qscale).astype(bf16)
    for h in range(GQA_KV_HEADS):
        sl = slice(h * HEAD_DIM, (h + 1) * HEAD_DIM)
        k = kv_ref[:, sl].astype(f32)
        k = k * lax.rsqrt(jnp.mean(k * k, axis=-1, keepdims=True) + EPS) * gk_ref[...]
        ko_ref[:, sl] = _rope128(k, cos, sin, 32).astype(bf16)


def gqa_prep(p, gq, gk, cos_a, sin_a, tm):
    t = p.shape[0]
    qw, kw = GQA_HEADS * HEAD_DIM, GQA_KV_HEADS * HEAD_DIM
    tab = pl.BlockSpec((tm, V7X_LANES), lambda i: (i, 0))
    vec = pl.BlockSpec((1, HEAD_DIM), lambda i: (0, 0))
    return pl.pallas_call(
        _gqa_prep_kernel, grid=(t // tm,),
        in_specs=[pl.BlockSpec((tm, qw), lambda i: (i, QA_OFF // qw)),
                  pl.BlockSpec((tm, 2 * kw), lambda i: (i, KA_OFF // (2 * kw))), vec, vec, tab, tab],
        out_specs=[pl.BlockSpec((tm, qw), lambda i: (i, 0)), pl.BlockSpec((tm, kw), lambda i: (i, 0))],
        out_shape=[jax.ShapeDtypeStruct((t, qw), bf16), jax.ShapeDtypeStruct((t, kw), bf16)],
        compiler_params=_cparams(("arbitrary",), 32), name="gqa_prep",
    )(p, p, gq.reshape(1, -1), gk.reshape(1, -1), cos_a, sin_a)


def _mla_prep_kernel(q_ref, ckv_ref, kr_ref, kvg_ref, wkvb_ref, gq_ref, gkn_ref, gkr_ref, cos_ref, sin_ref,
                     qo_ref, ko_ref, vo_ref):
    cos, sin = cos_ref[...], sin_ref[...]
    qscale = MLA_QK ** -0.5 * LOG2E
    inv_qk = 1.0 / MLA_QK
    for h in range(MLA_HEADS):
        lo = h * MLA_QPAD
        qn = q_ref[:, lo:lo + V7X_LANES].astype(f32)
        qr = q_ref[:, lo + V7X_LANES:lo + MLA_QPAD].astype(f32)
        ss = jnp.sum(qn * qn, axis=-1, keepdims=True) + jnp.sum(qr * qr, axis=-1, keepdims=True)
        r = lax.rsqrt(ss * inv_qk + EPS) * qscale
        qo_ref[:, lo:lo + V7X_LANES] = (qn * r * gq_ref[:, :V7X_LANES]).astype(bf16)
        qo_ref[:, lo + V7X_LANES:lo + MLA_QPAD] = (_rope128(qr * gq_ref[:, V7X_LANES:], cos, sin, 16) * r).astype(bf16)
    c = ckv_ref[...].astype(f32)
    cn = c * lax.rsqrt(jnp.mean(c * c, axis=-1, keepdims=True) + EPS) * kvg_ref[...]
    kv = jnp.dot(cn.astype(bf16), wkvb_ref[...], preferred_element_type=f32)
    kr = kr_ref[...].astype(f32)
    ss_r = jnp.sum(kr * kr, axis=-1, keepdims=True)
    kr_rot = _rope128(kr * gkr_ref[...], cos, sin, 16)
    nv = MLA_HEADS * MLA_NOPE
    for h in range(MLA_HEADS):
        kn = kv[:, h * MLA_NOPE:(h + 1) * MLA_NOPE]
        r = lax.rsqrt((jnp.sum(kn * kn, axis=-1, keepdims=True) + ss_r) * inv_qk + EPS)
        lo = h * MLA_QPAD
        ko_ref[:, lo:lo + V7X_LANES] = (kn * r * gkn_ref[...]).astype(bf16)
        ko_ref[:, lo + V7X_LANES:lo + MLA_QPAD] = (kr_rot * r).astype(bf16)
    vo_ref[...] = kv[:, nv:].astype(bf16)


def mla_prep(p, kv_g, wkvb_p, gq_pad, gk_nope, gk_rope, cos_c, sin_c, tm):
    t = p.shape[0]
    qw = MLA_HEADS * MLA_QPAD
    vw = MLA_HEADS * MLA_V
    tab = pl.BlockSpec((tm, V7X_LANES), lambda i: (i, 0))

    def vec(n):
        return pl.BlockSpec((1, n), lambda i: (0, 0))

    return pl.pallas_call(
        _mla_prep_kernel, grid=(t // tm,),
        in_specs=[pl.BlockSpec((tm, qw), lambda i: (i, QC_OFF // qw)),
                  pl.BlockSpec((tm, MLA_RANK), lambda i: (i, CKV_OFF // MLA_RANK)),
                  pl.BlockSpec((tm, V7X_LANES), lambda i: (i, KR_OFF // V7X_LANES)),
                  vec(MLA_RANK), pl.BlockSpec((MLA_RANK, 2 * vw), lambda i: (0, 0)),
                  vec(MLA_QPAD), vec(V7X_LANES), vec(V7X_LANES), tab, tab],
        out_specs=[pl.BlockSpec((tm, qw), lambda i: (i, 0)), pl.BlockSpec((tm, qw), lambda i: (i, 0)),
                   pl.BlockSpec((tm, vw), lambda i: (i, 0))],
        out_shape=[jax.ShapeDtypeStruct((t, qw), bf16), jax.ShapeDtypeStruct((t, qw), bf16),
                   jax.ShapeDtypeStruct((t, vw), bf16)],
        compiler_params=_cparams(("arbitrary",), 48), name="mla_prep",
    )(p, p, p, kv_g.reshape(1, -1), wkvb_p, gq_pad.reshape(1, -1), gk_nope.reshape(1, -1),
      gk_rope.reshape(1, -1), cos_c, sin_c)


FLASH_COLS = 256


def _scores(k, qt_ref, s_ref):
    s_ref[0:k.shape[0], :] = jnp.dot(k, qt_ref[...], preferred_element_type=f32)


def _softmax_pv(s_ref, tk, v, m_ref, l_ref, acc_ref):
    vt = v.T
    step = min(FLASH_COLS, s_ref.shape[1])
    for c0 in range(0, s_ref.shape[1], step):
        cols = slice(c0, c0 + step)
        s = s_ref[0:tk, cols]
        m_prev = m_ref[:, cols]
        m_new = jnp.maximum(m_prev, jnp.max(s, axis=0, keepdims=True))
        alpha = jnp.exp2(m_prev - m_new)
        pr = jnp.exp2(s - m_new)
        l_ref[:, cols] = alpha * l_ref[:, cols] + jnp.sum(pr, axis=0, keepdims=True)
        acc_ref[:, cols] = alpha * acc_ref[:, cols] + jnp.dot(vt, pr.astype(bf16), preferred_element_type=f32)
        m_ref[:, cols] = m_new


def _flash_kernel(*refs, group, dq, dv, tq, tk, n_lat):
    if n_lat:
        q_ref, kl_ref, vl_ref, kc_ref, vc_ref, o_ref, qt_ref, m_ref, l_ref, acc_ref, s0_ref, s1_ref = refs
    else:
        q_ref, kc_ref, vc_ref, o_ref, qt_ref, m_ref, l_ref, acc_ref, s0_ref, s1_ref = refs
    for g in range(group):
        qt_ref[:, g * tq:(g + 1) * tq] = q_ref[:, g * dq:(g + 1) * dq].T
    m_ref[...] = jnp.full(m_ref.shape, -1e30, f32)
    l_ref[...] = jnp.zeros(l_ref.shape, f32)
    acc_ref[...] = jnp.zeros(acc_ref.shape, f32)
    n_ctx = kc_ref.shape[0]
    _scores(kc_ref[...], qt_ref, s1_ref)
    if n_lat:
        def kchunk(j):
            return kl_ref[j * tk:(j + 1) * tk, :]

        def vchunk(j):
            return vl_ref[j * tk:(j + 1) * tk, :]

        assert n_lat % 2 == 0
        _scores(kchunk(0), qt_ref, s0_ref)
        _softmax_pv(s1_ref, n_ctx, vc_ref[...], m_ref, l_ref, acc_ref)
        for i in range(n_lat // 2 - 1):
            _scores(kchunk(2 * i + 1), qt_ref, s1_ref)
            _softmax_pv(s0_ref, tk, vchunk(2 * i), m_ref, l_ref, acc_ref)
            _scores(kchunk(2 * i + 2), qt_ref, s0_ref)
            _softmax_pv(s1_ref, tk, vchunk(2 * i + 1), m_ref, l_ref, acc_ref)
        _scores(kchunk(n_lat - 1), qt_ref, s1_ref)
        _softmax_pv(s0_ref, tk, vchunk(n_lat - 2), m_ref, l_ref, acc_ref)
        _softmax_pv(s1_ref, tk, vchunk(n_lat - 1), m_ref, l_ref, acc_ref)
    else:
        _softmax_pv(s1_ref, n_ctx, vc_ref[...], m_ref, l_ref, acc_ref)
    for g in range(group):
        cols = slice(g * tq, (g + 1) * tq)
        o_ref[:, g * dv:(g + 1) * dv] = (acc_ref[:, cols] / l_ref[:, cols]).T.astype(o_ref.dtype)


def flash_attention(q, k, v, v_col_off, *, batch, seq, ctx_len, kv_heads, group, dq, dv, tq, tk, lat):
    ctx_blk0 = batch * seq // ctx_len
    voff = v_col_off // dv
    rows = group * tq
    kc_spec = pl.BlockSpec((ctx_len, dq), lambda b, h, i: (ctx_blk0 + b, h))
    vc_spec = pl.BlockSpec((ctx_len, dv), lambda b, h, i: (ctx_blk0 + b, voff + h))
    if lat:
        nq = seq // tq
        in_specs = [pl.BlockSpec((tq, group * dq), lambda b, h, i: (b * nq + i, h)),
                    pl.BlockSpec((seq, dq), lambda b, h, i: (b, h)),
                    pl.BlockSpec((seq, dv), lambda b, h, i: (b, voff + h)), kc_spec, vc_spec]
        args = [q, k, v, k, v]
        out_spec = pl.BlockSpec((tq, group * dv), lambda b, h, i: (b * nq + i, h))
        out_rows = batch * seq
        n_lat = seq // tk
    else:
        nq = 1
        assert tq == ctx_len
        in_specs = [pl.BlockSpec((tq, group * dq), lambda b, h, i: (ctx_blk0 + b, h)), kc_spec, vc_spec]
        args = [q, k, v]
        out_spec = pl.BlockSpec((tq, group * dv), lambda b, h, i: (b, h))
        out_rows = batch * ctx_len
        n_lat = 0
    return pl.pallas_call(
        functools.partial(_flash_kernel, group=group, dq=dq, dv=dv, tq=tq, tk=tk, n_lat=n_lat),
        grid=(batch, kv_heads, nq), in_specs=in_specs, out_specs=out_spec,
        out_shape=jax.ShapeDtypeStruct((out_rows, kv_heads * group * dv), bf16),
        scratch_shapes=[pltpu.VMEM((dq, rows), bf16), pltpu.VMEM((1, rows), f32), pltpu.VMEM((1, rows), f32),
                        pltpu.VMEM((dv, rows), f32)] + [pltpu.VMEM((max(tk, ctx_len) if lat else ctx_len, rows), f32)] * 2,
        compiler_params=_cparams(("arbitrary", "arbitrary", "arbitrary"), 48),
        name="flash_lat" if lat else "flash_ctx",
    )(*args)


CONV_HALO = 16


def _conv3_kernel(prev_ref, cur_ref, next_ref, w_ref, b_ref, o_ref, *, blocks_per_seq):
    rows = cur_ref.shape[0]
    pos = pl.program_id(0) % blocks_per_seq
    x = cur_ref[...].astype(f32)
    row = lax.broadcasted_iota(jnp.int32, x.shape, 0)
    prev_row = jnp.where(pos == 0, 0.0, prev_ref[...].astype(f32)[CONV_HALO - 1:CONV_HALO, :])
    next_row = jnp.where(pos == blocks_per_seq - 1, 0.0, next_ref[...].astype(f32)[0:1, :])
    xm = jnp.where(row == 0, prev_row, pltpu.roll(x, 1, 0))
    xp = jnp.where(row == rows - 1, next_row, pltpu.roll(x, rows - 1, 0))
    o_ref[...] = xm * w_ref[0:1, :] + x * w_ref[1:2, :] + xp * w_ref[2:3, :] + b_ref[...]


def short_conv(p, conv_w, conv_b, row0, nrows, seq_len, rows, cw):
    width = 3 * HY_C
    rb0 = row0 // rows
    sub = rows // CONV_HALO
    last_halo = p.shape[0] // CONV_HALO - 1
    c0 = HY_OFF // cw
    return pl.pallas_call(
        functools.partial(_conv3_kernel, blocks_per_seq=seq_len // rows),
        grid=(nrows // rows, width // cw),
        in_specs=[
            pl.BlockSpec((CONV_HALO, cw), lambda r, c: (jnp.maximum((rb0 + r) * sub - 1, 0), c0 + c)),
            pl.BlockSpec((rows, cw), lambda r, c: (rb0 + r, c0 + c)),
            pl.BlockSpec((CONV_HALO, cw), lambda r, c: (jnp.minimum((rb0 + r + 1) * sub, last_halo), c0 + c)),
            pl.BlockSpec((3, cw), lambda r, c: (0, c)),
            pl.BlockSpec((1, cw), lambda r, c: (0, c)),
        ],
        out_specs=pl.BlockSpec((rows, cw), lambda r, c: (r, c)),
        out_shape=jax.ShapeDtypeStruct((nrows, width), f32),
        compiler_params=_cparams(("arbitrary", "arbitrary"), 32), name="short_conv",
    )(p, p, p, conv_w, conv_b.reshape(1, width))


@functools.lru_cache(maxsize=None)
def _dft_tables(length):
    nb = 128 if length >= 1024 else 16
    n_fft = 2 * length
    na = n_fft // nb
    nah = na // 2
    lo = np.arange(nb, dtype=np.int64)[:, None, None]
    k1 = np.arange(na, dtype=np.int64)[None, :, None]
    hi = np.arange(nah, dtype=np.int64)[None, None, :]
    ang = 2.0 * np.pi * (((nb * hi + lo) * k1) % n_fft) / n_fft
    c, s = np.cos(ang), np.sin(ang)
    a1 = np.concatenate([np.concatenate([c, s], axis=2), np.concatenate([-s, c], axis=2)], axis=1)
    ct, st = np.swapaxes(c, 1, 2) / n_fft, np.swapaxes(s, 1, 2) / n_fft
    a3 = np.concatenate([np.concatenate([ct, -st], axis=2), np.concatenate([st, ct], axis=2)], axis=1)
    kk = np.arange(nb, dtype=np.int64)
    angb = 2.0 * np.pi * ((kk[:, None] * kk[None, :]) % nb) / nb
    cb, sb = np.cos(angb), np.sin(angb)
    mf = np.block([[cb, sb], [-sb, cb]])
    mfc = np.block([[cb, -sb], [sb, cb]])
    return dict(nb=nb, na=na, nah=nah,
                a1=jnp.asarray(a1, dtype=bf16), a1r=jnp.asarray(a1[:, :, :nah], dtype=bf16),
                a3=jnp.asarray(a3, dtype=bf16), mf=jnp.asarray(mf, dtype=bf16), mfc=jnp.asarray(mfc, dtype=bf16))


def _tap_times(j, nb, nah):
    ridx = lax.broadcasted_iota(jnp.int32, (V7X_SUBLANES * nah, 1), 0)
    return nb * (ridx & (nah - 1)) + (j * V7X_SUBLANES + (ridx >> (nah.bit_length() - 1)))


def _filter_mlp_kernel(band_ref, w1_ref, b1_ref, w2_ref, b2_ref, freq_ref, o_ref, *, length, nb, nah):
    hp = lax.Precision.HIGHEST
    t = _tap_times(pl.program_id(0), nb, nah).astype(f32)
    t_unit = t / float(max(length - 1, 1))
    lane = lax.broadcasted_iota(jnp.int32, (t.shape[0], V7X_LANES), 1)
    ang = ((2.0 * math.pi / length) * t) * band_ref[...]
    feats = jnp.where(lane == 0, t_unit,
                      jnp.where(lane <= HY_BANDS, jnp.cos(ang), jnp.where(lane <= 2 * HY_BANDS, -jnp.sin(ang), 0.0)))
    h = jnp.sin(freq_ref[0:1, :] * (jnp.dot(feats, w1_ref[...], precision=hp, preferred_element_type=f32) + b1_ref[...]))
    o_ref[...] = jnp.sin(freq_ref[1:2, :] * (jnp.dot(h, w2_ref[...], precision=hp, preferred_element_type=f32)
                                            + b2_ref[...]))


def filter_mlp(tabs, length, band_row, w1p, b1, w2, b2, freq):
    nb, nah = tabs["nb"], tabs["nah"]
    rows = V7X_SUBLANES * nah
    full = lambda *shape: pl.BlockSpec(shape, lambda j: (0,) * len(shape))
    return pl.pallas_call(
        functools.partial(_filter_mlp_kernel, length=length, nb=nb, nah=nah), grid=(nb // V7X_SUBLANES,),
        in_specs=[full(1, V7X_LANES), full(V7X_LANES, HY_FW), full(1, HY_FW), full(HY_FW, HY_FW), full(1, HY_FW),
                  full(2, HY_FW)],
        out_specs=pl.BlockSpec((rows, HY_FW), lambda j: (j, 0)),
        out_shape=jax.ShapeDtypeStruct((length, HY_FW), f32),
        compiler_params=_cparams(("arbitrary",), 32), name="hy_filter_mlp",
    )(band_row, w1p, b1, w2, b2, freq)


def _filter_stage1_kernel(a1r_ref, h_ref, w3_ref, delta_ref, o_ref, ss_ref, *, length, nb, nah):
    j = pl.program_id(1)
    t_int = _tap_times(j, nb, nah)
    t_unit = t_int.astype(f32) / float(max(length - 1, 1))
    decay = jnp.exp(-t_unit * delta_ref[...])
    hb = h_ref[...].astype(bf16)

    @pl.when(j == 0)
    def _():
        ss_ref[...] = jnp.zeros(ss_ref.shape, f32)

    na = 2 * nah
    for o in range(HY_ORDER):
        fwd = jnp.dot(hb, w3_ref[2 * o].astype(bf16), preferred_element_type=f32) * decay
        bwd = jnp.dot(hb, w3_ref[2 * o + 1].astype(bf16), preferred_element_type=f32) * decay
        bwd = jnp.where(t_int == 0, 0.0, bwd)
        ss_ref[o:o + 1, :] += jnp.sum(fwd * fwd + bwd * bwd, axis=0, keepdims=True)
        sb, db = (fwd + bwd).astype(bf16), (fwd - bwd).astype(bf16)
        for l in range(V7X_SUBLANES):
            a = a1r_ref[l]
            fs = jnp.dot(a, sb[l * nah:(l + 1) * nah], preferred_element_type=f32)
            fd = jnp.dot(a, db[l * nah:(l + 1) * nah], preferred_element_type=f32)
            o_ref[o, 0, l] = _pack_complex(fs[:na], fs[na:])
            o_ref[o, 1, l] = _pack_complex(fd[:na], fd[na:])


def filter_stage1(tabs, length, hmlp, w3r, delta_row):
    nb, na, nah = tabs["nb"], tabs["na"], tabs["nah"]
    ncb = HY_C // V7X_LANES
    return pl.pallas_call(
        functools.partial(_filter_stage1_kernel, length=length, nb=nb, nah=nah),
        grid=(ncb, nb // V7X_SUBLANES),
        in_specs=[pl.BlockSpec((V7X_SUBLANES, 2 * na, nah), lambda c, j: (j, 0, 0)),
                  pl.BlockSpec((V7X_SUBLANES * nah, HY_FW), lambda c, j: (j, 0)),
                  pl.BlockSpec((2 * HY_ORDER, HY_FW, V7X_LANES), lambda c, j: (0, 0, c)),
                  pl.BlockSpec((1, V7X_LANES), lambda c, j: (0, c))],
        out_specs=[pl.BlockSpec((None, HY_ORDER, 2, V7X_SUBLANES, na, V7X_LANES), lambda c, j: (c, 0, 0, j, 0, 0)),
                   pl.BlockSpec((V7X_SUBLANES, V7X_LANES), lambda c, j: (0, c))],
        out_shape=[jax.ShapeDtypeStruct((ncb, HY_ORDER, 2, nb, na, V7X_LANES), u32),
                   jax.ShapeDtypeStruct((V7X_SUBLANES, HY_C), f32)],
        compiler_params=_cparams(("arbitrary", "arbitrary"), 48), name="hy_filter_s1",
    )(tabs["a1r"], hmlp, w3r, delta_row)


def _tile_rows(ref, l):
    tiles = math.prod(ref.shape[:-2])
    flat = ref.reshape(tiles * V7X_SUBLANES, ref.shape[-1])
    return flat.at[pl.ds(l, tiles, stride=V7X_SUBLANES), :]


def _pack_complex(re, im):
    def rne(x):
        b = pltpu.bitcast(x, u32)
        return b + (jnp.uint32(0x7FFF) + ((b >> 16) & jnp.uint32(1)))
    return (rne(re) & jnp.uint32(0xFFFF0000)) | (rne(im) >> 16)


def _unpack_complex(w):
    return pltpu.bitcast(w & jnp.uint32(0xFFFF0000), f32), pltpu.bitcast(w << 16, f32)


def _filter_stage2_kernel(x_ref, m_ref, o_ref, *, nb):
    for kl in range(V7X_SUBLANES):
        re, im = _unpack_complex(_tile_rows(x_ref, kl)[...])
        rs = jnp.concatenate([re[:nb], im[:nb]], axis=0).astype(bf16)
        rd = jnp.concatenate([re[nb:], im[nb:]], axis=0).astype(bf16)
        h_re = jnp.dot(m_ref[0:nb, :], rs, preferred_element_type=f32)
        h_im = jnp.dot(m_ref[nb:, :], rd, preferred_element_type=f32)
        o_ref[kl] = _pack_complex(h_re, h_im)


def filter_stage2(tabs, hin):
    nb, na = tabs["nb"], tabs["na"]
    ncb = HY_C // V7X_LANES
    return pl.pallas_call(
        functools.partial(_filter_stage2_kernel, nb=nb), grid=(HY_ORDER, ncb, na // V7X_SUBLANES),
        in_specs=[pl.BlockSpec((None, None, 2, nb, V7X_SUBLANES, V7X_LANES), lambda o, c, j: (c, o, 0, 0, j, 0)),
                  pl.BlockSpec((2 * nb, 2 * nb), lambda o, c, j: (0, 0))],
        out_specs=pl.BlockSpec((None, None, V7X_SUBLANES, nb, V7X_LANES), lambda o, c, j: (c, o, j, 0, 0)),
        out_shape=jax.ShapeDtypeStruct((ncb, HY_ORDER, na, nb, V7X_LANES), u32),
        compiler_params=_cparams(("arbitrary", "arbitrary", "arbitrary"), 48), name="hy_filter_s2",
    )(hin, tabs["mf"])


def _conv_stage1_kernel(zr_ref, zi_ref, a1_ref, o_ref, *, na):
    for l in range(V7X_SUBLANES):
        rhs = jnp.concatenate([_tile_rows(zr_ref, l)[...], _tile_rows(zi_ref, l)[...]], axis=0).astype(bf16)
        y = jnp.dot(a1_ref[l], rhs, preferred_element_type=f32)
        o_ref[l] = _pack_complex(y[:na], y[na:])


def conv_stage1(tabs, z4, col_off):
    nb, na, nah = tabs["nb"], tabs["na"], tabs["nah"]
    ncb = HY_C // V7X_LANES
    c0 = col_off // V7X_LANES
    return pl.pallas_call(
        functools.partial(_conv_stage1_kernel, na=na), grid=(ncb, nb // V7X_SUBLANES),
        in_specs=[pl.BlockSpec((None, nah, V7X_SUBLANES, V7X_LANES), lambda c, j: (0, 0, j, c0 + c)),
                  pl.BlockSpec((None, nah, V7X_SUBLANES, V7X_LANES), lambda c, j: (1, 0, j, c0 + c)),
                  pl.BlockSpec((V7X_SUBLANES, 2 * na, na), lambda c, j: (j, 0, 0))],
        out_specs=pl.BlockSpec((None, V7X_SUBLANES, na, V7X_LANES), lambda c, j: (c, j, 0, 0)),
        out_shape=jax.ShapeDtypeStruct((ncb, nb, na, V7X_LANES), u32),
        compiler_params=_cparams(("arbitrary", "arbitrary"), 48), name="hy_conv_s1",
    )(z4, z4, tabs["a1"])


def _conv_stage2_kernel(x_ref, h_ref, mf_ref, mfc_ref, o_ref, *, nb):
    def lanes(a):
        return jnp.concatenate([a[:nb], a[nb:]], axis=1)

    for kl in range(V7X_SUBLANES):
        re, im = _unpack_complex(_tile_rows(x_ref, kl)[...])
        rhs = jnp.concatenate([lanes(re), lanes(im)], axis=0).astype(bf16)
        x = jnp.dot(mf_ref[...], rhs, preferred_element_type=f32)
        xr, xi = x[:nb], x[nb:]
        hr0, hi0 = _unpack_complex(h_ref[0, kl])
        hr1, hi1 = _unpack_complex(h_ref[1, kl])
        hr, hi = jnp.concatenate([hr0, hr1], axis=1), jnp.concatenate([hi0, hi1], axis=1)
        y = jnp.concatenate([xr * hr - xi * hi, xr * hi + xi * hr], axis=0).astype(bf16)
        e = jnp.dot(mfc_ref[...], y, preferred_element_type=f32)
        packed = _pack_complex(e[:nb], e[nb:])
        o_ref[0, kl] = packed[:, :V7X_LANES]
        o_ref[1, kl] = packed[:, V7X_LANES:]


def conv_stage2(tabs, din, hspec, order):
    nb, na = tabs["nb"], tabs["na"]
    ncb = HY_C // V7X_LANES
    return pl.pallas_call(
        functools.partial(_conv_stage2_kernel, nb=nb), grid=(ncb // 2, na // V7X_SUBLANES),
        in_specs=[pl.BlockSpec((2, nb, V7X_SUBLANES, V7X_LANES), lambda c, j: (c, 0, j, 0)),
                  pl.BlockSpec((2, None, V7X_SUBLANES, nb, V7X_LANES), lambda c, j: (c, order, j, 0, 0)),
                  pl.BlockSpec((2 * nb, 2 * nb), lambda c, j: (0, 0)),
                  pl.BlockSpec((2 * nb, 2 * nb), lambda c, j: (0, 0))],
        out_specs=pl.BlockSpec((2, V7X_SUBLANES, nb, V7X_LANES), lambda c, j: (c, j, 0, 0)),
        out_shape=jax.ShapeDtypeStruct((ncb, na, nb, V7X_LANES), u32),
        compiler_params=_cparams(("arbitrary", "arbitrary"), 48), name="hy_conv_s2",
    )(din, hspec, tabs["mf"], tabs["mfc"])


def _conv_stage3_kernel(e_ref, a3_ref, zr_ref, zi_ref, gr_ref, gi_ref, ss_ref, skip_ref, o_ref, *, order):
    rs = lax.rsqrt(ss_ref[order:order + 1, :] + EPS)
    skip = skip_ref[order:order + 1, :]
    for l in range(V7X_SUBLANES):
        re, im = _unpack_complex(_tile_rows(e_ref, l)[...])
        rhs = jnp.concatenate([re, im], axis=0).astype(bf16)
        y = jnp.dot(a3_ref[l], rhs, preferred_element_type=f32) * rs
        z = jnp.concatenate([_tile_rows(zr_ref, l)[...], _tile_rows(zi_ref, l)[...]], axis=0)
        g = jnp.concatenate([_tile_rows(gr_ref, l)[...], _tile_rows(gi_ref, l)[...]], axis=0)
        _tile_rows(o_ref, l)[...] = g * (y + skip * z)


def conv_stage3(tabs, ein, z4, z_off, g4, g_off, ss, skip, order):
    nb, na, nah = tabs["nb"], tabs["na"], tabs["nah"]
    ncb = HY_C // V7X_LANES
    zc, gc = z_off // V7X_LANES, g_off // V7X_LANES

    def slab(b, c0):
        return pl.BlockSpec((None, nah, V7X_SUBLANES, V7X_LANES), lambda c, j: (b, 0, j, c0 + c))

    return pl.pallas_call(
        functools.partial(_conv_stage3_kernel, order=order), grid=(ncb, nb // V7X_SUBLANES),
        in_specs=[pl.BlockSpec((None, na, V7X_SUBLANES, V7X_LANES), lambda c, j: (c, 0, j, 0)),
                  pl.BlockSpec((V7X_SUBLANES, na, 2 * na), lambda c, j: (j, 0, 0)),
                  slab(0, zc), slab(1, zc), slab(0, gc), slab(1, gc),
                  pl.BlockSpec((V7X_SUBLANES, V7X_LANES), lambda c, j: (0, c)),
                  pl.BlockSpec((HY_ORDER, V7X_LANES), lambda c, j: (0, c))],
        out_specs=pl.BlockSpec((2, nah, V7X_SUBLANES, V7X_LANES), lambda c, j: (0, 0, j, c)),
        out_shape=jax.ShapeDtypeStruct((2, nah, nb, HY_C), f32),
        compiler_params=_cparams(("arbitrary", "arbitrary"), 48), name="hy_conv_s3",
    )(ein, tabs["a3"], z4, z4, g4, g4, ss, skip)


def hyena_mixer(uc, length, hf, skip):
    tabs = _dft_tables(length)
    nb, nah = tabs["nb"], tabs["nah"]
    band_row, w1p, b1, w2, b2, w3r, freq, delta_row = hf
    hmlp = filter_mlp(tabs, length, band_row, w1p, b1, w2, b2, freq)
    hin, ss = filter_stage1(tabs, length, hmlp, w3r, delta_row)
    hspec = filter_stage2(tabs, hin)
    u4 = uc.reshape(2, nah, nb, 3 * HY_C)
    z4, z_off = u4, 0
    for o in range(HY_ORDER):
        din = conv_stage1(tabs, z4, z_off)
        ein = conv_stage2(tabs, din, hspec, o)
        z4 = conv_stage3(tabs, ein, z4, z_off, u4, (o + 1) * HY_C, ss, skip, o)
        z_off = 0
    return z4.reshape(2 * length, HY_C)


def _merge_kernel(gl_ref, al_ref, ac_ref, hl_ref, hc_ref, cl_ref, cc_ref, wg_ref, bg_ref, wb_ref, wo_ref,
                  o_ref, acc_ref, br_ref, *, lat_blocks):
    i, j = pl.program_id(0), pl.program_id(1)

    @pl.when(j == 0)
    def _():
        acc_ref[...] = jnp.zeros(acc_ref.shape, f32)

    @pl.when((j == 0) & (i < lat_blocks))
    def _():
        br_ref[0] = al_ref[...]
        br_ref[1] = hl_ref[...].astype(bf16)
        br_ref[2] = cl_ref[...]

    @pl.when((j == 0) & (i >= lat_blocks))
    def _():
        br_ref[0] = ac_ref[...]
        br_ref[1] = hc_ref[...].astype(bf16)
        br_ref[2] = cc_ref[...]

    gl = gl_ref[...]
    y = None
    for n in range(N_BRANCH):
        z = jnp.dot(gl, wg_ref[n], preferred_element_type=f32) + bg_ref[n]
        g = 1.0 / (1.0 + jnp.exp(-z))
        term = g * jnp.dot(br_ref[n], wb_ref[n], preferred_element_type=f32)
        y = term if y is None else y + term
    acc_ref[...] += jnp.dot(y.astype(bf16), wo_ref[...], preferred_element_type=f32)

    @pl.when(j == pl.num_programs(1) - 1)
    def _():
        o_ref[...] = acc_ref[...].astype(o_ref.dtype)


def merge_out(p, a, hy, c, wg, bg, wb, wo, layer, tm, tj):
    t = p.shape[0]
    d = wo.shape[2]
    nl = a[0].shape[0] // tm
    assert a[1].shape[0] % tm == 0 and t == a[0].shape[0] + a[1].shape[0]
    lat = pl.BlockSpec((tm, BRANCH_W), lambda i, j: (jnp.minimum(i, nl - 1), 0))
    ctx = pl.BlockSpec((tm, BRANCH_W), lambda i, j: (jnp.maximum(i - nl, 0), 0))
    return pl.pallas_call(
        functools.partial(_merge_kernel, lat_blocks=nl), grid=(t // tm, d // tj),
        in_specs=[pl.BlockSpec((tm, GATE_RANK), lambda i, j: (i, GATE_OFF // GATE_RANK)),
                  lat, ctx, lat, ctx, lat, ctx,
                  pl.BlockSpec((None, N_BRANCH, GATE_RANK, tj), lambda i, j: (layer, 0, 0, j)),
                  pl.BlockSpec((None, N_BRANCH, 1, tj), lambda i, j: (layer, 0, 0, j)),
                  pl.BlockSpec((None, N_BRANCH, BRANCH_W, tj), lambda i, j: (layer, 0, 0, j)),
                  pl.BlockSpec((None, tj, d), lambda i, j: (layer, j, 0))],
        out_specs=pl.BlockSpec((tm, d), lambda i, j: (i, 0)),
        out_shape=jax.ShapeDtypeStruct((t, d), bf16),
        scratch_shapes=[pltpu.VMEM((tm, d), f32), pltpu.VMEM((N_BRANCH, tm, BRANCH_W), bf16)],
        compiler_params=_cparams(("arbitrary", "arbitrary"), 58), name="merge_out",
    )(p, a[0], a[1], hy[0], hy[1], c[0], c[1], wg, bg, wb, wo)


def _ffn_kernel(h_ref, wg_ref, wu_ref, wd_ref, o_ref, acc_ref):
    j = pl.program_id(1)

    @pl.when(j == 0)
    def _():
        acc_ref[...] = jnp.zeros(acc_ref.shape, f32)

    h = h_ref[...]
    g = jnp.dot(h, wg_ref[...], preferred_element_type=f32)
    u = jnp.dot(h, wu_ref[...], preferred_element_type=f32)
    a = (g * (1.0 / (1.0 + jnp.exp(-g))) * u).astype(bf16)
    acc_ref[...] += jnp.dot(a, wd_ref[...], preferred_element_type=f32)

    @pl.when(j == pl.num_programs(1) - 1)
    def _():
        o_ref[...] = acc_ref[...].astype(o_ref.dtype)


def ffn(h, w_gu, w_down, layer, tm, th):
    t, d = h.shape
    hidden = w_down.shape[1]
    nh = hidden // th
    return pl.pallas_call(
        _ffn_kernel, grid=(t // tm, nh),
        in_specs=[pl.BlockSpec((tm, d), lambda i, j: (i, 0)),
                  pl.BlockSpec((None, d, th), lambda i, j: (layer, 0, j)),
                  pl.BlockSpec((None, d, th), lambda i, j: (layer, 0, nh + j)),
                  pl.BlockSpec((None, th, d), lambda i, j: (layer, j, 0))],
        out_specs=pl.BlockSpec((tm, d), lambda i, j: (i, 0)),
        out_shape=jax.ShapeDtypeStruct((t, d), bf16),
        scratch_shapes=[pltpu.VMEM((tm, d), f32)],
        compiler_params=_cparams(("arbitrary", "arbitrary"), 56), name="ffn",
    )(h, w_gu, w_gu, w_down)


def _pack_w_in(w_in):
    depth, d, _ = w_in.shape
    o_qa, o_ka, o_hy = 0, GQA_HEADS * HEAD_DIM, (GQA_HEADS + 2 * GQA_KV_HEADS) * HEAD_DIM
    o_qc = o_hy + 3 * HY_C
    o_ckv = o_qc + MLA_HEADS * MLA_QK
    o_kr = o_ckv + MLA_RANK
    o_gate = o_kr + MLA_ROPE
    qc = w_in[:, :, o_qc:o_ckv].reshape(depth, d, MLA_HEADS, MLA_QK)
    qc = jnp.pad(qc, ((0, 0), (0, 0), (0, 0), (0, MLA_QPAD - MLA_QK))).reshape(depth, d, MLA_HEADS * MLA_QPAD)
    tail = NP_COLS - KR_OFF - MLA_ROPE
    parts = [qc, w_in[:, :, o_qa:o_ka], w_in[:, :, o_ka:o_hy], w_in[:, :, o_hy:o_qc], w_in[:, :, o_ckv:o_kr],
             w_in[:, :, o_gate:o_gate + GATE_RANK], w_in[:, :, o_kr:o_gate], jnp.zeros((depth, d, tail), w_in.dtype)]
    return jnp.concatenate(parts, axis=2).astype(bf16)


def _pack_w_kvb(w):
    depth = w.shape[0]
    w = w.reshape(depth, MLA_RANK, MLA_HEADS, 2, MLA_NOPE)
    return jnp.swapaxes(w, 2, 3).reshape(depth, MLA_RANK, 2 * MLA_HEADS * MLA_NOPE).astype(bf16)


def kernel(x, c, ctx, c_ctx, norm1_g, norm2_g, ada_down, ada_up, ada_b, w_in, gqa_q_norm, gqa_k_norm, hy_conv_w,
           hy_conv_b, hf_w1, hf_b1, hf_w2, hf_b2, hf_w3, hf_freq, hy_skip, mla_kv_norm, mla_w_kvb, mla_q_norm,
           mla_k_norm, w_gate_up, b_gate, w_branch, w_out, ffn_w_gu, ffn_w_down):
    batch, seq, d = x.shape
    ctx_len = ctx.shape[1]
    depth = w_in.shape[0]
    hidden = ffn_w_down.shape[1]
    assert batch == 2, "the Hyena long convolution packs the two batches as one complex sequence"
    assert w_in.shape[2] == IN_COLS and seq % GRID_W == 0
    n_lat, n_ctx = batch * seq, batch * ctx_len
    t_all = n_lat + n_ctx

    tm = _row_tile(seq, n_ctx, (512, 256, 128))
    tm_ew = _row_tile(seq, n_ctx, (256, 128))
    tm_mm = 768 if t_all % 768 == 0 else tm
    tm_in, tn_in = tm_mm, 1536
    tj = 512
    th = 256
    tq_a = min(256, ctx_len)
    tq_c = min(1024, seq)
    tk = min(512, seq)

    def grp(tile):
        per = seq // tile
        return lambda i: jnp.minimum(i // per, batch)

    w_in_p = _pack_w_in(w_in)
    w_kvb_p = _pack_w_kvb(mla_w_kvb)
    wg_b, wb_b, wo_b = w_gate_up.astype(bf16), w_branch.astype(bf16), w_out.astype(bf16)
    wgu_b, wd_b = ffn_w_gu.astype(bf16), ffn_w_down.astype(bf16)
    bg = b_gate.reshape(depth, N_BRANCH, 1, d)
    gq_pad = jnp.pad(mla_q_norm, ((0, 0), (0, MLA_QPAD - MLA_QK)))
    gk_nope = mla_k_norm[:, :MLA_NOPE]
    gk_rope = jnp.pad(mla_k_norm[:, MLA_NOPE:], ((0, 0), (0, V7X_LANES - MLA_ROPE)))
    w1p = jnp.pad(hf_w1, ((0, 0), (0, V7X_LANES - HY_EMB), (0, 0)))
    w3r = jnp.swapaxes(hf_w3.reshape(depth, HY_FW, 2 * HY_ORDER, HY_C), 1, 2)
    bands = jnp.linspace(1e-4, HY_BANDS - 1, HY_BANDS, dtype=f32)
    band_row = jnp.concatenate([jnp.zeros((1,), f32), bands, bands,
                                jnp.zeros((V7X_LANES - HY_EMB,), f32)]).reshape(1, V7X_LANES)
    delta_row = jnp.abs(jnp.linspace(math.log(HY_DECAY_TARGET) / HY_DECAY_PCT_MIN,
                                     math.log(HY_DECAY_TARGET) / HY_DECAY_PCT_MAX, HY_C, dtype=f32)).reshape(1, HY_C)

    xs = jnp.concatenate([x.reshape(n_lat, d), ctx.reshape(n_ctx, d)], axis=0)
    tpos = jnp.arange(seq, dtype=jnp.int32)
    zpad = jnp.zeros((n_ctx,), f32)
    pos_row = jnp.concatenate([jnp.tile((tpos // GRID_W).astype(f32), batch), zpad]).reshape(t_all, 1)
    pos_col = jnp.concatenate([jnp.tile((tpos % GRID_W).astype(f32), batch), zpad]).reshape(t_all, 1)
    cond8 = jnp.concatenate([c, c_ctx[None, :], jnp.zeros((8 - batch - 1, d), f32)], axis=0)

    mod = ada_modulation(cond8, ada_down, ada_up, ada_b)
    mod5 = mod.reshape(depth, 8, N_MOD, 1, d)
    cos_a, sin_a, cos_c, sin_c = rope_tables(pos_row, pos_col, tm_ew)

    delta = None
    for l in range(depth):
        if l == 0:
            (h,) = prenorm(xs, mod5, l, grp(tm_ew), tm_ew, norm_g=norm1_g[l], shift_idx=0, scale_idx=1)
        else:
            xs, h = prenorm(xs, mod5, l, grp(tm_ew), tm_ew, delta=delta, gate_idx=5, gate_layer=l - 1,
                            norm_g=norm1_g[l], shift_idx=0, scale_idx=1)
        p = matmul(h, w_in_p, l, tm_in, tn_in)

        qa, ka = gqa_prep(p, gqa_q_norm[l], gqa_k_norm[l], cos_a, sin_a, tm_ew)
        fa = dict(batch=batch, seq=seq, ctx_len=ctx_len, kv_heads=GQA_KV_HEADS, group=GQA_GROUP,
                  dq=HEAD_DIM, dv=HEAD_DIM, tk=tk)
        att_a = (flash_attention(qa, ka, p, VA_OFF, tq=tq_a, lat=True, **fa),
                 flash_attention(qa, ka, p, VA_OFF, tq=ctx_len, lat=False, **fa))

        qc, kc, vc = mla_prep(p, mla_kv_norm[l], w_kvb_p[l], gq_pad[l], gk_nope[l], gk_rope[l], cos_c, sin_c, tm_ew)
        fc = dict(batch=batch, seq=seq, ctx_len=ctx_len, kv_heads=MLA_HEADS, group=1, dq=MLA_QPAD, dv=MLA_V, tk=tk)
        att_c = (flash_attention(qc, kc, vc, 0, tq=tq_c, lat=True, **fc),
                 flash_attention(qc, kc, vc, 0, tq=ctx_len, lat=False, **fc))

        hf = (band_row, w1p[l], hf_b1[l].reshape(1, -1), hf_w2[l], hf_b2[l].reshape(1, -1), w3r[l], hf_freq[l],
              delta_row)
        uc_lat = short_conv(p, hy_conv_w[l], hy_conv_b[l], 0, n_lat, seq, min(512, seq), 512)
        uc_ctx = short_conv(p, hy_conv_w[l], hy_conv_b[l], n_lat, n_ctx, ctx_len, min(256, ctx_len), 512)
        hy = (hyena_mixer(uc_lat, seq, hf, hy_skip[l]), hyena_mixer(uc_ctx, ctx_len, hf, hy_skip[l]))

        delta = merge_out(p, att_a, hy, att_c, wg_b, bg, wb_b, wo_b, l, tm, tj)
        xs, h2 = prenorm(xs, mod5, l, grp(tm_ew), tm_ew, delta=delta, gate_idx=2, gate_layer=l,
                         norm_g=norm2_g[l], shift_idx=3, scale_idx=4)
        delta = ffn(h2, wgu_b, wd_b, l, tm_mm, th)

    (out,) = prenorm(xs, mod5, depth - 1, grp(tm_ew), tm_ew, delta=delta, gate_idx=5, gate_layer=depth - 1,
                     rows=n_lat)
    return out.reshape(batch, seq, d)
```

```python
import functools
import math

import jax
import jax.numpy as jnp
import numpy as np
from jax import lax
from jax.experimental import pallas as pl
from jax.experimental.pallas import tpu as pltpu

f32 = jnp.float32
bf16 = jnp.bfloat16
u32 = jnp.uint32

GRID_W = 64
ROPE_THETA = 10000.0
EPS = 1e-6
HEAD_DIM = 128
GQA_HEADS = 8
GQA_KV_HEADS = 2
GQA_GROUP = GQA_HEADS // GQA_KV_HEADS
HY_C = 1024
HY_ORDER = 2
HY_EMB = 33
HY_BANDS = (HY_EMB - 1) // 2
HY_FW = 64
HY_DECAY_TARGET = 1e-2
HY_DECAY_PCT_MIN = 0.3
HY_DECAY_PCT_MAX = 1.5
MLA_HEADS = 8
MLA_NOPE = 128
MLA_ROPE = 64
MLA_QK = MLA_NOPE + MLA_ROPE
MLA_V = 128
MLA_RANK = 512
N_BRANCH = 3
BRANCH_W = 1024
GATE_RANK = 256
ADA_RANK = 256
N_MOD = 6
IN_COLS = 6976

V7X_LANES = 128
V7X_SUBLANES = 8
V7X_VMEM_BYTES = 64 * 1024 * 1024
MIB = 1024 * 1024

MLA_QPAD = 2 * V7X_LANES
QC_OFF = 0
QA_OFF = QC_OFF + MLA_HEADS * MLA_QPAD
KA_OFF = QA_OFF + GQA_HEADS * HEAD_DIM
VA_OFF = KA_OFF + GQA_KV_HEADS * HEAD_DIM
HY_OFF = VA_OFF + GQA_KV_HEADS * HEAD_DIM
CKV_OFF = HY_OFF + 3 * HY_C
GATE_OFF = CKV_OFF + MLA_RANK
KR_OFF = GATE_OFF + GATE_RANK
NP_COLS = 7680

LOG2E = math.log2(math.e)


def _cparams(sem, vmem_mib):
    return pltpu.CompilerParams(dimension_semantics=sem, vmem_limit_bytes=int(vmem_mib * MIB))


def _row_tile(s, nctx, cands):
    for t in cands:
        if s % t == 0 and nctx % t == 0:
            return t
    raise ValueError("no row tile fits")


def _ada_kernel(cond_ref, down_ref, up_ref, b_ref, o_ref, t_ref):
    @pl.when(pl.program_id(1) == 0)
    def _():
        c = cond_ref[...]
        c = c * (1.0 / (1.0 + jnp.exp(-c)))
        t_ref[...] = jnp.dot(c, down_ref[...], precision=lax.Precision.HIGHEST, preferred_element_type=f32)

    o_ref[...] = jnp.dot(t_ref[...], up_ref[...], precision=lax.Precision.HIGHEST,
                         preferred_element_type=f32) + b_ref[...]


def ada_modulation(cond8, ada_down, ada_up, ada_b):
    depth, d, _ = ada_down.shape
    n = ada_up.shape[2]
    tn = 2048 if n % 2048 == 0 else 512
    return pl.pallas_call(
        _ada_kernel,
        grid=(depth, n // tn),
        in_specs=[
            pl.BlockSpec((8, d), lambda l, j: (0, 0)),
            pl.BlockSpec((None, d, ADA_RANK), lambda l, j: (l, 0, 0)),
            pl.BlockSpec((None, ADA_RANK, tn), lambda l, j: (l, 0, j)),
            pl.BlockSpec((None, 1, tn), lambda l, j: (l, 0, j)),
        ],
        out_specs=pl.BlockSpec((None, 8, tn), lambda l, j: (l, 0, j)),
        out_shape=jax.ShapeDtypeStruct((depth, 8, n), f32),
        scratch_shapes=[pltpu.VMEM((8, ADA_RANK), f32)],
        compiler_params=_cparams(("arbitrary", "arbitrary"), 40),
        name="ada_modulation",
    )(cond8, ada_down, ada_up, ada_b.reshape(depth, 1, n))


def _rope_table_kernel(pr_ref, pc_ref, ca_ref, sa_ref, cc_ref, sc_ref):
    shape = ca_ref.shape
    lane = lax.broadcasted_iota(jnp.int32, shape, 1)
    pr = jnp.broadcast_to(pr_ref[...], shape)
    pc = jnp.broadcast_to(pc_ref[...], shape)
    log_theta = math.log(ROPE_THETA)
    fa = jnp.exp((lane & 31).astype(f32) * (-log_theta / 32.0))
    ang = jnp.where(lane < 64, pr, pc) * fa
    ca_ref[...] = jnp.cos(ang)
    sa_ref[...] = jnp.where((lane & 63) < 32, -1.0, 1.0) * jnp.sin(ang)
    fc = jnp.exp((lane & 15).astype(f32) * (-log_theta / 16.0))
    angc = jnp.where(lane < 64, jnp.where(lane < 32, pr, pc) * fc, 0.0)
    cc_ref[...] = jnp.cos(angc)
    sc_ref[...] = jnp.where((lane & 31) < 16, -1.0, 1.0) * jnp.sin(angc)


def rope_tables(pos_row, pos_col, tm):
    t = pos_row.shape[0]
    spec1 = pl.BlockSpec((tm, 1), lambda i: (i, 0))
    spec = pl.BlockSpec((tm, V7X_LANES), lambda i: (i, 0))
    sh = jax.ShapeDtypeStruct((t, V7X_LANES), f32)
    return pl.pallas_call(
        _rope_table_kernel, grid=(t // tm,), in_specs=[spec1, spec1], out_specs=[spec] * 4,
        out_shape=[sh] * 4, compiler_params=_cparams(("arbitrary",), 32), name="rope_tables",
    )(pos_row, pos_col)


def _prenorm_kernel(*refs, has_delta, want_h):
    if has_delta:
        x_ref, d_ref, gate_ref = refs[:3]
        rest = refs[3:]
    else:
        x_ref = refs[0]
        rest = refs[1:]
    x = x_ref[...]
    if has_delta:
        x = x + gate_ref[...] * d_ref[...].astype(f32)
    if want_h:
        g_ref, shift_ref, scale_ref = rest[:3]
        outs = rest[3:]
    else:
        outs = rest
    k = 0
    if has_delta:
        outs[k][...] = x
        k += 1
    if want_h:
        y = x * lax.rsqrt(jnp.mean(x * x, axis=-1, keepdims=True) + EPS)
        y = y * g_ref[...]
        outs[k][...] = (y * (1.0 + scale_ref[...]) + shift_ref[...]).astype(bf16)


def prenorm(x, mod5, layer, grp_of_block, tm, *, delta=None, gate_idx=None, gate_layer=None,
            norm_g=None, shift_idx=None, scale_idx=None, rows=None):
    t, d = x.shape
    rows = t if rows is None else rows
    has_delta = delta is not None
    want_h = norm_g is not None
    row_spec = pl.BlockSpec((tm, d), lambda i: (i, 0))

    def mod_spec(lyr, which):
        return pl.BlockSpec((None, None, None, 1, d), lambda i: (lyr, grp_of_block(i), which, 0, 0))

    in_specs, args = [row_spec], [x]
    if has_delta:
        in_specs += [row_spec, mod_spec(gate_layer, gate_idx)]
        args += [delta, mod5]
    if want_h:
        in_specs += [pl.BlockSpec((1, d), lambda i: (0, 0)), mod_spec(layer, shift_idx), mod_spec(layer, scale_idx)]
        args += [norm_g.reshape(1, d), mod5, mod5]
    out_specs, out_shape = [], []
    if has_delta:
        out_specs.append(row_spec)
        out_shape.append(jax.ShapeDtypeStruct((rows, d), f32))
    if want_h:
        out_specs.append(row_spec)
        out_shape.append(jax.ShapeDtypeStruct((rows, d), bf16))
    return pl.pallas_call(
        functools.partial(_prenorm_kernel, has_delta=has_delta, want_h=want_h),
        grid=(rows // tm,), in_specs=in_specs, out_specs=out_specs, out_shape=out_shape,
        compiler_params=_cparams(("arbitrary",), 48), name="prenorm",
    )(*args)


def _matmul_kernel(a_ref, w_ref, o_ref):
    o_ref[...] = jnp.dot(a_ref[...], w_ref[...], preferred_element_type=f32).astype(o_ref.dtype)


def matmul(a, w, layer, tm, tn):
    t, k = a.shape
    n = w.shape[2]
    return pl.pallas_call(
        _matmul_kernel, grid=(t // tm, n // tn),
        in_specs=[pl.BlockSpec((tm, k), lambda i, j: (i, 0)),
                  pl.BlockSpec((None, k, tn), lambda i, j: (layer, 0, j))],
        out_specs=pl.BlockSpec((tm, tn), lambda i, j: (i, j)),
        out_shape=jax.ShapeDtypeStruct((t, n), bf16),
        compiler_params=_cparams(("arbitrary", "arbitrary"), 48), name="in_proj",
    )(a, w)


def _rope128(x, cos, sin_signed, half):
    lane = lax.broadcasted_iota(jnp.int32, x.shape, 1)
    first = (lane & (2 * half - 1)) < half
    swapped = jnp.where(first, pltpu.roll(x, V7X_LANES - half, 1), pltpu.roll(x, half, 1))
    return x * cos + swapped * sin_signed


def _gqa_prep_kernel(q_ref, kv_ref, gq_ref, gk_ref, cos_ref, sin_ref, qo_ref, ko_ref):
    cos, sin = cos_ref[...], sin_ref[...]
    qscale = HEAD_DIM ** -0.5 * LOG2E
    for h in range(GQA_HEADS):
        sl = slice(h * HEAD_DIM, (h + 1) * HEAD_DIM)
        q = q_ref[:, sl].astype(f32)
        q = q * lax.rsqrt(jnp.mean(q * q, axis=-1, keepdims=True) + EPS) * gq_ref[...]
        qo_ref[:, sl] = (_rope128(q, cos, sin, 32) * qscale).astype(bf16)
    for h in range(GQA_KV_HEADS):
        sl = slice(h * HEAD_DIM, (h + 1) * HEAD_DIM)
        k = kv_ref[:, sl].astype(f32)
        k = k * lax.rsqrt(jnp.mean(k * k, axis=-1, keepdims=True) + EPS) * gk_ref[...]
        ko_ref[:, sl] = _rope128(k, cos, sin, 32).astype(bf16)


def gqa_prep(p, gq, gk, cos_a, sin_a, tm):
    t = p.shape[0]
    qw, kw = GQA_HEADS * HEAD_DIM, GQA_KV_HEADS * HEAD_DIM
    tab = pl.BlockSpec((tm, V7X_LANES), lambda i: (i, 0))
    vec = pl.BlockSpec((1, HEAD_DIM), lambda i: (0, 0))
    return pl.pallas_call(
        _gqa_prep_kernel, grid=(t // tm,),
        in_specs=[pl.BlockSpec((tm, qw), lambda i: (i, QA_OFF // qw)),
                  pl.BlockSpec((tm, 2 * kw), lambda i: (i, KA_OFF // (2 * kw))), vec, vec, tab, tab],
        out_specs=[pl.BlockSpec((tm, qw), lambda i: (i, 0)), pl.BlockSpec((tm, kw), lambda i: (i, 0))],
        out_shape=[jax.ShapeDtypeStruct((t, qw), bf16), jax.ShapeDtypeStruct((t, kw), bf16)],
        compiler_params=_cparams(("arbitrary",), 32), name="gqa_prep",
    )(p, p, gq.reshape(1, -1), gk.reshape(1, -1), cos_a, sin_a)


def _mla_prep_kernel(q_ref, ckv_ref, kr_ref, kvg_ref, wkvb_ref, gq_ref, gkn_ref, gkr_ref, cos_ref, sin_ref,
                     qo_ref, ko_ref, vo_ref):
    cos, sin = cos_ref[...], sin_ref[...]
    qscale = MLA_QK ** -0.5 * LOG2E
    inv_qk = 1.0 / MLA_QK
    for h in range(MLA_HEADS):
        lo = h * MLA_QPAD
        qn = q_ref[:, lo:lo + V7X_LANES].astype(f32)
        qr = q_ref[:, lo + V7X_LANES:lo + MLA_QPAD].astype(f32)
        ss = jnp.sum(qn * qn, axis=-1, keepdims=True) + jnp.sum(qr * qr, axis=-1, keepdims=True)
        r = lax.rsqrt(ss * inv_qk + EPS) * qscale
        qo_ref[:, lo:lo + V7X_LANES] = (qn * r * gq_ref[:, :V7X_LANES]).astype(bf16)
        qo_ref[:, lo + V7X_LANES:lo + MLA_QPAD] = (_rope128(qr * gq_ref[:, V7X_LANES:], cos, sin, 16) * r).astype(bf16)
    c = ckv_ref[...].astype(f32)
    cn = c * lax.rsqrt(jnp.mean(c * c, axis=-1, keepdims=True) + EPS) * kvg_ref[...]
    kv = jnp.dot(cn.astype(bf16), wkvb_ref[...], preferred_element_type=f32)
    kr = kr_ref[...].astype(f32)
    ss_r = jnp.sum(kr * kr, axis=-1, keepdims=True)
    kr_rot = _rope128(kr * gkr_ref[...], cos, sin, 16)
    nv = MLA_HEADS * MLA_NOPE
    for h in range(MLA_HEADS):
        kn = kv[:, h * MLA_NOPE:(h + 1) * MLA_NOPE]
        r = lax.rsqrt((jnp.sum(kn * kn, axis=-1, keepdims=True) + ss_r) * inv_qk + EPS)
        lo = h * MLA_QPAD
        ko_ref[:, lo:lo + V7X_LANES] = (kn * r * gkn_ref[...]).astype(bf16)
        ko_ref[:, lo + V7X_LANES:lo + MLA_QPAD] = (kr_rot * r).astype(bf16)
    vo_ref[...] = kv[:, nv:].astype(bf16)


def mla_prep(p, kv_g, wkvb_p, gq_pad, gk_nope, gk_rope, cos_c, sin_c, tm):
    t = p.shape[0]
    qw = MLA_HEADS * MLA_QPAD
    vw = MLA_HEADS * MLA_V
    tab = pl.BlockSpec((tm, V7X_LANES), lambda i: (i, 0))

    def vec(n):
        return pl.BlockSpec((1, n), lambda i: (0, 0))

    return pl.pallas_call(
        _mla_prep_kernel, grid=(t // tm,),
        in_specs=[pl.BlockSpec((tm, qw), lambda i: (i, QC_OFF // qw)),
                  pl.BlockSpec((tm, MLA_RANK), lambda i: (i, CKV_OFF // MLA_RANK)),
                  pl.BlockSpec((tm, V7X_LANES), lambda i: (i, KR_OFF // V7X_LANES)),
                  vec(MLA_RANK), pl.BlockSpec((MLA_RANK, 2 * vw), lambda i: (0, 0)),
                  vec(MLA_QPAD), vec(V7X_LANES), vec(V7X_LANES), tab, tab],
        out_specs=[pl.BlockSpec((tm, qw), lambda i: (i, 0)), pl.BlockSpec((tm, qw), lambda i: (i, 0)),
                   pl.BlockSpec((tm, vw), lambda i: (i, 0))],
        out_shape=[jax.ShapeDtypeStruct((t, qw), bf16), jax.ShapeDtypeStruct((t, qw), bf16),
                   jax.ShapeDtypeStruct((t, vw), bf16)],
        compiler_params=_cparams(("arbitrary",), 48), name="mla_prep",
    )(p, p, p, kv_g.reshape(1, -1), wkvb_p, gq_pad.reshape(1, -1), gk_nope.reshape(1, -1),
      gk_rope.reshape(1, -1), cos_c, sin_c)


FLASH_COLS = 256


def _scores(k, qt_ref, s_ref):
    s_ref[0:k.shape[0], :] = jnp.dot(k, qt_ref[...], preferred_element_type=f32)


def _softmax_pv(s_ref, tk, v, m_ref, l_ref, acc_ref):
    vt = v.T
    step = min(FLASH_COLS, s_ref.shape[1])
    for c0 in range(0, s_ref.shape[1], step):
        cols = slice(c0, c0 + step)
        s = s_ref[0:tk, cols]
        m_prev = m_ref[:, cols]
        m_new = jnp.maximum(m_prev, jnp.max(s, axis=0, keepdims=True))
        alpha = jnp.exp2(m_prev - m_new)
        pr = jnp.exp2(s - m_new)
        l_ref[:, cols] = alpha * l_ref[:, cols] + jnp.sum(pr, axis=0, keepdims=True)
        acc_ref[:, cols] = alpha * acc_ref[:, cols] + jnp.dot(vt, pr.astype(bf16), preferred_element_type=f32)
        m_ref[:, cols] = m_new


def _flash_kernel(*refs, group, dq, dv, tq, tk, n_lat):
    if n_lat:
        q_ref, kl_ref, vl_ref, kc_ref, vc_ref, o_ref, qt_ref, m_ref, l_ref, acc_ref, s0_ref, s1_ref = refs
    else:
        q_ref, kc_ref, vc_ref, o_ref, qt_ref, m_ref, l_ref, acc_ref, s0_ref, s1_ref = refs
    for g in range(group):
        qt_ref[:, g * tq:(g + 1) * tq] = q_ref[:, g * dq:(g + 1) * dq].T
    m_ref[...] = jnp.full(m_ref.shape, -1e30, f32)
    l_ref[...] = jnp.zeros(l_ref.shape, f32)
    acc_ref[...] = jnp.zeros(acc_ref.shape, f32)
    n_ctx = kc_ref.shape[0]
    _scores(kc_ref[...], qt_ref, s1_ref)
    if n_lat:
        def kchunk(j):
            return kl_ref[j * tk:(j + 1) * tk, :]

        def vchunk(j):
            return vl_ref[j * tk:(j + 1) * tk, :]

        assert n_lat % 2 == 0
        _scores(kchunk(0), qt_ref, s0_ref)
        _softmax_pv(s1_ref, n_ctx, vc_ref[...], m_ref, l_ref, acc_ref)
        for i in range(n_lat // 2 - 1):
            _scores(kchunk(2 * i + 1), qt_ref, s1_ref)
            _softmax_pv(s0_ref, tk, vchunk(2 * i), m_ref, l_ref, acc_ref)
            _scores(kchunk(2 * i + 2), qt_ref, s0_ref)
            _softmax_pv(s1_ref, tk, vchunk(2 * i + 1), m_ref, l_ref, acc_ref)
        _scores(kchunk(n_lat - 1), qt_ref, s1_ref)
        _softmax_pv(s0_ref, tk, vchunk(n_lat - 2), m_ref, l_ref, acc_ref)
        _softmax_pv(s1_ref, tk, vchunk(n_lat - 1), m_ref, l_ref, acc_ref)
    else:
        _softmax_pv(s1_ref, n_ctx, vc_ref[...], m_ref, l_ref, acc_ref)
    for g in range(group):
        cols = slice(g * tq, (g + 1) * tq)
        o_ref[:, g * dv:(g + 1) * dv] = (acc_ref[:, cols] / l_ref[:, cols]).T.astype(o_ref.dtype)


def flash_attention(q, k, v, v_col_off, *, batch, seq, ctx_len, kv_heads, group, dq, dv, tq, tk, lat):
    ctx_blk0 = batch * seq // ctx_len
    voff = v_col_off // dv
    rows = group * tq
    kc_spec = pl.BlockSpec((ctx_len, dq), lambda b, h, i: (ctx_blk0 + b, h))
    vc_spec = pl.BlockSpec((ctx_len, dv), lambda b, h, i: (ctx_blk0 + b, voff + h))
    if lat:
        nq = seq // tq
        in_specs = [pl.BlockSpec((tq, group * dq), lambda b, h, i: (b * nq + i, h)),
                    pl.BlockSpec((seq, dq), lambda b, h, i: (b, h)),
                    pl.BlockSpec((seq, dv), lambda b, h, i: (b, voff + h)), kc_spec, vc_spec]
        args = [q, k, v, k, v]
        out_spec = pl.BlockSpec((tq, group * dv), lambda b, h, i: (b * nq + i, h))
        out_rows = batch * seq
        n_lat = seq // tk
    else:
        nq = 1
        assert tq == ctx_len
        in_specs = [pl.BlockSpec((tq, group * dq), lambda b, h, i: (ctx_blk0 + b, h)), kc_spec, vc_spec]
        args = [q, k, v]
        out_spec = pl.BlockSpec((tq, group * dv), lambda b, h, i: (b, h))
        out_rows = batch * ctx_len
        n_lat = 0
    return pl.pallas_call(
        functools.partial(_flash_kernel, group=group, dq=dq, dv=dv, tq=tq, tk=tk, n_lat=n_lat),
        grid=(batch, kv_heads, nq), in_specs=in_specs, out_specs=out_spec,
        out_shape=jax.ShapeDtypeStruct((out_rows, kv_heads * group * dv), bf16),
        scratch_shapes=[pltpu.VMEM((dq, rows), bf16), pltpu.VMEM((1, rows), f32), pltpu.VMEM((1, rows), f32),
                        pltpu.VMEM((dv, rows), f32)] + [pltpu.VMEM((max(tk, ctx_len) if lat else ctx_len, rows), f32)] * 2,
        compiler_params=_cparams(("arbitrary", "arbitrary", "arbitrary"), 48),
        name="flash_lat" if lat else "flash_ctx",
    )(*args)


CONV_HALO = 16


def _conv3_kernel(prev_ref, cur_ref, next_ref, w_ref, b_ref, o_ref, *, blocks_per_seq):
    rows = cur_ref.shape[0]
    pos = pl.program_id(0) % blocks_per_seq
    x = cur_ref[...].astype(f32)
    row = lax.broadcasted_iota(jnp.int32, x.shape, 0)
    prev_row = jnp.where(pos == 0, 0.0, prev_ref[...].astype(f32)[CONV_HALO - 1:CONV_HALO, :])
    next_row = jnp.where(pos == blocks_per_seq - 1, 0.0, next_ref[...].astype(f32)[0:1, :])
    xm = jnp.where(row == 0, prev_row, pltpu.roll(x, 1, 0))
    xp = jnp.where(row == rows - 1, next_row, pltpu.roll(x, rows - 1, 0))
    o_ref[...] = xm * w_ref[0:1, :] + x * w_ref[1:2, :] + xp * w_ref[2:3, :] + b_ref[...]


def short_conv(p, conv_w, conv_b, row0, nrows, seq_len, rows, cw):
    width = 3 * HY_C
    rb0 = row0 // rows
    sub = rows // CONV_HALO
    last_halo = p.shape[0] // CONV_HALO - 1
    c0 = HY_OFF // cw
    return pl.pallas_call(
        functools.partial(_conv3_kernel, blocks_per_seq=seq_len // rows),
        grid=(nrows // rows, width // cw),
        in_specs=[
            pl.BlockSpec((CONV_HALO, cw), lambda r, c: (jnp.maximum((rb0 + r) * sub - 1, 0), c0 + c)),
            pl.BlockSpec((rows, cw), lambda r, c: (rb0 + r, c0 + c)),
            pl.BlockSpec((CONV_HALO, cw), lambda r, c: (jnp.minimum((rb0 + r + 1) * sub, last_halo), c0 + c)),
            pl.BlockSpec((3, cw), lambda r, c: (0, c)),
            pl.BlockSpec((1, cw), lambda r, c: (0, c)),
        ],
        out_specs=pl.BlockSpec((rows, cw), lambda r, c: (r, c)),
        out_shape=jax.ShapeDtypeStruct((nrows, width), f32),
        compiler_params=_cparams(("arbitrary", "arbitrary"), 32), name="short_conv",
    )(p, p, p, conv_w, conv_b.reshape(1, width))


@functools.lru_cache(maxsize=None)
def _dft_tables(length):
    nb = 128 if length >= 1024 else 16
    n_fft = 2 * length
    na = n_fft // nb
    nah = na // 2
    lo = np.arange(nb, dtype=np.int64)[:, None, None]
    k1 = np.arange(na, dtype=np.int64)[None, :, None]
    hi = np.arange(nah, dtype=np.int64)[None, None, :]
    ang = 2.0 * np.pi * (((nb * hi + lo) * k1) % n_fft) / n_fft
    c, s = np.cos(ang), np.sin(ang)
    a1 = np.concatenate([np.concatenate([c, s], axis=2), np.concatenate([-s, c], axis=2)], axis=1)
    ct, st = np.swapaxes(c, 1, 2) / n_fft, np.swapaxes(s, 1, 2) / n_fft
    a3 = np.concatenate([np.concatenate([ct, -st], axis=2), np.concatenate([st, ct], axis=2)], axis=1)
    kk = np.arange(nb, dtype=np.int64)
    angb = 2.0 * np.pi * ((kk[:, None] * kk[None, :]) % nb) / nb
    cb, sb = np.cos(angb), np.sin(angb)
    mf = np.block([[cb, sb], [-sb, cb]])
    mfc = np.block([[cb, -sb], [sb, cb]])
    return dict(nb=nb, na=na, nah=nah,
                a1=jnp.asarray(a1, dtype=bf16), a1r=jnp.asarray(a1[:, :, :nah], dtype=bf16),
                a3=jnp.asarray(a3, dtype=bf16), mf=jnp.asarray(mf, dtype=bf16), mfc=jnp.asarray(mfc, dtype=bf16))


def _tap_times(j, nb, nah):
    ridx = lax.broadcasted_iota(jnp.int32, (V7X_SUBLANES * nah, 1), 0)
    return nb * (ridx & (nah - 1)) + (j * V7X_SUBLANES + (ridx >> (nah.bit_length() - 1)))


def _filter_mlp_kernel(band_ref, w1_ref, b1_ref, w2_ref, b2_ref, freq_ref, o_ref, *, length, nb, nah):
    hp = lax.Precision.HIGHEST
    t = _tap_times(pl.program_id(0), nb, nah).astype(f32)
    t_unit = t / float(max(length - 1, 1))
    lane = lax.broadcasted_iota(jnp.int32, (t.shape[0], V7X_LANES), 1)
    ang = ((2.0 * math.pi / length) * t) * band_ref[...]
    feats = jnp.where(lane == 0, t_unit,
                      jnp.where(lane <= HY_BANDS, jnp.cos(ang), jnp.where(lane <= 2 * HY_BANDS, -jnp.sin(ang), 0.0)))
    h = jnp.sin(freq_ref[0:1, :] * (jnp.dot(feats, w1_ref[...], precision=hp, preferred_element_type=f32) + b1_ref[...]))
    o_ref[...] = jnp.sin(freq_ref[1:2, :] * (jnp.dot(h, w2_ref[...], precision=hp, preferred_element_type=f32)
                                            + b2_ref[...]))


def filter_mlp(tabs, length, band_row, w1p, b1, w2, b2, freq):
    nb, nah = tabs["nb"], tabs["nah"]
    rows = V7X_SUBLANES * nah
    full = lambda *shape: pl.BlockSpec(shape, lambda j: (0,) * len(shape))
    return pl.pallas_call(
        functools.partial(_filter_mlp_kernel, length=length, nb=nb, nah=nah), grid=(nb // V7X_SUBLANES,),
        in_specs=[full(1, V7X_LANES), full(V7X_LANES, HY_FW), full(1, HY_FW), full(HY_FW, HY_FW), full(1, HY_FW),
                  full(2, HY_FW)],
        out_specs=pl.BlockSpec((rows, HY_FW), lambda j: (j, 0)),
        out_shape=jax.ShapeDtypeStruct((length, HY_FW), f32),
        compiler_params=_cparams(("arbitrary",), 32), name="hy_filter_mlp",
    )(band_row, w1p, b1, w2, b2, freq)


def _filter_stage1_kernel(a1r_ref, h_ref, w3_ref, delta_ref, o_ref, ss_ref, *, length, nb, nah):
    j = pl.program_id(1)
    t_int = _tap_times(j, nb, nah)
    t_unit = t_int.astype(f32) / float(max(length - 1, 1))
    decay = jnp.exp(-t_unit * delta_ref[...])
    hb = h_ref[...].astype(bf16)

    @pl.when(j == 0)
    def _():
        ss_ref[...] = jnp.zeros(ss_ref.shape, f32)

    na = 2 * nah
    for o in range(HY_ORDER):
        fwd = jnp.dot(hb, w3_ref[2 * o].astype(bf16), preferred_element_type=f32) * decay
        bwd = jnp.dot(hb, w3_ref[2 * o + 1].astype(bf16), preferred_element_type=f32) * decay
        bwd = jnp.where(t_int == 0, 0.0, bwd)
        ss_ref[o:o + 1, :] += jnp.sum(fwd * fwd + bwd * bwd, axis=0, keepdims=True)
        sb, db = (fwd + bwd).astype(bf16), (fwd - bwd).astype(bf16)
        for l in range(V7X_SUBLANES):
            a = a1r_ref[l]
            fs = jnp.dot(a, sb[l * nah:(l + 1) * nah], preferred_element_type=f32)
            fd = jnp.dot(a, db[l * nah:(l + 1) * nah], preferred_element_type=f32)
            o_ref[o, 0, l] = _pack_complex(fs[:na], fs[na:])
            o_ref[o, 1, l] = _pack_complex(fd[:na], fd[na:])


def filter_stage1(tabs, length, hmlp, w3r, delta_row):
    nb, na, nah = tabs["nb"], tabs["na"], tabs["nah"]
    ncb = HY_C // V7X_LANES
    return pl.pallas_call(
        functools.partial(_filter_stage1_kernel, length=length, nb=nb, nah=nah),
        grid=(ncb, nb // V7X_SUBLANES),
        in_specs=[pl.BlockSpec((V7X_SUBLANES, 2 * na, nah), lambda c, j: (j, 0, 0)),
                  pl.BlockSpec((V7X_SUBLANES * nah, HY_FW), lambda c, j: (j, 0)),
                  pl.BlockSpec((2 * HY_ORDER, HY_FW, V7X_LANES), lambda c, j: (0, 0, c)),
                  pl.BlockSpec((1, V7X_LANES), lambda c, j: (0, c))],
        out_specs=[pl.BlockSpec((None, HY_ORDER, 2, V7X_SUBLANES, na, V7X_LANES), lambda c, j: (c, 0, 0, j, 0, 0)),
                   pl.BlockSpec((V7X_SUBLANES, V7X_LANES), lambda c, j: (0, c))],
        out_shape=[jax.ShapeDtypeStruct((ncb, HY_ORDER, 2, nb, na, V7X_LANES), u32),
                   jax.ShapeDtypeStruct((V7X_SUBLANES, HY_C), f32)],
        compiler_params=_cparams(("arbitrary", "arbitrary"), 48), name="hy_filter_s1",
    )(tabs["a1r"], hmlp, w3r, delta_row)


def _tile_rows(ref, l):
    tiles = math.prod(ref.shape[:-2])
    flat = ref.reshape(tiles * V7X_SUBLANES, ref.shape[-1])
    return flat.at[pl.ds(l, tiles, stride=V7X_SUBLANES), :]


def _pack_complex(re, im):
    def rne(x):
        b = pltpu.bitcast(x, u32)
        return b + (jnp.uint32(0x7FFF) + ((b >> 16) & jnp.uint32(1)))
    return (rne(re) & jnp.uint32(0xFFFF0000)) | (rne(im) >> 16)


def _unpack_complex(w):
    return pltpu.bitcast(w & jnp.uint32(0xFFFF0000), f32), pltpu.bitcast(w << 16, f32)


def _filter_stage2_kernel(x_ref, m_ref, o_ref, *, nb):
    def part(a, p):
        return jnp.concatenate([a[(2 * o + p) * nb:(2 * o + p + 1) * nb] for o in range(HY_ORDER)], axis=1)

    for kl in range(V7X_SUBLANES):
        re, im = _unpack_complex(_tile_rows(x_ref, kl)[...])
        rs = jnp.concatenate([part(re, 0), part(im, 0)], axis=0).astype(bf16)
        rd = jnp.concatenate([part(re, 1), part(im, 1)], axis=0).astype(bf16)
        h_re = jnp.dot(m_ref[0:nb, :], rs, preferred_element_type=f32)
        h_im = jnp.dot(m_ref[nb:, :], rd, preferred_element_type=f32)
        packed = _pack_complex(h_re, h_im)
        for o in range(HY_ORDER):
            o_ref[o, kl] = packed[:, o * V7X_LANES:(o + 1) * V7X_LANES]


def filter_stage2(tabs, hin):
    nb, na = tabs["nb"], tabs["na"]
    ncb = HY_C // V7X_LANES
    return pl.pallas_call(
        functools.partial(_filter_stage2_kernel, nb=nb), grid=(ncb, na // V7X_SUBLANES),
        in_specs=[pl.BlockSpec((None, HY_ORDER, 2, nb, V7X_SUBLANES, V7X_LANES), lambda c, j: (c, 0, 0, 0, j, 0)),
                  pl.BlockSpec((2 * nb, 2 * nb), lambda c, j: (0, 0))],
        out_specs=pl.BlockSpec((None, HY_ORDER, V7X_SUBLANES, nb, V7X_LANES), lambda c, j: (c, 0, j, 0, 0)),
        out_shape=jax.ShapeDtypeStruct((ncb, HY_ORDER, na, nb, V7X_LANES), u32),
        compiler_params=_cparams(("arbitrary", "arbitrary"), 48), name="hy_filter_s2",
    )(hin, tabs["mf"])


def _conv_stage1_kernel(zr_ref, zi_ref, a1_ref, o_ref, *, na):
    for l in range(V7X_SUBLANES):
        rhs = jnp.concatenate([_tile_rows(zr_ref, l)[...], _tile_rows(zi_ref, l)[...]], axis=0).astype(bf16)
        y = jnp.dot(a1_ref[l], rhs, preferred_element_type=f32)
        o_ref[l] = _pack_complex(y[:na], y[na:])


def conv_stage1(tabs, z4, col_off):
    nb, na, nah = tabs["nb"], tabs["na"], tabs["nah"]
    ncb = HY_C // V7X_LANES
    c0 = col_off // V7X_LANES
    return pl.pallas_call(
        functools.partial(_conv_stage1_kernel, na=na), grid=(ncb, nb // V7X_SUBLANES),
        in_specs=[pl.BlockSpec((None, nah, V7X_SUBLANES, V7X_LANES), lambda c, j: (0, 0, j, c0 + c)),
                  pl.BlockSpec((None, nah, V7X_SUBLANES, V7X_LANES), lambda c, j: (1, 0, j, c0 + c)),
                  pl.BlockSpec((V7X_SUBLANES, 2 * na, na), lambda c, j: (j, 0, 0))],
        out_specs=pl.BlockSpec((None, V7X_SUBLANES, na, V7X_LANES), lambda c, j: (c, j, 0, 0)),
        out_shape=jax.ShapeDtypeStruct((ncb, nb, na, V7X_LANES), u32),
        compiler_params=_cparams(("arbitrary", "arbitrary"), 48), name="hy_conv_s1",
    )(z4, z4, tabs["a1"])


def _conv_stage2_kernel(x_ref, h_ref, mf_ref, mfc_ref, o_ref, *, nb):
    def lanes(a):
        return jnp.concatenate([a[:nb], a[nb:]], axis=1)

    for kl in range(V7X_SUBLANES):
        re, im = _unpack_complex(_tile_rows(x_ref, kl)[...])
        rhs = jnp.concatenate([lanes(re), lanes(im)], axis=0).astype(bf16)
        x = jnp.dot(mf_ref[...], rhs, preferred_element_type=f32)
        xr, xi = x[:nb], x[nb:]
        hr0, hi0 = _unpack_complex(h_ref[0, kl])
        hr1, hi1 = _unpack_complex(h_ref[1, kl])
        hr, hi = jnp.concatenate([hr0, hr1], axis=1), jnp.concatenate([hi0, hi1], axis=1)
        y = jnp.concatenate([xr * hr - xi * hi, xr * hi + xi * hr], axis=0).astype(bf16)
        e = jnp.dot(mfc_ref[...], y, preferred_element_type=f32)
        packed = _pack_complex(e[:nb], e[nb:])
        o_ref[0, kl] = packed[:, :V7X_LANES]
        o_ref[1, kl] = packed[:, V7X_LANES:]


def conv_stage2(tabs, din, hspec, order):
    nb, na = tabs["nb"], tabs["na"]
    ncb = HY_C // V7X_LANES
    return pl.pallas_call(
        functools.partial(_conv_stage2_kernel, nb=nb), grid=(ncb // 2, na // V7X_SUBLANES),
        in_specs=[pl.BlockSpec((2, nb, V7X_SUBLANES, V7X_LANES), lambda c, j: (c, 0, j, 0)),
                  pl.BlockSpec((2, None, V7X_SUBLANES, nb, V7X_LANES), lambda c, j: (c, order, j, 0, 0)),
                  pl.BlockSpec((2 * nb, 2 * nb), lambda c, j: (0, 0)),
                  pl.BlockSpec((2 * nb, 2 * nb), lambda c, j: (0, 0))],
        out_specs=pl.BlockSpec((2, V7X_SUBLANES, nb, V7X_LANES), lambda c, j: (c, j, 0, 0)),
        out_shape=jax.ShapeDtypeStruct((ncb, na, nb, V7X_LANES), u32),
        compiler_params=_cparams(("arbitrary", "arbitrary"), 48), name="hy_conv_s2",
    )(din, hspec, tabs["mf"], tabs["mfc"])


def _conv_stage3_kernel(e_ref, a3_ref, zr_ref, zi_ref, gr_ref, gi_ref, ss_ref, skip_ref, o_ref, *, order):
    rs = lax.rsqrt(ss_ref[order:order + 1, :] + EPS)
    skip = skip_ref[order:order + 1, :]
    for l in range(V7X_SUBLANES):
        re, im = _unpack_complex(_tile_rows(e_ref, l)[...])
        rhs = jnp.concatenate([re, im], axis=0).astype(bf16)
        y = jnp.dot(a3_ref[l], rhs, preferred_element_type=f32) * rs
        z = jnp.concatenate([_tile_rows(zr_ref, l)[...], _tile_rows(zi_ref, l)[...]], axis=0)
        g = jnp.concatenate([_tile_rows(gr_ref, l)[...], _tile_rows(gi_ref, l)[...]], axis=0)
        _tile_rows(o_ref, l)[...] = g * (y + skip * z)


def conv_stage3(tabs, ein, z4, z_off, g4, g_off, ss, skip, order):
    nb, na, nah = tabs["nb"], tabs["na"], tabs["nah"]
    ncb = HY_C // V7X_LANES
    zc, gc = z_off // V7X_LANES, g_off // V7X_LANES

    def slab(b, c0):
        return pl.BlockSpec((None, nah, V7X_SUBLANES, V7X_LANES), lambda c, j: (b, 0, j, c0 + c))

    return pl.pallas_call(
        functools.partial(_conv_stage3_kernel, order=order), grid=(ncb, nb // V7X_SUBLANES),
        in_specs=[pl.BlockSpec((None, na, V7X_SUBLANES, V7X_LANES), lambda c, j: (c, 0, j, 0)),
                  pl.BlockSpec((V7X_SUBLANES, na, 2 * na), lambda c, j: (j, 0, 0)),
                  slab(0, zc), slab(1, zc), slab(0, gc), slab(1, gc),
                  pl.BlockSpec((V7X_SUBLANES, V7X_LANES), lambda c, j: (0, c)),
                  pl.BlockSpec((HY_ORDER, V7X_LANES), lambda c, j: (0, c))],
        out_specs=pl.BlockSpec((2, nah, V7X_SUBLANES, V7X_LANES), lambda c, j: (0, 0, j, c)),
        out_shape=jax.ShapeDtypeStruct((2, nah, nb, HY_C), f32),
        compiler_params=_cparams(("arbitrary", "arbitrary"), 48), name="hy_conv_s3",
    )(ein, tabs["a3"], z4, z4, g4, g4, ss, skip)


def hyena_mixer(uc, length, hf, skip):
    tabs = _dft_tables(length)
    nb, nah = tabs["nb"], tabs["nah"]
    band_row, w1p, b1, w2, b2, w3r, freq, delta_row = hf
    hmlp = filter_mlp(tabs, length, band_row, w1p, b1, w2, b2, freq)
    hin, ss = filter_stage1(tabs, length, hmlp, w3r, delta_row)
    hspec = filter_stage2(tabs, hin)
    u4 = uc.reshape(2, nah, nb, 3 * HY_C)
    z4, z_off = u4, 0
    for o in range(HY_ORDER):
        din = conv_stage1(tabs, z4, z_off)
        ein = conv_stage2(tabs, din, hspec, o)
        z4 = conv_stage3(tabs, ein, z4, z_off, u4, (o + 1) * HY_C, ss, skip, o)
        z_off = 0
    return z4.reshape(2 * length, HY_C)


def _merge_kernel(gl_ref, al_ref, ac_ref, hl_ref, hc_ref, cl_ref, cc_ref, wg_ref, bg_ref, wb_ref, wo_ref,
                  o_ref, acc_ref, br_ref, *, lat_blocks):
    i, j = pl.program_id(0), pl.program_id(1)

    @pl.when(j == 0)
    def _():
        acc_ref[...] = jnp.zeros(acc_ref.shape, f32)

    @pl.when((j == 0) & (i < lat_blocks))
    def _():
        br_ref[0] = al_ref[...]
        br_ref[1] = hl_ref[...].astype(bf16)
        br_ref[2] = cl_ref[...]

    @pl.when((j == 0) & (i >= lat_blocks))
    def _():
        br_ref[0] = ac_ref[...]
        br_ref[1] = hc_ref[...].astype(bf16)
        br_ref[2] = cc_ref[...]

    gl = gl_ref[...]
    y = None
    for n in range(N_BRANCH):
        z = jnp.dot(gl, wg_ref[n], preferred_element_type=f32) + bg_ref[n]
        g = 1.0 / (1.0 + jnp.exp(-z))
        term = g * jnp.dot(br_ref[n], wb_ref[n], preferred_element_type=f32)
        y = term if y is None else y + term
    acc_ref[...] += jnp.dot(y.astype(bf16), wo_ref[...], preferred_element_type=f32)

    @pl.when(j == pl.num_programs(1) - 1)
    def _():
        o_ref[...] = acc_ref[...].astype(o_ref.dtype)


def merge_out(p, a, hy, c, wg, bg, wb, wo, layer, tm, tj):
    t = p.shape[0]
    d = wo.shape[2]
    nl = a[0].shape[0] // tm
    assert a[1].shape[0] % tm == 0 and t == a[0].shape[0] + a[1].shape[0]
    lat = pl.BlockSpec((tm, BRANCH_W), lambda i, j: (jnp.minimum(i, nl - 1), 0))
    ctx = pl.BlockSpec((tm, BRANCH_W), lambda i, j: (jnp.maximum(i - nl, 0), 0))
    return pl.pallas_call(
        functools.partial(_merge_kernel, lat_blocks=nl), grid=(t // tm, d // tj),
        in_specs=[pl.BlockSpec((tm, GATE_RANK), lambda i, j: (i, GATE_OFF // GATE_RANK)),
                  lat, ctx, lat, ctx, lat, ctx,
                  pl.BlockSpec((None, N_BRANCH, GATE_RANK, tj), lambda i, j: (layer, 0, 0, j)),
                  pl.BlockSpec((None, N_BRANCH, 1, tj), lambda i, j: (layer, 0, 0, j)),
                  pl.BlockSpec((None, N_BRANCH, BRANCH_W, tj), lambda i, j: (layer, 0, 0, j)),
                  pl.BlockSpec((None, tj, d), lambda i, j: (layer, j, 0))],
        out_specs=pl.BlockSpec((tm, d), lambda i, j: (i, 0)),
        out_shape=jax.ShapeDtypeStruct((t, d), bf16),
        scratch_shapes=[pltpu.VMEM((tm, d), f32), pltpu.VMEM((N_BRANCH, tm, BRANCH_W), bf16)],
        compiler_params=_cparams(("arbitrary", "arbitrary"), 58), name="merge_out",
    )(p, a[0], a[1], hy[0], hy[1], c[0], c[1], wg, bg, wb, wo)


def _ffn_kernel(h_ref, wg_ref, wu_ref, wd_ref, o_ref, acc_ref):
    j = pl.program_id(1)

    @pl.when(j == 0)
    def _():
        acc_ref[...] = jnp.zeros(acc_ref.shape, f32)

    h = h_ref[...]
    g = jnp.dot(h, wg_ref[...], preferred_element_type=f32)
    u = jnp.dot(h, wu_ref[...], preferred_element_type=f32)
    a = (g * (1.0 / (1.0 + jnp.exp(-g))) * u).astype(bf16)
    acc_ref[...] += jnp.dot(a, wd_ref[...], preferred_element_type=f32)

    @pl.when(j == pl.num_programs(1) - 1)
    def _():
        o_ref[...] = acc_ref[...].astype(o_ref.dtype)


def ffn(h, w_gu, w_down, layer, tm, th):
    t, d = h.shape
    hidden = w_down.shape[1]
    nh = hidden // th
    return pl.pallas_call(
        _ffn_kernel, grid=(t // tm, nh),
        in_specs=[pl.BlockSpec((tm, d), lambda i, j: (i, 0)),
                  pl.BlockSpec((None, d, th), lambda i, j: (layer, 0, j)),
                  pl.BlockSpec((None, d, th), lambda i, j: (layer, 0, nh + j)),
                  pl.BlockSpec((None, th, d), lambda i, j: (layer, j, 0))],
        out_specs=pl.BlockSpec((tm, d), lambda i, j: (i, 0)),
        out_shape=jax.ShapeDtypeStruct((t, d), bf16),
        scratch_shapes=[pltpu.VMEM((tm, d), f32)],
        compiler_params=_cparams(("arbitrary", "arbitrary"), 56), name="ffn",
    )(h, w_gu, w_gu, w_down)


def _pack_w_in(w_in):
    depth, d, _ = w_in.shape
    o_qa, o_ka, o_hy = 0, GQA_HEADS * HEAD_DIM, (GQA_HEADS + 2 * GQA_KV_HEADS) * HEAD_DIM
    o_qc = o_hy + 3 * HY_C
    o_ckv = o_qc + MLA_HEADS * MLA_QK
    o_kr = o_ckv + MLA_RANK
    o_gate = o_kr + MLA_ROPE
    qc = w_in[:, :, o_qc:o_ckv].reshape(depth, d, MLA_HEADS, MLA_QK)
    qc = jnp.pad(qc, ((0, 0), (0, 0), (0, 0), (0, MLA_QPAD - MLA_QK))).reshape(depth, d, MLA_HEADS * MLA_QPAD)
    tail = NP_COLS - KR_OFF - MLA_ROPE
    parts = [qc, w_in[:, :, o_qa:o_ka], w_in[:, :, o_ka:o_hy], w_in[:, :, o_hy:o_qc], w_in[:, :, o_ckv:o_kr],
             w_in[:, :, o_gate:o_gate + GATE_RANK], w_in[:, :, o_kr:o_gate], jnp.zeros((depth, d, tail), w_in.dtype)]
    return jnp.concatenate(parts, axis=2).astype(bf16)


def _pack_w_kvb(w):
    depth = w.shape[0]
    w = w.reshape(depth, MLA_RANK, MLA_HEADS, 2, MLA_NOPE)
    return jnp.swapaxes(w, 2, 3).reshape(depth, MLA_RANK, 2 * MLA_HEADS * MLA_NOPE).astype(bf16)


def kernel(x, c, ctx, c_ctx, norm1_g, norm2_g, ada_down, ada_up, ada_b, w_in, gqa_q_norm, gqa_k_norm, hy_conv_w,
           hy_conv_b, hf_w1, hf_b1, hf_w2, hf_b2, hf_w3, hf_freq, hy_skip, mla_kv_norm, mla_w_kvb, mla_q_norm,
           mla_k_norm, w_gate_up, b_gate, w_branch, w_out, ffn_w_gu, ffn_w_down):
    batch, seq, d = x.shape
    ctx_len = ctx.shape[1]
    depth = w_in.shape[0]
    hidden = ffn_w_down.shape[1]
    assert batch == 2, "the Hyena long convolution packs the two batches as one complex sequence"
    assert w_in.shape[2] == IN_COLS and seq % GRID_W == 0
    n_lat, n_ctx = batch * seq, batch * ctx_len
    t_all = n_lat + n_ctx

    tm = _row_tile(seq, n_ctx, (512, 256, 128))
    tm_ew = _row_tile(seq, n_ctx, (256, 128))
    tm_mm = 768 if t_all % 768 == 0 else tm
    tm_in, tn_in = tm_mm, 1536
    tj = 512
    th = 256
    tq_a = min(256, ctx_len)
    tq_c = min(1024, seq)
    tk = min(512, seq)

    def grp(tile):
        per = seq // tile
        return lambda i: jnp.minimum(i // per, batch)

    w_in_p = _pack_w_in(w_in)
    w_kvb_p = _pack_w_kvb(mla_w_kvb)
    wg_b, wb_b, wo_b = w_gate_up.astype(bf16), w_branch.astype(bf16), w_out.astype(bf16)
    wgu_b, wd_b = ffn_w_gu.astype(bf16), ffn_w_down.astype(bf16)
    bg = b_gate.reshape(depth, N_BRANCH, 1, d)
    gq_pad = jnp.pad(mla_q_norm, ((0, 0), (0, MLA_QPAD - MLA_QK)))
    gk_nope = mla_k_norm[:, :MLA_NOPE]
    gk_rope = jnp.pad(mla_k_norm[:, MLA_NOPE:], ((0, 0), (0, V7X_LANES - MLA_ROPE)))
    w1p = jnp.pad(hf_w1, ((0, 0), (0, V7X_LANES - HY_EMB), (0, 0)))
    w3r = jnp.swapaxes(hf_w3.reshape(depth, HY_FW, 2 * HY_ORDER, HY_C), 1, 2)
    bands = jnp.linspace(1e-4, HY_BANDS - 1, HY_BANDS, dtype=f32)
    band_row = jnp.concatenate([jnp.zeros((1,), f32), bands, bands,
                                jnp.zeros((V7X_LANES - HY_EMB,), f32)]).reshape(1, V7X_LANES)
    delta_row = jnp.abs(jnp.linspace(math.log(HY_DECAY_TARGET) / HY_DECAY_PCT_MIN,
                                     math.log(HY_DECAY_TARGET) / HY_DECAY_PCT_MAX, HY_C, dtype=f32)).reshape(1, HY_C)

    xs = jnp.concatenate([x.reshape(n_lat, d), ctx.reshape(n_ctx, d)], axis=0)
    tpos = jnp.arange(seq, dtype=jnp.int32)
    zpad = jnp.zeros((n_ctx,), f32)
    pos_row = jnp.concatenate([jnp.tile((tpos // GRID_W).astype(f32), batch), zpad]).reshape(t_all, 1)
    pos_col = jnp.concatenate([jnp.tile((tpos % GRID_W).astype(f32), batch), zpad]).reshape(t_all, 1)
    cond8 = jnp.concatenate([c, c_ctx[None, :], jnp.zeros((8 - batch - 1, d), f32)], axis=0)

    mod = ada_modulation(cond8, ada_down, ada_up, ada_b)
    mod5 = mod.reshape(depth, 8, N_MOD, 1, d)
    cos_a, sin_a, cos_c, sin_c = rope_tables(pos_row, pos_col, tm_ew)

    delta = None
    for l in range(depth):
        if l == 0:
            (h,) = prenorm(xs, mod5, l, grp(tm_ew), tm_ew, norm_g=norm1_g[l], shift_idx=0, scale_idx=1)
        else:
            xs, h = prenorm(xs, mod5, l, grp(tm_ew), tm_ew, delta=delta, gate_idx=5, gate_layer=l - 1,
                            norm_g=norm1_g[l], shift_idx=0, scale_idx=1)
        p = matmul(h, w_in_p, l, tm_in, tn_in)

        qa, ka = gqa_prep(p, gqa_q_norm[l], gqa_k_norm[l], cos_a, sin_a, tm_ew)
        fa = dict(batch=batch, seq=seq, ctx_len=ctx_len, kv_heads=GQA_KV_HEADS, group=GQA_GROUP,
                  dq=HEAD_DIM, dv=HEAD_DIM, tk=tk)
        att_a = (flash_attention(qa, ka, p, VA_OFF, tq=tq_a, lat=True, **fa),
                 flash_attention(qa, ka, p, VA_OFF, tq=ctx_len, lat=False, **fa))

        qc, kc, vc = mla_prep(p, mla_kv_norm[l], w_kvb_p[l], gq_pad[l], gk_nope[l], gk_rope[l], cos_c, sin_c, tm_ew)
        fc = dict(batch=batch, seq=seq, ctx_len=ctx_len, kv_heads=MLA_HEADS, group=1, dq=MLA_QPAD, dv=MLA_V, tk=tk)
        att_c = (flash_attention(qc, kc, vc, 0, tq=tq_c, lat=True, **fc),
                 flash_attention(qc, kc, vc, 0, tq=ctx_len, lat=False, **fc))

        hf = (band_row, w1p[l], hf_b1[l].reshape(1, -1), hf_w2[l], hf_b2[l].reshape(1, -1), w3r[l], hf_freq[l],
              delta_row)
        uc_lat = short_conv(p, hy_conv_w[l], hy_conv_b[l], 0, n_lat, seq, min(512, seq), 512)
        uc_ctx = short_conv(p, hy_conv_w[l], hy_conv_b[l], n_lat, n_ctx, ctx_len, min(256, ctx_len), 512)
        hy = (hyena_mixer(uc_lat, seq, hf, hy_skip[l]), hyena_mixer(uc_ctx, ctx_len, hf, hy_skip[l]))

        delta = merge_out(p, att_a, hy, att_c, wg_b, bg, wb_b, wo_b, l, tm, tj)
        xs, h2 = prenorm(xs, mod5, l, grp(tm_ew), tm_ew, delta=delta, gate_idx=2, gate_layer=l,
                         norm_g=norm2_g[l], shift_idx=3, scale_idx=4)
        delta = ffn(h2, wgu_b, wd_b, l, tm_mm, th)

    (out,) = prenorm(xs, mod5, depth - 1, grp(tm_ew), tm_ew, delta=delta, gate_idx=5, gate_layer=depth - 1,
                     rows=n_lat)
    return out.reshape(batch, seq, d)
```
